```python
import math
import jax, jax.numpy as jnp
from jax import lax
import numpy as np

D_MODEL = 1024
BATCH = 4
SEQ = 4096
DEPTH = 2
DEC_BATCH = 128
DEC_SEQ = 4
PAST_LEN = 2048
PAGE_SIZE = 128

CONV_W = D_MODEL // 4
POOL_W = D_MODEL // 4
N_HEADS = 8
HEAD_DIM = (D_MODEL - CONV_W - POOL_W) // N_HEADS
NSA_W = N_HEADS * HEAD_DIM
KV_HEADS = 2
GQA_GROUP = N_HEADS // KV_HEADS
KV_W = KV_HEADS * HEAD_DIM
CONV_WIDTH = 31
CONV_BUF = CONV_WIDTH - 1
POOL_WINDOWS = (2, 4, 8, 16)
POOL_GROUP_W = POOL_W // len(POOL_WINDOWS)
POOL_BUF = max(POOL_WINDOWS) - 1
CMP_STRIDE = 16
CMP_BLOCK = 2 * CMP_STRIDE
SLC_BLOCK = 64
SLC_TOPK = 16
WINDOW = 512
Q_BLOCK = 128
NUM_BUCKETS = 32
MAX_DISTANCE = 128
D_FF = -(-(8 * D_MODEL) // (3 * 256)) * 256
EPS = 1e-6
NEG = -1e30
FORCE = 1e4

OFF_POOL = 2 * CONV_W
OFF_Q = OFF_POOL + POOL_W
OFF_KV = OFF_Q + NSA_W
OFF_GATE = OFF_KV + 6 * KV_W
IN_W = OFF_GATE + 3 * N_HEADS

kernel_name = 'hybrid_conv_pool_nsa_decoder_step'


def rmsnorm(x, g):
    xf = x.astype(jnp.float32)
    y = xf * lax.rsqrt(jnp.mean(xf * xf, axis=-1, keepdims=True) + EPS)
    return (y * g.astype(jnp.float32)).astype(x.dtype)


def layernorm(x, g, b):
    xf = x.astype(jnp.float32)
    mu = jnp.mean(xf, axis=-1, keepdims=True)
    xc = xf - mu
    var = jnp.mean(xc * xc, axis=-1, keepdims=True)
    return (xc * lax.rsqrt(var + EPS) * g.astype(jnp.float32) + b.astype(jnp.float32)).astype(x.dtype)


def t5_bucket(dist):
    n = jnp.maximum(dist, 0)
    max_exact = NUM_BUCKETS // 2
    nf = jnp.maximum(n, 1).astype(jnp.float32)
    large = max_exact + (jnp.log(nf / max_exact) / math.log(MAX_DISTANCE / max_exact)
                         * (NUM_BUCKETS - max_exact)).astype(jnp.int32)
    large = jnp.minimum(large, NUM_BUCKETS - 1)
    return jnp.where(n < max_exact, n, large)


def masked_softmax(s, mask):
    s = jnp.where(mask, s.astype(jnp.float32), NEG)
    p = jax.nn.softmax(s, axis=-1)
    return p * jnp.any(mask, axis=-1, keepdims=True)


def project(h, w_in):
    B, T, _ = h.shape
    z = h @ w_in
    glu = z[..., :CONV_W] * jax.nn.sigmoid(z[..., CONV_W:OFF_POOL])
    u = z[..., OFF_POOL:OFF_Q]
    q = z[..., OFF_Q:OFF_KV].reshape(B, T, KV_HEADS, GQA_GROUP, HEAD_DIM)
    kv = z[..., OFF_KV:OFF_GATE].reshape(B, T, 3, 2, KV_HEADS, HEAD_DIM)
    gates = jax.nn.sigmoid(z[..., OFF_GATE:]).reshape(B, T, 3, KV_HEADS, GQA_GROUP, 1)
    return glu, u, q, kv, gates


def conv_mix(glu_ext, dw, b, ln_g, ln_b):
    y = lax.conv_general_dilated(glu_ext, dw[:, None, :], window_strides=(1,), padding='VALID',
                                 dimension_numbers=('NWC', 'WIO', 'NWC'),
                                 feature_group_count=CONV_W)
    return jax.nn.silu(layernorm(y + b, ln_g, ln_b))


def pool_mix(u_ext, pos0, w_pool, scale):
    B, L, _ = u_ext.shape
    T = L - POOL_BUF
    cs = jnp.pad(jnp.cumsum(u_ext.astype(jnp.float32), axis=1), ((0, 0), (1, 0), (0, 0)))
    end = cs[:, POOL_BUF + 1:POOL_BUF + 1 + T]
    u = u_ext[:, POOL_BUF:].astype(jnp.float32)
    t = pos0 + jnp.arange(T, dtype=jnp.int32)
    diffs = []
    for g, w in enumerate(POOL_WINDOWS):
        sl = slice(g * POOL_GROUP_W, (g + 1) * POOL_GROUP_W)
        start = cs[:, POOL_BUF + 1 - w:POOL_BUF + 1 - w + T, sl]
        cnt = jnp.minimum(w, t + 1).astype(jnp.float32)[:, None]
        diffs.append((end[..., sl] - start) / cnt - u[..., sl])
    d = jnp.stack(diffs, axis=2)
    y = jnp.einsum('btgc,gce->btge', d, w_pool.astype(jnp.float32)).reshape(B, T, POOL_W)
    return (y * scale.astype(jnp.float32)).astype(u_ext.dtype)


def compress(k, pe, w):
    B, L = k.shape[:2]
    Lp = -(-L // CMP_STRIDE) * CMP_STRIDE
    k = jnp.pad(k, ((0, 0), (0, Lp - L), (0, 0), (0, 0)))
    sub = k.reshape(B, Lp // CMP_STRIDE, CMP_STRIDE, KV_HEADS, HEAD_DIM)
    blocks = jnp.concatenate([sub[:, :-1], sub[:, 1:]], axis=2)
    return jnp.einsum('bnlgd,lde->bnge', blocks + pe[:, None, :], w)


def to_blocks(k):
    B, L = k.shape[:2]
    Lp = -(-L // SLC_BLOCK) * SLC_BLOCK
    k = jnp.pad(k, ((0, 0), (0, Lp - L), (0, 0), (0, 0)))
    return k.reshape(B, Lp // SLC_BLOCK, SLC_BLOCK, KV_HEADS, HEAD_DIM).transpose(0, 3, 1, 2, 4)


def nsa_attend(q, q_pos, kc, vc, ks, vs, kw, vw, kw_pos, tbl):
    B, Q = q.shape[:2]
    scale = HEAD_DIM ** -0.5
    n_cmp, n_slc = kc.shape[1], ks.shape[2]
    c_end = jnp.arange(n_cmp, dtype=jnp.int32) * CMP_STRIDE + (CMP_BLOCK - 1)
    dist = q_pos[:, None] - c_end[None, :]
    s = jnp.einsum('bqgrd,bngd->bgrqn', q, kc).astype(jnp.float32) * scale + tbl[:, :, t5_bucket(dist)]
    p_cmp = masked_softmax(s, dist >= 0)
    o_cmp = jnp.einsum('bgrqn,bngd->bqgrd', p_cmp.astype(vc.dtype), vc)
    c_start = jnp.arange(n_cmp, dtype=jnp.int32) * CMP_STRIDE
    s_start = jnp.arange(n_slc, dtype=jnp.int32) * SLC_BLOCK
    overlap = ((c_start[:, None] < s_start[None, :] + SLC_BLOCK)
               & (c_start[:, None] + CMP_BLOCK > s_start[None, :])).astype(jnp.float32)
    p_slc = jnp.einsum('bgrqn,nj->bgqj', p_cmp, overlap)
    blk = jnp.arange(n_slc, dtype=jnp.int32)[None, :]
    cur = (q_pos // SLC_BLOCK)[:, None]
    forced = (blk == 0) | (blk == cur) | (blk == cur - 1)
    score = jnp.where(forced, FORCE, jnp.where(blk <= cur, p_slc, NEG))
    _, idx = lax.top_k(score, min(SLC_TOPK, n_slc))
    n_sel = idx.shape[-1]
    M = n_sel * SLC_BLOCK
    bi = jnp.arange(B)[:, None, None, None]
    gi = jnp.arange(KV_HEADS)[None, :, None, None]
    kg = ks[bi, gi, idx].reshape(B, KV_HEADS, Q, M, HEAD_DIM)
    vg = vs[bi, gi, idx].reshape(B, KV_HEADS, Q, M, HEAD_DIM)
    pos = (idx[..., None] * SLC_BLOCK + jnp.arange(SLC_BLOCK, dtype=jnp.int32)).reshape(B, KV_HEADS, Q, M)
    dist = q_pos[None, None, :, None] - pos
    bias = tbl[gi[..., None], jnp.arange(GQA_GROUP)[None, None, :, None, None], t5_bucket(dist)[:, :, None]]
    s = jnp.einsum('bqgrd,bgqmd->bgrqm', q, kg).astype(jnp.float32) * scale + bias
    p = masked_softmax(s, (dist >= 0)[:, :, None])
    o_slc = jnp.einsum('bgrqm,bgqmd->bqgrd', p.astype(vg.dtype), vg)
    dist = q_pos[:, None] - kw_pos[None, :]
    mask = (dist >= 0) & (dist < WINDOW) & (kw_pos[None, :] >= 0)
    s = jnp.einsum('bqgrd,bkgd->bgrqk', q, kw).astype(jnp.float32) * scale + tbl[:, :, t5_bucket(dist)]
    p = masked_softmax(s, mask)
    o_win = jnp.einsum('bgrqk,bkgd->bqgrd', p.astype(vw.dtype), vw)
    return o_cmp, o_slc, o_win


def gate_sum(gates, o_cmp, o_slc, o_win):
    return gates[:, :, 0] * o_cmp + gates[:, :, 1] * o_slc + gates[:, :, 2] * o_win


def nsa_prompt(q, kv, gates, pe_k, wk, pe_v, wv, tbl):
    B, T = q.shape[:2]
    kc = compress(kv[:, :, 0, 0], pe_k, wk)
    vc = compress(kv[:, :, 0, 1], pe_v, wv)
    ks = to_blocks(kv[:, :, 1, 0])
    vs = to_blocks(kv[:, :, 1, 1])
    pad = ((0, 0), (WINDOW, 0), (0, 0), (0, 0))
    kw = jnp.pad(kv[:, :, 2, 0], pad)
    vw = jnp.pad(kv[:, :, 2, 1], pad)

    def block(i):
        q0 = i * Q_BLOCK
        qb = lax.dynamic_slice_in_dim(q, q0, Q_BLOCK, axis=1)
        gb = lax.dynamic_slice_in_dim(gates, q0, Q_BLOCK, axis=1)
        kwb = lax.dynamic_slice_in_dim(kw, q0, WINDOW + Q_BLOCK, axis=1)
        vwb = lax.dynamic_slice_in_dim(vw, q0, WINDOW + Q_BLOCK, axis=1)
        q_pos = q0 + jnp.arange(Q_BLOCK, dtype=jnp.int32)
        kw_pos = q0 - WINDOW + jnp.arange(WINDOW + Q_BLOCK, dtype=jnp.int32)
        return gate_sum(gb, *nsa_attend(qb, q_pos, kc, vc, ks, vs, kwb, vwb, kw_pos, tbl))

    o = lax.map(block, jnp.arange(T // Q_BLOCK, dtype=jnp.int32))
    return jnp.moveaxis(o, 0, 1).reshape(B, T, NSA_W)


def nsa_sample(q, kv, gates, cmp_pages, slc_pages, win_buf, page_table, pe_k, wk, pe_v, wv, tbl):
    B, S = q.shape[:2]
    past = page_table.shape[1] * cmp_pages.shape[1]
    cmp_full = jnp.concatenate([cmp_pages[page_table].reshape(B, past, 2, KV_HEADS, HEAD_DIM), kv[:, :, 0]], axis=1)
    slc_full = jnp.concatenate([slc_pages[page_table].reshape(B, past, 2, KV_HEADS, HEAD_DIM), kv[:, :, 1]], axis=1)
    win = jnp.concatenate([win_buf, kv[:, :, 2]], axis=1)
    wb = win_buf.shape[1]
    kc = compress(cmp_full[:, :, 0], pe_k, wk)
    vc = compress(cmp_full[:, :, 1], pe_v, wv)
    ks = to_blocks(slc_full[:, :, 0])
    vs = to_blocks(slc_full[:, :, 1])
    q_pos = past + jnp.arange(S, dtype=jnp.int32)
    kw_pos = past - wb + jnp.arange(wb + S, dtype=jnp.int32)
    o = gate_sum(gates, *nsa_attend(q, q_pos, kc, vc, ks, vs, win[:, :, 0], win[:, :, 1], kw_pos, tbl))
    return o.reshape(B, S, NSA_W), win[:, S:]


def residual_update(x, conv_o, pool_o, nsa_o, w_out, g2, w_gu, w_down):
    x = x + jnp.concatenate([conv_o, pool_o, nsa_o], axis=-1) @ w_out
    gu = rmsnorm(x, g2) @ w_gu
    return x + (jax.nn.silu(gu[..., :D_FF]) * gu[..., D_FF:]) @ w_down


def setup_inputs(seed: int = 0) -> dict:
    key = jax.random.key(seed)
    k = jax.random.split(key, 26)
    f32 = jnp.float32

    def nrm(kk, shape, s):
        return jax.random.normal(kk, shape, f32) * s

    n_pages = PAST_LEN // PAGE_SIZE
    n_used = DEC_BATCH * n_pages
    n_phys = n_used + max(1, n_used // 4)
    win_len = min(WINDOW, PAST_LEN)
    page_table = jax.random.permutation(k[7], n_phys)[:n_used].reshape(DEC_BATCH, n_pages).astype(jnp.int32)
    return {
        'x_prompt': nrm(k[0], (BATCH, SEQ, D_MODEL), 1.0),
        'x_sample': nrm(k[1], (DEC_BATCH, DEC_SEQ, D_MODEL), 1.0),
        'cache_cmp_kv': nrm(k[2], (DEPTH, n_phys, PAGE_SIZE, 2, KV_HEADS, HEAD_DIM), 1.0),
        'cache_slc_kv': nrm(k[3], (DEPTH, n_phys, PAGE_SIZE, 2, KV_HEADS, HEAD_DIM), 1.0),
        'cache_win_kv': nrm(k[4], (DEPTH, DEC_BATCH, win_len, 2, KV_HEADS, HEAD_DIM), 1.0),
        'state_conv': nrm(k[5], (DEPTH, DEC_BATCH, CONV_BUF, CONV_W), 0.5),
        'state_pool': nrm(k[6], (DEPTH, DEC_BATCH, POOL_BUF, POOL_W), 1.0),
        'page_table': page_table,
        'rel_bias': nrm(k[8], (NUM_BUCKETS, N_HEADS), 0.5),
        'norm1': 1.0 + nrm(k[9], (DEPTH, D_MODEL), 0.02),
        'w_in': nrm(k[10], (DEPTH, D_MODEL, IN_W), D_MODEL ** -0.5),
        'conv_dw': nrm(k[11], (DEPTH, CONV_WIDTH, CONV_W), CONV_WIDTH ** -0.5),
        'conv_b': nrm(k[12], (DEPTH, CONV_W), 0.02),
        'conv_ln_g': 1.0 + nrm(k[13], (DEPTH, CONV_W), 0.02),
        'conv_ln_b': nrm(k[14], (DEPTH, CONV_W), 0.02),
        'pool_w': nrm(k[15], (DEPTH, len(POOL_WINDOWS), POOL_GROUP_W, POOL_GROUP_W), POOL_GROUP_W ** -0.5),
        'pool_scale': 1.0 + nrm(k[16], (DEPTH, POOL_W), 0.1),
        'cmp_pe_k': nrm(k[17], (DEPTH, CMP_BLOCK, HEAD_DIM), 0.02),
        'cmp_wk': nrm(k[18], (DEPTH, CMP_BLOCK, HEAD_DIM, HEAD_DIM), (CMP_BLOCK * HEAD_DIM) ** -0.5),
        'cmp_pe_v': nrm(k[19], (DEPTH, CMP_BLOCK, HEAD_DIM), 0.02),
        'cmp_wv': nrm(k[20], (DEPTH, CMP_BLOCK, HEAD_DIM, HEAD_DIM), (CMP_BLOCK * HEAD_DIM) ** -0.5),
        'w_out': nrm(k[21], (DEPTH, CONV_W + POOL_W + NSA_W, D_MODEL), D_MODEL ** -0.5),
        'norm2': 1.0 + nrm(k[22], (DEPTH, D_MODEL), 0.02),
        'w_gu': nrm(k[23], (DEPTH, D_MODEL, 2 * D_FF), D_MODEL ** -0.5),
        'w_down': nrm(k[24], (DEPTH, D_FF, D_MODEL), D_FF ** -0.5),
        'final_norm': 1.0 + nrm(k[25], (D_MODEL,), 0.02),
    }


def reference(x_prompt, x_sample, cache_cmp_kv, cache_slc_kv, cache_win_kv, state_conv, state_pool,
              page_table, rel_bias, norm1, w_in, conv_dw, conv_b, conv_ln_g, conv_ln_b, pool_w,
              pool_scale, cmp_pe_k, cmp_wk, cmp_pe_v, cmp_wv, w_out, norm2, w_gu, w_down, final_norm):
    tbl = rel_bias.reshape(NUM_BUCKETS, KV_HEADS, GQA_GROUP).transpose(1, 2, 0)
    T = x_prompt.shape[1]
    past = page_table.shape[1] * cache_cmp_kv.shape[2]
    xp, xs = x_prompt, x_sample
    p_cmp, p_slc, p_win, p_conv, p_pool = [], [], [], [], []
    s_cmp, s_slc, s_win, s_conv, s_pool = [], [], [], [], []
    for l in range(DEPTH):
        glu, u, q, kv, gates = project(rmsnorm(xp, norm1[l]), w_in[l])
        glu_ext = jnp.pad(glu, ((0, 0), (CONV_BUF, 0), (0, 0)))
        u_ext = jnp.pad(u, ((0, 0), (POOL_BUF, 0), (0, 0)))
        conv_o = conv_mix(glu_ext, conv_dw[l], conv_b[l], conv_ln_g[l], conv_ln_b[l])
        pool_o = pool_mix(u_ext, 0, pool_w[l], pool_scale[l])
        nsa_o = nsa_prompt(q, kv, gates, cmp_pe_k[l], cmp_wk[l], cmp_pe_v[l], cmp_wv[l], tbl)
        xp = residual_update(xp, conv_o, pool_o, nsa_o, w_out[l], norm2[l], w_gu[l], w_down[l])
        p_cmp.append(kv[:, :, 0])
        p_slc.append(kv[:, :, 1])
        p_win.append(kv[:, T - min(WINDOW, T):, 2])
        p_conv.append(glu_ext[:, -CONV_BUF:])
        p_pool.append(u_ext[:, -POOL_BUF:])
        glu, u, q, kv, gates = project(rmsnorm(xs, norm1[l]), w_in[l])
        glu_ext = jnp.concatenate([state_conv[l], glu], axis=1)
        u_ext = jnp.concatenate([state_pool[l], u], axis=1)
        conv_o = conv_mix(glu_ext, conv_dw[l], conv_b[l], conv_ln_g[l], conv_ln_b[l])
        pool_o = pool_mix(u_ext, past, pool_w[l], pool_scale[l])
        nsa_o, new_win = nsa_sample(q, kv, gates, cache_cmp_kv[l], cache_slc_kv[l], cache_win_kv[l],
                                    page_table, cmp_pe_k[l], cmp_wk[l], cmp_pe_v[l], cmp_wv[l], tbl)
        xs = residual_update(xs, conv_o, pool_o, nsa_o, w_out[l], norm2[l], w_gu[l], w_down[l])
        s_cmp.append(kv[:, :, 0])
        s_slc.append(kv[:, :, 1])
        s_win.append(new_win)
        s_conv.append(glu_ext[:, -CONV_BUF:])
        s_pool.append(u_ext[:, -POOL_BUF:])
    y_prompt = rmsnorm(xp, final_norm)
    y_sample = rmsnorm(xs, final_norm)
    return (y_prompt, y_sample,
            jnp.stack(p_cmp), jnp.stack(p_slc), jnp.stack(p_win), jnp.stack(p_conv), jnp.stack(p_pool),
            jnp.stack(s_cmp), jnp.stack(s_slc), jnp.stack(s_win), jnp.stack(s_conv), jnp.stack(s_pool))
```

```python
import functools
import math

import jax
import jax.numpy as jnp
from jax import lax
from jax.experimental import pallas as pl
from jax.experimental.pallas import tpu as pltpu

D_MODEL = 1024
CONV_W = 256
POOL_W = 256
N_HEADS = 8
HEAD_DIM = 64
NSA_W = N_HEADS * HEAD_DIM
KV_HEADS = 2
GQA_GROUP = N_HEADS // KV_HEADS
KV_W = KV_HEADS * HEAD_DIM
CONV_WIDTH = 31
CONV_BUF = CONV_WIDTH - 1
POOL_WINDOWS = (2, 4, 8, 16)
POOL_GROUP_W = POOL_W // len(POOL_WINDOWS)
POOL_BUF = max(POOL_WINDOWS) - 1
CMP_STRIDE = 16
CMP_BLOCK = 2 * CMP_STRIDE
SLC_BLOCK = 64
SLC_TOPK = 16
WINDOW = 512
Q_BLOCK = 128
NUM_BUCKETS = 32
MAX_DISTANCE = 128
D_FF = 2816
EPS = 1e-6
NEG = -1e30
FORCE = 1e4

OFF_POOL = 2 * CONV_W
OFF_Q = OFF_POOL + POOL_W
OFF_KV = OFF_Q + NSA_W
OFF_GATE = OFF_KV + 6 * KV_W
IN_W = OFF_GATE + 3 * N_HEADS
LANE = 128
IN_W_PAD = -(-IN_W // LANE) * LANE
GATE_PAD = IN_W_PAD - OFF_GATE

VMEM_LIMIT = 56 * 1024 * 1024

F32 = jnp.float32
BF16 = jnp.bfloat16


def _rms(x, g):
    return x * lax.rsqrt(jnp.mean(x * x, axis=-1, keepdims=True) + EPS) * g


def _inproj_kernel(x_ref, g_ref, w_ref, glu_ref, u_ref, q_ref, kvc_ref, kvs_ref, kvw_ref, gate_ref):
    h = _rms(x_ref[...], g_ref[...]).astype(BF16)
    z = jnp.dot(h, w_ref[...], preferred_element_type=F32)
    glu_ref[...] = z[:, :CONV_W] * jax.nn.sigmoid(z[:, CONV_W:OFF_POOL])
    u_ref[...] = z[:, OFF_POOL:OFF_Q]
    q_ref[...] = z[:, OFF_Q:OFF_KV]
    kvc_ref[...] = z[:, OFF_KV:OFF_KV + 2 * KV_W]
    kvs_ref[...] = z[:, OFF_KV + 2 * KV_W:OFF_KV + 4 * KV_W]
    kvw_ref[...] = z[:, OFF_KV + 4 * KV_W:OFF_GATE]
    gate_ref[...] = jax.nn.sigmoid(z[:, OFF_GATE:])


def _inproj(x2d, g, w_pad, tm):
    n = x2d.shape[0]
    row = lambda w: pl.BlockSpec((tm, w), lambda i: (i, 0))
    full = lambda a: pl.BlockSpec(a.shape, lambda i: (0, 0))
    widths = (CONV_W, POOL_W, NSA_W, 2 * KV_W, 2 * KV_W, 2 * KV_W, GATE_PAD)
    return pl.pallas_call(
        _inproj_kernel,
        grid=(n // tm,),
        in_specs=[row(D_MODEL), full(g), full(w_pad)],
        out_specs=[row(w) for w in widths],
        out_shape=[jax.ShapeDtypeStruct((n, w), F32) for w in widths],
        compiler_params=pltpu.CompilerParams(dimension_semantics=("parallel",), vmem_limit_bytes=VMEM_LIMIT),
        name="inproj",
    )(x2d, g, w_pad)


def _ffn_kernel(x_ref, conv_ref, pool_ref, nsa_ref, wo_ref, g2_ref, wg_ref, wu_ref, wd_ref, gf_ref,
                y_ref, x1_ref, h2_ref, acc_ref, *, final_norm):
    j = pl.program_id(1)

    @pl.when(j == 0)
    def _():
        mix = jnp.concatenate([conv_ref[...], pool_ref[...], nsa_ref[...]], axis=-1).astype(BF16)
        x1 = x_ref[...] + jnp.dot(mix, wo_ref[...], preferred_element_type=F32)
        x1_ref[...] = x1
        h2_ref[...] = _rms(x1, g2_ref[...]).astype(BF16)
        acc_ref[...] = jnp.zeros_like(acc_ref)

    h2 = h2_ref[...]
    gate = jnp.dot(h2, wg_ref[...], preferred_element_type=F32)
    up = jnp.dot(h2, wu_ref[...], preferred_element_type=F32)
    act = (jax.nn.silu(gate) * up).astype(BF16)
    acc_ref[...] += jnp.dot(act, wd_ref[...], preferred_element_type=F32)

    @pl.when(j == pl.num_programs(1) - 1)
    def _():
        y = x1_ref[...] + acc_ref[...]
        if final_norm:
            y = _rms(y, gf_ref[...])
        y_ref[...] = y


def _ffn(x2d, conv_o, pool_o, nsa_o, w_out, g2, w_gu, w_down, gf, tm, tf, final_norm):
    n = x2d.shape[0]
    nf = D_FF // tf
    row = lambda w: pl.BlockSpec((tm, w), lambda i, j: (i, 0))
    full = lambda a: pl.BlockSpec(a.shape, lambda i, j: (0, 0))
    return pl.pallas_call(
        functools.partial(_ffn_kernel, final_norm=final_norm),
        grid=(n // tm, nf),
        in_specs=[row(D_MODEL), row(CONV_W), row(POOL_W), row(NSA_W), full(w_out), full(g2),
                  pl.BlockSpec((D_MODEL, tf), lambda i, j: (0, j)),
                  pl.BlockSpec((D_MODEL, tf), lambda i, j: (0, j + nf)),
                  pl.BlockSpec((tf, D_MODEL), lambda i, j: (j, 0)),
                  full(gf)],
        out_specs=row(D_MODEL),
        out_shape=jax.ShapeDtypeStruct((n, D_MODEL), F32),
        scratch_shapes=[pltpu.VMEM((tm, D_MODEL), F32), pltpu.VMEM((tm, D_MODEL), BF16),
                        pltpu.VMEM((tm, D_MODEL), F32)],
        compiler_params=pltpu.CompilerParams(dimension_semantics=("parallel", "arbitrary"),
                                             vmem_limit_bytes=VMEM_LIMIT),
        name="ffn",
    )(x2d, conv_o, pool_o, nsa_o, w_out, g2, w_gu, w_gu, w_down, gf)


def _layernorm(x, g, b):
    mu = jnp.mean(x, axis=-1, keepdims=True)
    xc = x - mu
    var = jnp.mean(xc * xc, axis=-1, keepdims=True)
    return xc * lax.rsqrt(var + EPS) * g + b


def _t5_bucket(dist):
    n = jnp.maximum(dist, 0)
    max_exact = NUM_BUCKETS // 2
    nf = jnp.maximum(n, 1).astype(F32)
    large = max_exact + (jnp.log(nf / max_exact) / math.log(MAX_DISTANCE / max_exact)
                         * (NUM_BUCKETS - max_exact)).astype(jnp.int32)
    large = jnp.minimum(large, NUM_BUCKETS - 1)
    return jnp.where(n < max_exact, n, large)


def _masked_softmax(s, mask):
    s = jnp.where(mask, s.astype(F32), NEG)
    p = jax.nn.softmax(s, axis=-1)
    return p * jnp.any(mask, axis=-1, keepdims=True)


def _conv_mix(glu_ext, dw, b, ln_g, ln_b):
    y = lax.conv_general_dilated(glu_ext, dw[:, None, :], window_strides=(1,), padding='VALID',
                                 dimension_numbers=('NWC', 'WIO', 'NWC'), feature_group_count=CONV_W)
    return jax.nn.silu(_layernorm(y + b, ln_g, ln_b))


def _pool_mix(u_ext, pos0, w_pool, scale):
    B, L, _ = u_ext.shape
    T = L - POOL_BUF
    cs = jnp.pad(jnp.cumsum(u_ext, axis=1), ((0, 0), (1, 0), (0, 0)))
    end = cs[:, POOL_BUF + 1:POOL_BUF + 1 + T]
    u = u_ext[:, POOL_BUF:]
    t = pos0 + jnp.arange(T, dtype=jnp.int32)
    diffs = []
    for g, w in enumerate(POOL_WINDOWS):
        sl = slice(g * POOL_GROUP_W, (g + 1) * POOL_GROUP_W)
        start = cs[:, POOL_BUF + 1 - w:POOL_BUF + 1 - w + T, sl]
        cnt = jnp.minimum(w, t + 1).astype(F32)[:, None]
        diffs.append((end[..., sl] - start) / cnt - u[..., sl])
    d = jnp.stack(diffs, axis=2)
    y = jnp.einsum('btgc,gce->btge', d, w_pool).reshape(B, T, POOL_W)
    return y * scale


def _compress(k, pe, w):
    B, L = k.shape[:2]
    Lp = -(-L // CMP_STRIDE) * CMP_STRIDE
    k = jnp.pad(k, ((0, 0), (0, Lp - L), (0, 0), (0, 0)))
    sub = k.reshape(B, Lp // CMP_STRIDE, CMP_STRIDE, KV_HEADS, HEAD_DIM)
    blocks = jnp.concatenate([sub[:, :-1], sub[:, 1:]], axis=2)
    return jnp.einsum('bnlgd,lde->bnge', blocks + pe[:, None, :], w)


def _to_blocks(k):
    B, L = k.shape[:2]
    Lp = -(-L // SLC_BLOCK) * SLC_BLOCK
    k = jnp.pad(k, ((0, 0), (0, Lp - L), (0, 0), (0, 0)))
    return k.reshape(B, Lp // SLC_BLOCK, SLC_BLOCK, KV_HEADS, HEAD_DIM).transpose(0, 3, 1, 2, 4)


def _nsa_attend(q, q_pos, kc, vc, ks, vs, kw, vw, kw_pos, tbl):
    B, Q = q.shape[:2]
    scale = HEAD_DIM ** -0.5
    n_cmp, n_slc = kc.shape[1], ks.shape[2]
    c_end = jnp.arange(n_cmp, dtype=jnp.int32) * CMP_STRIDE + (CMP_BLOCK - 1)
    dist = q_pos[:, None] - c_end[None, :]
    s = jnp.einsum('bqgrd,bngd->bgrqn', q, kc).astype(F32) * scale + tbl[:, :, _t5_bucket(dist)]
    p_cmp = _masked_softmax(s, dist >= 0)
    o_cmp = jnp.einsum('bgrqn,bngd->bqgrd', p_cmp.astype(vc.dtype), vc)
    c_start = jnp.arange(n_cmp, dtype=jnp.int32) * CMP_STRIDE
    s_start = jnp.arange(n_slc, dtype=jnp.int32) * SLC_BLOCK
    overlap = ((c_start[:, None] < s_start[None, :] + SLC_BLOCK)
               & (c_start[:, None] + CMP_BLOCK > s_start[None, :])).astype(F32)
    p_slc = jnp.einsum('bgrqn,nj->bgqj', p_cmp, overlap)
    blk = jnp.arange(n_slc, dtype=jnp.int32)[None, :]
    cur = (q_pos // SLC_BLOCK)[:, None]
    forced = (blk == 0) | (blk == cur) | (blk == cur - 1)
    score = jnp.where(forced, FORCE, jnp.where(blk <= cur, p_slc, NEG))
    _, idx = lax.top_k(score, min(SLC_TOPK, n_slc))
    n_sel = idx.shape[-1]
    M = n_sel * SLC_BLOCK
    bi = jnp.arange(B)[:, None, None, None]
    gi = jnp.arange(KV_HEADS)[None, :, None, None]
    kg = ks[bi, gi, idx].reshape(B, KV_HEADS, Q, M, HEAD_DIM)
    vg = vs[bi, gi, idx].reshape(B, KV_HEADS, Q, M, HEAD_DIM)
    pos = (idx[..., None] * SLC_BLOCK + jnp.arange(SLC_BLOCK, dtype=jnp.int32)).reshape(B, KV_HEADS, Q, M)
    dist = q_pos[None, None, :, None] - pos
    bias = tbl[gi[..., None], jnp.arange(GQA_GROUP)[None, None, :, None, None], _t5_bucket(dist)[:, :, None]]
    s = jnp.einsum('bqgrd,bgqmd->bgrqm', q, kg).astype(F32) * scale + bias
    p = _masked_softmax(s, (dist >= 0)[:, :, None])
    o_slc = jnp.einsum('bgrqm,bgqmd->bqgrd', p.astype(vg.dtype), vg)
    dist = q_pos[:, None] - kw_pos[None, :]
    mask = (dist >= 0) & (dist < WINDOW) & (kw_pos[None, :] >= 0)
    s = jnp.einsum('bqgrd,bkgd->bgrqk', q, kw).astype(F32) * scale + tbl[:, :, _t5_bucket(dist)]
    p = _masked_softmax(s, mask)
    o_win = jnp.einsum('bgrqk,bkgd->bqgrd', p.astype(vw.dtype), vw)
    return o_cmp, o_slc, o_win


def _gate_sum(gates, o_cmp, o_slc, o_win):
    return gates[:, :, 0] * o_cmp + gates[:, :, 1] * o_slc + gates[:, :, 2] * o_win


def _nsa_prompt(q, kv, gates, pe_k, wk, pe_v, wv, tbl):
    B, T = q.shape[:2]
    kc = _compress(kv[:, :, 0, 0], pe_k, wk)
    vc = _compress(kv[:, :, 0, 1], pe_v, wv)
    ks = _to_blocks(kv[:, :, 1, 0])
    vs = _to_blocks(kv[:, :, 1, 1])
    pad = ((0, 0), (WINDOW, 0), (0, 0), (0, 0))
    kw = jnp.pad(kv[:, :, 2, 0], pad)
    vw = jnp.pad(kv[:, :, 2, 1], pad)

    def block(i):
        q0 = i * Q_BLOCK
        qb = lax.dynamic_slice_in_dim(q, q0, Q_BLOCK, axis=1)
        gb = lax.dynamic_slice_in_dim(gates, q0, Q_BLOCK, axis=1)
        kwb = lax.dynamic_slice_in_dim(kw, q0, WINDOW + Q_BLOCK, axis=1)
        vwb = lax.dynamic_slice_in_dim(vw, q0, WINDOW + Q_BLOCK, axis=1)
        q_pos = q0 + jnp.arange(Q_BLOCK, dtype=jnp.int32)
        kw_pos = q0 - WINDOW + jnp.arange(WINDOW + Q_BLOCK, dtype=jnp.int32)
        return _gate_sum(gb, *_nsa_attend(qb, q_pos, kc, vc, ks, vs, kwb, vwb, kw_pos, tbl))

    o = lax.map(block, jnp.arange(T // Q_BLOCK, dtype=jnp.int32))
    return jnp.moveaxis(o, 0, 1).reshape(B, T, NSA_W)


def _nsa_sample(q, kv, gates, cmp_pages, slc_pages, win_buf, page_table, pe_k, wk, pe_v, wv, tbl):
    B, S = q.shape[:2]
    past = page_table.shape[1] * cmp_pages.shape[1]
    cmp_full = jnp.concatenate([cmp_pages[page_table].reshape(B, past, 2, KV_HEADS, HEAD_DIM), kv[:, :, 0]], axis=1)
    slc_full = jnp.concatenate([slc_pages[page_table].reshape(B, past, 2, KV_HEADS, HEAD_DIM), kv[:, :, 1]], axis=1)
    win = jnp.concatenate([win_buf, kv[:, :, 2]], axis=1)
    wb = win_buf.shape[1]
    kc = _compress(cmp_full[:, :, 0], pe_k, wk)
    vc = _compress(cmp_full[:, :, 1], pe_v, wv)
    ks = _to_blocks(slc_full[:, :, 0])
    vs = _to_blocks(slc_full[:, :, 1])
    q_pos = past + jnp.arange(S, dtype=jnp.int32)
    kw_pos = past - wb + jnp.arange(wb + S, dtype=jnp.int32)
    o = _gate_sum(gates, *_nsa_attend(q, q_pos, kc, vc, ks, vs, win[:, :, 0], win[:, :, 1], kw_pos, tbl))
    return o.reshape(B, S, NSA_W), win[:, S:]


def _group_layer(x, l, pos0, hist_conv, hist_pool, nsa_fn, P, final):
    B, T, _ = x.shape
    n = B * T
    tm = min(n, 1024)
    glu, u, q, kvc, kvs, kvw, gates = _inproj(x.reshape(n, D_MODEL), P['norm1'][l], P['w_in'][l], min(n, 512))
    glu = glu.reshape(B, T, CONV_W)
    u = u.reshape(B, T, POOL_W)
    glu_ext = jnp.concatenate([hist_conv, glu], axis=1)
    u_ext = jnp.concatenate([hist_pool, u], axis=1)
    conv_o = _conv_mix(glu_ext, P['conv_dw'][l], P['conv_b'][l], P['conv_ln_g'][l], P['conv_ln_b'][l])
    pool_o = _pool_mix(u_ext, pos0, P['pool_w'][l], P['pool_scale'][l])
    q5 = q.reshape(B, T, KV_HEADS, GQA_GROUP, HEAD_DIM)
    kv6 = jnp.stack([kvc, kvs, kvw], axis=1).reshape(B, T, 3, 2, KV_HEADS, HEAD_DIM)
    g6 = gates[:, :3 * N_HEADS].reshape(B, T, 3, KV_HEADS, GQA_GROUP, 1)
    nsa_o, extra = nsa_fn(q5, kv6, g6)
    y = _ffn(x.reshape(n, D_MODEL), conv_o.reshape(n, CONV_W), pool_o.reshape(n, POOL_W),
             nsa_o.reshape(n, NSA_W), P['w_out'][l], P['norm2'][l], P['w_gu'][l], P['w_down'][l],
             P['final_norm'], tm, 256, final)
    shp = (B, T, 2, KV_HEADS, HEAD_DIM)
    return (y.reshape(B, T, D_MODEL), kvc.reshape(shp), kvs.reshape(shp), kvw.reshape(shp),
            glu_ext[:, -CONV_BUF:], u_ext[:, -POOL_BUF:], extra)


def kernel(x_prompt, x_sample, cache_cmp_kv, cache_slc_kv, cache_win_kv, state_conv, state_pool, page_table, rel_bias, norm1, w_in, conv_dw, conv_b, conv_ln_g, conv_ln_b, pool_w, pool_scale, cmp_pe_k, cmp_wk, cmp_pe_v, cmp_wv, w_out, norm2, w_gu, w_down, final_norm):
    depth = w_in.shape[0]
    tbl = rel_bias.reshape(NUM_BUCKETS, KV_HEADS, GQA_GROUP).transpose(1, 2, 0)
    T = x_prompt.shape[1]
    past = page_table.shape[1] * cache_cmp_kv.shape[2]
    P = dict(
        norm1=norm1[:, None, :], norm2=norm2[:, None, :], final_norm=final_norm[None, :],
        w_in=jnp.pad(w_in, ((0, 0), (0, 0), (0, IN_W_PAD - IN_W))).astype(BF16),
        w_out=w_out.astype(BF16), w_gu=w_gu.astype(BF16), w_down=w_down.astype(BF16),
        conv_dw=conv_dw, conv_b=conv_b, conv_ln_g=conv_ln_g, conv_ln_b=conv_ln_b,
        pool_w=pool_w, pool_scale=pool_scale)
    xp, xs = x_prompt, x_sample
    Bp, Bs = xp.shape[0], xs.shape[0]
    outs_p, outs_s = [], []
    for l in range(depth):
        final = l == depth - 1
        nsa_p = lambda q, kv, g: (_nsa_prompt(q, kv, g, cmp_pe_k[l], cmp_wk[l], cmp_pe_v[l], cmp_wv[l], tbl), None)
        r = _group_layer(xp, l, 0, jnp.zeros((Bp, CONV_BUF, CONV_W), F32), jnp.zeros((Bp, POOL_BUF, POOL_W), F32),
                         nsa_p, P, final)
        xp = r[0]
        outs_p.append((r[1], r[2], r[3][:, T - min(WINDOW, T):], r[4], r[5]))
        nsa_s = lambda q, kv, g: _nsa_sample(q, kv, g, cache_cmp_kv[l], cache_slc_kv[l], cache_win_kv[l],
                                             page_table, cmp_pe_k[l], cmp_wk[l], cmp_pe_v[l], cmp_wv[l], tbl)
        r = _group_layer(xs, l, past, state_conv[l], state_pool[l], nsa_s, P, final)
        xs = r[0]
        outs_s.append((r[1], r[2], r[6], r[4], r[5]))
    st = lambda outs, i: jnp.stack([o[i] for o in outs])
    return (xp, xs,
            st(outs_p, 0), st(outs_p, 1), st(outs_p, 2), st(outs_p, 3), st(outs_p, 4),
            st(outs_s, 0), st(outs_s, 1), st(outs_s, 2), st(outs_s, 3), st(outs_s, 4))
```

```python
import functools
import math

import jax
import jax.numpy as jnp
from jax import lax
from jax.experimental import pallas as pl
from jax.experimental.pallas import tpu as pltpu

D_MODEL = 1024
CONV_W = 256
POOL_W = 256
N_HEADS = 8
HEAD_DIM = 64
NSA_W = N_HEADS * HEAD_DIM
KV_HEADS = 2
GQA_GROUP = N_HEADS // KV_HEADS
KV_W = KV_HEADS * HEAD_DIM
CONV_WIDTH = 31
CONV_BUF = CONV_WIDTH - 1
POOL_WINDOWS = (2, 4, 8, 16)
POOL_GROUP_W = POOL_W // len(POOL_WINDOWS)
POOL_BUF = max(POOL_WINDOWS) - 1
CMP_STRIDE = 16
CMP_BLOCK = 2 * CMP_STRIDE
SLC_BLOCK = 64
SLC_TOPK = 16
WINDOW = 512
Q_BLOCK = 128
NUM_BUCKETS = 32
MAX_DISTANCE = 128
D_FF = 2816
EPS = 1e-6
NEG = -1e30
FORCE = 1e4

OFF_POOL = 2 * CONV_W
OFF_Q = OFF_POOL + POOL_W
OFF_KV = OFF_Q + NSA_W
OFF_GATE = OFF_KV + 6 * KV_W
IN_W = OFF_GATE + 3 * N_HEADS
LANE = 128
IN_W_PAD = -(-IN_W // LANE) * LANE
GATE_PAD = IN_W_PAD - OFF_GATE

VMEM_LIMIT = 56 * 1024 * 1024

F32 = jnp.float32
BF16 = jnp.bfloat16


def _rms(x, g):
    return x * lax.rsqrt(jnp.mean(x * x, axis=-1, keepdims=True) + EPS) * g


def _inproj_kernel(x_ref, g_ref, w_ref, glu_ref, u_ref, q_ref, kvc_ref, kvs_ref, kvw_ref, gate_ref):
    h = _rms(x_ref[...], g_ref[...]).astype(BF16)
    z = jnp.dot(h, w_ref[...], preferred_element_type=F32)
    glu_ref[...] = z[:, :CONV_W] * jax.nn.sigmoid(z[:, CONV_W:OFF_POOL])
    u_ref[...] = z[:, OFF_POOL:OFF_Q]
    q_ref[...] = z[:, OFF_Q:OFF_KV]
    kvc_ref[...] = z[:, OFF_KV:OFF_KV + 2 * KV_W]
    kvs_ref[...] = z[:, OFF_KV + 2 * KV_W:OFF_KV + 4 * KV_W]
    kvw_ref[...] = z[:, OFF_KV + 4 * KV_W:OFF_GATE]
    gate_ref[...] = jax.nn.sigmoid(z[:, OFF_GATE:])


def _inproj(x2d, g, w_pad, tm):
    n = x2d.shape[0]
    row = lambda w: pl.BlockSpec((tm, w), lambda i: (i, 0))
    full = lambda a: pl.BlockSpec(a.shape, lambda i: (0, 0))
    widths = (CONV_W, POOL_W, NSA_W, 2 * KV_W, 2 * KV_W, 2 * KV_W, GATE_PAD)
    return pl.pallas_call(
        _inproj_kernel,
        grid=(n // tm,),
        in_specs=[row(D_MODEL), full(g), full(w_pad)],
        out_specs=[row(w) for w in widths],
        out_shape=[jax.ShapeDtypeStruct((n, w), F32) for w in widths],
        compiler_params=pltpu.CompilerParams(dimension_semantics=("parallel",), vmem_limit_bytes=VMEM_LIMIT),
        name="inproj",
    )(x2d, g, w_pad)


def _ffn_kernel(x_ref, conv_ref, pool_ref, nsa_ref, wo_ref, g2_ref, wg_ref, wu_ref, wd_ref, gf_ref,
                y_ref, x1_ref, h2_ref, acc_ref, *, final_norm):
    j = pl.program_id(1)

    @pl.when(j == 0)
    def _():
        mix = jnp.concatenate([conv_ref[...], pool_ref[...], nsa_ref[...]], axis=-1).astype(BF16)
        x1 = x_ref[...] + jnp.dot(mix, wo_ref[...], preferred_element_type=F32)
        x1_ref[...] = x1
        h2_ref[...] = _rms(x1, g2_ref[...]).astype(BF16)
        acc_ref[...] = jnp.zeros_like(acc_ref)

    h2 = h2_ref[...]
    gate = jnp.dot(h2, wg_ref[...], preferred_element_type=F32)
    up = jnp.dot(h2, wu_ref[...], preferred_element_type=F32)
    act = (jax.nn.silu(gate) * up).astype(BF16)
    acc_ref[...] += jnp.dot(act, wd_ref[...], preferred_element_type=F32)

    @pl.when(j == pl.num_programs(1) - 1)
    def _():
        y = x1_ref[...] + acc_ref[...]
        if final_norm:
            y = _rms(y, gf_ref[...])
        y_ref[...] = y


def _ffn(x2d, conv_o, pool_o, nsa_o, w_out, g2, w_gu, w_down, gf, tm, tf, final_norm):
    n = x2d.shape[0]
    nf = D_FF // tf
    row = lambda w: pl.BlockSpec((tm, w), lambda i, j: (i, 0))
    full = lambda a: pl.BlockSpec(a.shape, lambda i, j: (0, 0))
    return pl.pallas_call(
        functools.partial(_ffn_kernel, final_norm=final_norm),
        grid=(n // tm, nf),
        in_specs=[row(D_MODEL), row(CONV_W), row(POOL_W), row(NSA_W), full(w_out), full(g2),
                  pl.BlockSpec((D_MODEL, tf), lambda i, j: (0, j)),
                  pl.BlockSpec((D_MODEL, tf), lambda i, j: (0, j + nf)),
                  pl.BlockSpec((tf, D_MODEL), lambda i, j: (j, 0)),
                  full(gf)],
        out_specs=row(D_MODEL),
        out_shape=jax.ShapeDtypeStruct((n, D_MODEL), F32),
        scratch_shapes=[pltpu.VMEM((tm, D_MODEL), F32), pltpu.VMEM((tm, D_MODEL), BF16),
                        pltpu.VMEM((tm, D_MODEL), F32)],
        compiler_params=pltpu.CompilerParams(dimension_semantics=("parallel", "arbitrary"),
                                             vmem_limit_bytes=VMEM_LIMIT),
        name="ffn",
    )(x2d, conv_o, pool_o, nsa_o, w_out, g2, w_gu, w_gu, w_down, gf)


def _bias_kernel(tbl_ref, dist_ref, out_ref, *, hi):
    g = pl.program_id(0)
    d = dist_ref[0]
    rows = d.shape[0]
    n = jnp.maximum(d, 0)
    max_exact = NUM_BUCKETS // 2
    nf = jnp.maximum(n, 1).astype(F32)
    large = max_exact + (jnp.log(nf / max_exact) / math.log(MAX_DISTANCE / max_exact)
                         * (NUM_BUCKETS - max_exact)).astype(jnp.int32)
    large = jnp.minimum(large, NUM_BUCKETS - 1)
    bucket = jnp.where(n < max_exact, n, large)
    valid = (d >= 0) & (d < hi)
    for r in range(GQA_GROUP):
        acc = jnp.zeros(d.shape, F32)
        for b in range(NUM_BUCKETS):
            acc = jnp.where(bucket == b, tbl_ref[(g * GQA_GROUP + r) * NUM_BUCKETS + b], acc)
        out_ref[0, 0, r * rows:(r + 1) * rows, :] = jnp.where(valid, acc, NEG)


def _bias_tables(tbl_flat, dist, hi):
    nblk, R, C = dist.shape
    return pl.pallas_call(
        functools.partial(_bias_kernel, hi=hi),
        grid_spec=pltpu.PrefetchScalarGridSpec(
            num_scalar_prefetch=1,
            grid=(KV_HEADS, nblk),
            in_specs=[pl.BlockSpec((1, R, C), lambda g, i, tbl: (i, 0, 0))],
            out_specs=pl.BlockSpec((1, 1, GQA_GROUP * R, C), lambda g, i, tbl: (g, i, 0, 0)),
        ),
        out_shape=jax.ShapeDtypeStruct((KV_HEADS, nblk, GQA_GROUP * R, C), F32),
        compiler_params=pltpu.CompilerParams(dimension_semantics=("parallel", "parallel")),
        name="bias_tables",
    )(tbl_flat, dist)


def _compress_kernel(x_ref, pea_ref, peb_ref, wa_ref, wb_ref, o_ref):
    x = x_ref[0]
    a = jnp.dot((x + pea_ref[...]).astype(BF16), wa_ref[...], preferred_element_type=F32)
    b = jnp.dot((x + peb_ref[...]).astype(BF16), wb_ref[...], preferred_element_type=F32)
    o_ref[0] = a + pltpu.roll(b, b.shape[0] - 1, axis=0)


def _compress_weights(pe_k, wk, pe_v, wv):
    eye = jnp.eye(2 * KV_HEADS, dtype=F32)

    def half(lo):
        w = jnp.stack([wk[lo:lo + CMP_STRIDE], wk[lo:lo + CMP_STRIDE], wv[lo:lo + CMP_STRIDE], wv[lo:lo + CMP_STRIDE]], 1)
        big = jnp.einsum('lcde,cf->lcdfe', w, eye).reshape(CMP_STRIDE * 2 * KV_W, 2 * KV_W)
        pe = jnp.stack([pe_k[lo:lo + CMP_STRIDE], pe_k[lo:lo + CMP_STRIDE], pe_v[lo:lo + CMP_STRIDE], pe_v[lo:lo + CMP_STRIDE]], 1)
        return pe.reshape(1, CMP_STRIDE * 2 * KV_W), big.astype(BF16)

    pea, wa = half(0)
    peb, wb = half(CMP_STRIDE)
    return pea, peb, wa, wb


def _compress_call(x3, cw):
    B, nc, cw_in = x3.shape
    full = lambda a: pl.BlockSpec(a.shape, lambda b: (0, 0))
    return pl.pallas_call(
        _compress_kernel,
        grid=(B,),
        in_specs=[pl.BlockSpec((1, nc, cw_in), lambda b: (b, 0, 0))] + [full(a) for a in cw],
        out_specs=pl.BlockSpec((1, nc, 2 * KV_W), lambda b: (b, 0, 0)),
        out_shape=jax.ShapeDtypeStruct((B, nc, 2 * KV_W), F32),
        compiler_params=pltpu.CompilerParams(dimension_semantics=("parallel",), vmem_limit_bytes=VMEM_LIMIT),
        name="compress",
    )(x3, *cw)


SLC_TK = 256
SLC_C = SLC_TK + LANE
WIN_TK = Q_BLOCK


def _dot_nt(a, b):
    return lax.dot_general(a, b, (((1,), (1,)), ((), ())), preferred_element_type=F32)


def _half_select(slab, want_half, half):
    return jnp.where(half == want_half, slab, pltpu.roll(slab, HEAD_DIM, axis=1))


def _nsa_prompt_kernel(q_ref, gate_ref, kcvc_ref, kvs_ref, kvw_ref, bcmp_ref, bslc_ref, bwin_ref, et_ref, ovt_ref,
                       o_ref):
    g = pl.program_id(1)
    i = pl.program_id(2)
    Q = Q_BLOCK
    R = GQA_GROUP
    q0 = i * Q
    half = lax.broadcasted_iota(jnp.int32, (Q, LANE), 1) // HEAD_DIM

    qb = q_ref[...] * (HEAD_DIM ** -0.5)
    slabs = []
    for r in range(R):
        slab = qb[:, (r // 2) * LANE:(r // 2 + 1) * LANE]
        slabs.append(jnp.where(half == g, _half_select(slab, r % 2, half), 0.0))
    q_pad = jnp.concatenate(slabs, axis=0).astype(BF16)

    kc = kcvc_ref[0, :, :KV_W].astype(BF16)
    vc = kcvc_ref[0, :, KV_W:].astype(BF16)
    s = _dot_nt(q_pad, kc) + bcmp_ref[0, 0]
    e = jnp.exp(s - jnp.max(s, axis=-1, keepdims=True))
    p = e / jnp.sum(e, axis=-1, keepdims=True)
    row_pos = q0 + lax.broadcasted_iota(jnp.int32, (R * Q, 1), 0) % Q
    p = jnp.where(row_pos >= CMP_BLOCK - 1, p, 0.0)
    pb = p.astype(BF16)
    o_cmp = jnp.dot(pb, vc, preferred_element_type=F32)

    ovt = ovt_ref[...]
    imp = _dot_nt(ovt, pb[:Q])
    for r in range(1, R):
        imp = imp + _dot_nt(ovt, pb[r * Q:(r + 1) * Q])
    ns = imp.shape[0]
    jidx = lax.broadcasted_iota(jnp.int32, (ns, Q), 0)
    cur = (q0 + lax.broadcasted_iota(jnp.int32, (ns, Q), 1)) // SLC_BLOCK
    forced = (jidx == 0) | (jidx == cur) | (jidx == cur - 1)
    score = jnp.where(forced, FORCE, jnp.where(jidx <= cur, imp, NEG))
    rank = jnp.zeros((ns, Q), F32)
    for jp in range(ns):
        row = score[jp:jp + 1, :]
        rank = rank + jnp.where(jidx > jp, jnp.where(row >= score, 1.0, 0.0), jnp.where(row > score, 1.0, 0.0))
    selneg_t = jnp.where(rank < SLC_TOPK, 0.0, NEG)
    selneg_t = jnp.concatenate([selneg_t, jnp.zeros((LANE - ns, Q), F32)], axis=0)
    selneg = selneg_t.T.astype(BF16)
    lhs = jnp.concatenate([q_pad, jnp.concatenate([selneg] * R, axis=0)], axis=1)

    def flash_step(s, v_tile, carry):
        m, l, acc = carry
        m_new = jnp.maximum(m, jnp.max(s, axis=-1, keepdims=True))
        alpha = jnp.exp(m - m_new)
        pt = jnp.exp(s - m_new)
        l = alpha * l + jnp.sum(pt, axis=-1, keepdims=True)
        acc = alpha * acc + jnp.dot(pt.astype(BF16), v_tile, preferred_element_type=F32)
        return m_new, l, acc

    init = (jnp.full((R * Q, 1), NEG, F32), jnp.zeros((R * Q, 1), F32), jnp.zeros((R * Q, LANE), F32))

    def slc_body(t, carry):
        k0 = pl.multiple_of(t * SLC_TK, SLC_TK)
        kt = kvs_ref[pl.ds(k0, SLC_TK), :KV_W].astype(BF16)
        vt = kvs_ref[pl.ds(k0, SLC_TK), KV_W:].astype(BF16)
        rhs = jnp.concatenate([kt, et_ref[pl.ds(k0, SLC_TK), :]], axis=1)
        c0 = jnp.maximum(SLC_C - (q0 - k0), 0) // LANE
        bias = jnp.concatenate([bslc_ref[0, c0 + c] for c in range(SLC_TK // LANE)], axis=1)
        return flash_step(_dot_nt(lhs, rhs) + bias, vt, carry)

    m, l, acc = lax.fori_loop(0, q0 // SLC_TK + 1, slc_body, init)
    o_slc = acc / l

    def win_body(t, carry):
        k0 = pl.multiple_of(t * WIN_TK, WIN_TK)
        kt = kvw_ref[pl.ds(k0, WIN_TK), :KV_W].astype(BF16)
        vt = kvw_ref[pl.ds(k0, WIN_TK), KV_W:].astype(BF16)
        c0 = (WINDOW - (q0 - k0)) // LANE
        return flash_step(_dot_nt(q_pad, kt) + bwin_ref[0, c0], vt, carry)

    m, l, acc = lax.fori_loop(jnp.maximum(i - WINDOW // WIN_TK, 0), i + 1, win_body, init)
    o_win = acc / l

    gates = gate_ref[...]
    glane = lax.broadcasted_iota(jnp.int32, (Q, LANE), 1)

    def gate_col(branch):
        cols = [jnp.sum(jnp.where(glane == branch * N_HEADS + g * R + r, gates, 0.0), axis=-1, keepdims=True)
                for r in range(R)]
        return jnp.concatenate(cols, axis=0)

    o = gate_col(0) * o_cmp + gate_col(1) * o_slc + gate_col(2) * o_win
    for pair in range(R // 2):
        lo = _half_select(o[(2 * pair) * Q:(2 * pair + 1) * Q], g, half)
        hi = _half_select(o[(2 * pair + 1) * Q:(2 * pair + 2) * Q], g, half)
        o_ref[:, pair * LANE:(pair + 1) * LANE] = jnp.where(half == 0, lo, hi)


def _prompt_tables(tbl_flat, T):
    nqb = T // Q_BLOCK
    nc = T // CMP_STRIDE
    qi = jnp.arange(Q_BLOCK, dtype=jnp.int32)
    q_pos = jnp.arange(nqb, dtype=jnp.int32)[:, None, None] * Q_BLOCK + qi[None, :, None]
    c_end = jnp.arange(nc, dtype=jnp.int32) * CMP_STRIDE + (CMP_BLOCK - 1)
    big = 1 << 30
    bcmp = _bias_tables(tbl_flat, q_pos - c_end[None, None, :], big)

    def toeplitz(c_off, ncols, hi):
        cols = jnp.arange(ncols, dtype=jnp.int32)
        dist = qi[None, :, None] + c_off - cols.reshape(ncols // LANE, 1, LANE)
        return _bias_tables(tbl_flat, dist, hi)

    bslc = toeplitz(SLC_C, SLC_C + SLC_TK, big)
    bwin = toeplitz(WINDOW, WINDOW + WIN_TK, WINDOW)
    keys = jnp.arange(T, dtype=jnp.int32)
    et = (keys[:, None] // SLC_BLOCK == jnp.arange(LANE, dtype=jnp.int32)[None, :]).astype(BF16)
    c_start = jnp.arange(nc, dtype=jnp.int32) * CMP_STRIDE
    s_start = jnp.arange(T // SLC_BLOCK, dtype=jnp.int32) * SLC_BLOCK
    ovt = ((c_start[None, :] < s_start[:, None] + SLC_BLOCK)
           & (c_start[None, :] + CMP_BLOCK > s_start[:, None])).astype(BF16)
    return bcmp, bslc, bwin, et, ovt


def _nsa_prompt_call(q, gates, kcvc, kvs, kvw, tables, B, T):
    bcmp, bslc, bwin, et, ovt = tables
    nqb = T // Q_BLOCK
    nc = T // CMP_STRIDE
    R = GQA_GROUP
    return pl.pallas_call(
        _nsa_prompt_kernel,
        grid=(B, KV_HEADS, nqb),
        in_specs=[
            pl.BlockSpec((Q_BLOCK, R * HEAD_DIM), lambda b, g, i: (b * nqb + i, g)),
            pl.BlockSpec((Q_BLOCK, GATE_PAD), lambda b, g, i: (b * nqb + i, 0)),
            pl.BlockSpec((1, nc, 2 * KV_W), lambda b, g, i: (b, 0, 0)),
            pl.BlockSpec((T, 2 * KV_W), lambda b, g, i: (b, 0)),
            pl.BlockSpec((T, 2 * KV_W), lambda b, g, i: (b, 0)),
            pl.BlockSpec((1, 1, R * Q_BLOCK, nc), lambda b, g, i: (g, i, 0, 0)),
            pl.BlockSpec((1,) + bslc.shape[1:], lambda b, g, i: (g, 0, 0, 0)),
            pl.BlockSpec((1,) + bwin.shape[1:], lambda b, g, i: (g, 0, 0, 0)),
            pl.BlockSpec(et.shape, lambda b, g, i: (0, 0)),
            pl.BlockSpec(ovt.shape, lambda b, g, i: (0, 0)),
        ],
        out_specs=pl.BlockSpec((Q_BLOCK, R * HEAD_DIM), lambda b, g, i: (b * nqb + i, g)),
        out_shape=jax.ShapeDtypeStruct((B * T, NSA_W), F32),
        compiler_params=pltpu.CompilerParams(dimension_semantics=("parallel", "parallel", "arbitrary"),
                                             vmem_limit_bytes=VMEM_LIMIT),
        name="nsa_prompt",
    )(q, gates, kcvc, kvs, kvw, bcmp, bslc, bwin, et, ovt)


def _layernorm(x, g, b):
    mu = jnp.mean(x, axis=-1, keepdims=True)
    xc = x - mu
    var = jnp.mean(xc * xc, axis=-1, keepdims=True)
    return xc * lax.rsqrt(var + EPS) * g + b


def _t5_bucket(dist):
    n = jnp.maximum(dist, 0)
    max_exact = NUM_BUCKETS // 2
    nf = jnp.maximum(n, 1).astype(F32)
    large = max_exact + (jnp.log(nf / max_exact) / math.log(MAX_DISTANCE / max_exact)
                         * (NUM_BUCKETS - max_exact)).astype(jnp.int32)
    large = jnp.minimum(large, NUM_BUCKETS - 1)
    return jnp.where(n < max_exact, n, large)


def _masked_softmax(s, mask):
    s = jnp.where(mask, s.astype(F32), NEG)
    p = jax.nn.softmax(s, axis=-1)
    return p * jnp.any(mask, axis=-1, keepdims=True)


def _conv_mix(glu_ext, dw, b, ln_g, ln_b):
    y = lax.conv_general_dilated(glu_ext, dw[:, None, :], window_strides=(1,), padding='VALID',
                                 dimension_numbers=('NWC', 'WIO', 'NWC'), feature_group_count=CONV_W)
    return jax.nn.silu(_layernorm(y + b, ln_g, ln_b))


def _pool_mix(u_ext, pos0, w_pool, scale):
    B, L, _ = u_ext.shape
    T = L - POOL_BUF
    cs = jnp.pad(jnp.cumsum(u_ext, axis=1), ((0, 0), (1, 0), (0, 0)))
    end = cs[:, POOL_BUF + 1:POOL_BUF + 1 + T]
    u = u_ext[:, POOL_BUF:]
    t = pos0 + jnp.arange(T, dtype=jnp.int32)
    diffs = []
    for g, w in enumerate(POOL_WINDOWS):
        sl = slice(g * POOL_GROUP_W, (g + 1) * POOL_GROUP_W)
        start = cs[:, POOL_BUF + 1 - w:POOL_BUF + 1 - w + T, sl]
        cnt = jnp.minimum(w, t + 1).astype(F32)[:, None]
        diffs.append((end[..., sl] - start) / cnt - u[..., sl])
    d = jnp.stack(diffs, axis=2)
    y = jnp.einsum('btgc,gce->btge', d, w_pool).reshape(B, T, POOL_W)
    return y * scale


def _compress(k, pe, w):
    B, L = k.shape[:2]
    Lp = -(-L // CMP_STRIDE) * CMP_STRIDE
    k = jnp.pad(k, ((0, 0), (0, Lp - L), (0, 0), (0, 0)))
    sub = k.reshape(B, Lp // CMP_STRIDE, CMP_STRIDE, KV_HEADS, HEAD_DIM)
    blocks = jnp.concatenate([sub[:, :-1], sub[:, 1:]], axis=2)
    return jnp.einsum('bnlgd,lde->bnge', blocks + pe[:, None, :], w)


def _to_blocks(k):
    B, L = k.shape[:2]
    Lp = -(-L // SLC_BLOCK) * SLC_BLOCK
    k = jnp.pad(k, ((0, 0), (0, Lp - L), (0, 0), (0, 0)))
    return k.reshape(B, Lp // SLC_BLOCK, SLC_BLOCK, KV_HEADS, HEAD_DIM).transpose(0, 3, 1, 2, 4)


def _nsa_attend(q, q_pos, kc, vc, ks, vs, kw, vw, kw_pos, tbl):
    B, Q = q.shape[:2]
    scale = HEAD_DIM ** -0.5
    n_cmp, n_slc = kc.shape[1], ks.shape[2]
    c_end = jnp.arange(n_cmp, dtype=jnp.int32) * CMP_STRIDE + (CMP_BLOCK - 1)
    dist = q_pos[:, None] - c_end[None, :]
    s = jnp.einsum('bqgrd,bngd->bgrqn', q, kc).astype(F32) * scale + tbl[:, :, _t5_bucket(dist)]
    p_cmp = _masked_softmax(s, dist >= 0)
    o_cmp = jnp.einsum('bgrqn,bngd->bqgrd', p_cmp.astype(vc.dtype), vc)
    c_start = jnp.arange(n_cmp, dtype=jnp.int32) * CMP_STRIDE
    s_start = jnp.arange(n_slc, dtype=jnp.int32) * SLC_BLOCK
    overlap = ((c_start[:, None] < s_start[None, :] + SLC_BLOCK)
               & (c_start[:, None] + CMP_BLOCK > s_start[None, :])).astype(F32)
    p_slc = jnp.einsum('bgrqn,nj->bgqj', p_cmp, overlap)
    blk = jnp.arange(n_slc, dtype=jnp.int32)[None, :]
    cur = (q_pos // SLC_BLOCK)[:, None]
    forced = (blk == 0) | (blk == cur) | (blk == cur - 1)
    score = jnp.where(forced, FORCE, jnp.where(blk <= cur, p_slc, NEG))
    _, idx = lax.top_k(score, min(SLC_TOPK, n_slc))
    n_sel = idx.shape[-1]
    M = n_sel * SLC_BLOCK
    bi = jnp.arange(B)[:, None, None, None]
    gi = jnp.arange(KV_HEADS)[None, :, None, None]
    kg = ks[bi, gi, idx].reshape(B, KV_HEADS, Q, M, HEAD_DIM)
    vg = vs[bi, gi, idx].reshape(B, KV_HEADS, Q, M, HEAD_DIM)
    pos = (idx[..., None] * SLC_BLOCK + jnp.arange(SLC_BLOCK, dtype=jnp.int32)).reshape(B, KV_HEADS, Q, M)
    dist = q_pos[None, None, :, None] - pos
    bias = tbl[gi[..., None], jnp.arange(GQA_GROUP)[None, None, :, None, None], _t5_bucket(dist)[:, :, None]]
    s = jnp.einsum('bqgrd,bgqmd->bgrqm', q, kg).astype(F32) * scale + bias
    p = _masked_softmax(s, (dist >= 0)[:, :, None])
    o_slc = jnp.einsum('bgrqm,bgqmd->bqgrd', p.astype(vg.dtype), vg)
    dist = q_pos[:, None] - kw_pos[None, :]
    mask = (dist >= 0) & (dist < WINDOW) & (kw_pos[None, :] >= 0)
    s = jnp.einsum('bqgrd,bkgd->bgrqk', q, kw).astype(F32) * scale + tbl[:, :, _t5_bucket(dist)]
    p = _masked_softmax(s, mask)
    o_win = jnp.einsum('bgrqk,bkgd->bqgrd', p.astype(vw.dtype), vw)
    return o_cmp, o_slc, o_win


def _gate_sum(gates, o_cmp, o_slc, o_win):
    return gates[:, :, 0] * o_cmp + gates[:, :, 1] * o_slc + gates[:, :, 2] * o_win


def _nsa_prompt(q, kv, gates, pe_k, wk, pe_v, wv, tbl):
    B, T = q.shape[:2]
    kc = _compress(kv[:, :, 0, 0], pe_k, wk)
    vc = _compress(kv[:, :, 0, 1], pe_v, wv)
    ks = _to_blocks(kv[:, :, 1, 0])
    vs = _to_blocks(kv[:, :, 1, 1])
    pad = ((0, 0), (WINDOW, 0), (0, 0), (0, 0))
    kw = jnp.pad(kv[:, :, 2, 0], pad)
    vw = jnp.pad(kv[:, :, 2, 1], pad)

    def block(i):
        q0 = i * Q_BLOCK
        qb = lax.dynamic_slice_in_dim(q, q0, Q_BLOCK, axis=1)
        gb = lax.dynamic_slice_in_dim(gates, q0, Q_BLOCK, axis=1)
        kwb = lax.dynamic_slice_in_dim(kw, q0, WINDOW + Q_BLOCK, axis=1)
        vwb = lax.dynamic_slice_in_dim(vw, q0, WINDOW + Q_BLOCK, axis=1)
        q_pos = q0 + jnp.arange(Q_BLOCK, dtype=jnp.int32)
        kw_pos = q0 - WINDOW + jnp.arange(WINDOW + Q_BLOCK, dtype=jnp.int32)
        return _gate_sum(gb, *_nsa_attend(qb, q_pos, kc, vc, ks, vs, kwb, vwb, kw_pos, tbl))

    o = lax.map(block, jnp.arange(T // Q_BLOCK, dtype=jnp.int32))
    return jnp.moveaxis(o, 0, 1).reshape(B, T, NSA_W)


def _nsa_sample(q, kv, gates, cmp_pages, slc_pages, win_buf, page_table, pe_k, wk, pe_v, wv, tbl):
    B, S = q.shape[:2]
    past = page_table.shape[1] * cmp_pages.shape[1]
    cmp_full = jnp.concatenate([cmp_pages[page_table].reshape(B, past, 2, KV_HEADS, HEAD_DIM), kv[:, :, 0]], axis=1)
    slc_full = jnp.concatenate([slc_pages[page_table].reshape(B, past, 2, KV_HEADS, HEAD_DIM), kv[:, :, 1]], axis=1)
    win = jnp.concatenate([win_buf, kv[:, :, 2]], axis=1)
    wb = win_buf.shape[1]
    kc = _compress(cmp_full[:, :, 0], pe_k, wk)
    vc = _compress(cmp_full[:, :, 1], pe_v, wv)
    ks = _to_blocks(slc_full[:, :, 0])
    vs = _to_blocks(slc_full[:, :, 1])
    q_pos = past + jnp.arange(S, dtype=jnp.int32)
    kw_pos = past - wb + jnp.arange(wb + S, dtype=jnp.int32)
    o = _gate_sum(gates, *_nsa_attend(q, q_pos, kc, vc, ks, vs, win[:, :, 0], win[:, :, 1], kw_pos, tbl))
    return o.reshape(B, S, NSA_W), win[:, S:]


def _group_layer(x, l, pos0, hist_conv, hist_pool, nsa_fn, P, final):
    B, T, _ = x.shape
    n = B * T
    tm = min(n, 1024)
    glu, u, q, kvc, kvs, kvw, gates = _inproj(x.reshape(n, D_MODEL), P['norm1'][l], P['w_in'][l], min(n, 512))
    glu = glu.reshape(B, T, CONV_W)
    u = u.reshape(B, T, POOL_W)
    glu_ext = jnp.concatenate([hist_conv, glu], axis=1)
    u_ext = jnp.concatenate([hist_pool, u], axis=1)
    conv_o = _conv_mix(glu_ext, P['conv_dw'][l], P['conv_b'][l], P['conv_ln_g'][l], P['conv_ln_b'][l])
    pool_o = _pool_mix(u_ext, pos0, P['pool_w'][l], P['pool_scale'][l])
    nsa_o, extra = nsa_fn(q, gates, kvc, kvs, kvw)
    y = _ffn(x.reshape(n, D_MODEL), conv_o.reshape(n, CONV_W), pool_o.reshape(n, POOL_W),
             nsa_o.reshape(n, NSA_W), P['w_out'][l], P['norm2'][l], P['w_gu'][l], P['w_down'][l],
             P['final_norm'], tm, 256, final)
    shp = (B, T, 2, KV_HEADS, HEAD_DIM)
    return (y.reshape(B, T, D_MODEL), kvc.reshape(shp), kvs.reshape(shp), kvw.reshape(shp),
            glu_ext[:, -CONV_BUF:], u_ext[:, -POOL_BUF:], extra)


def kernel(x_prompt, x_sample, cache_cmp_kv, cache_slc_kv, cache_win_kv, state_conv, state_pool, page_table, rel_bias, norm1, w_in, conv_dw, conv_b, conv_ln_g, conv_ln_b, pool_w, pool_scale, cmp_pe_k, cmp_wk, cmp_pe_v, cmp_wv, w_out, norm2, w_gu, w_down, final_norm):
    depth = w_in.shape[0]
    tbl = rel_bias.reshape(NUM_BUCKETS, KV_HEADS, GQA_GROUP).transpose(1, 2, 0)
    T = x_prompt.shape[1]
    past = page_table.shape[1] * cache_cmp_kv.shape[2]
    P = dict(
        norm1=norm1[:, None, :], norm2=norm2[:, None, :], final_norm=final_norm[None, :],
        w_in=jnp.pad(w_in, ((0, 0), (0, 0), (0, IN_W_PAD - IN_W))).astype(BF16),
        w_out=w_out.astype(BF16), w_gu=w_gu.astype(BF16), w_down=w_down.astype(BF16),
        conv_dw=conv_dw, conv_b=conv_b, conv_ln_g=conv_ln_g, conv_ln_b=conv_ln_b,
        pool_w=pool_w, pool_scale=pool_scale)
    xp, xs = x_prompt, x_sample
    ptables = _prompt_tables(tbl.reshape(-1), T)
    Bp, Bs = xp.shape[0], xs.shape[0]
    outs_p, outs_s = [], []
    for l in range(depth):
        final = l == depth - 1
        cw = _compress_weights(cmp_pe_k[l], cmp_wk[l], cmp_pe_v[l], cmp_wv[l])

        def nsa_p(q, gates, kvc, kvs, kvw):
            kcvc = _compress_call(kvc.reshape(Bp, T // CMP_STRIDE, CMP_STRIDE * 2 * KV_W), cw)
            return _nsa_prompt_call(q, gates, kcvc, kvs, kvw, ptables, Bp, T), None

        r = _group_layer(xp, l, 0, jnp.zeros((Bp, CONV_BUF, CONV_W), F32), jnp.zeros((Bp, POOL_BUF, POOL_W), F32),
                         nsa_p, P, final)
        xp = r[0]
        outs_p.append((r[1], r[2], r[3][:, T - min(WINDOW, T):], r[4], r[5]))
        def nsa_s(q, gates, kvc, kvs, kvw):
            S = xs.shape[1]
            q5 = q.reshape(Bs, S, KV_HEADS, GQA_GROUP, HEAD_DIM)
            kv6 = jnp.stack([kvc, kvs, kvw], axis=1).reshape(Bs, S, 3, 2, KV_HEADS, HEAD_DIM)
            g6 = gates[:, :3 * N_HEADS].reshape(Bs, S, 3, KV_HEADS, GQA_GROUP, 1)
            return _nsa_sample(q5, kv6, g6, cache_cmp_kv[l], cache_slc_kv[l], cache_win_kv[l],
                               page_table, cmp_pe_k[l], cmp_wk[l], cmp_pe_v[l], cmp_wv[l], tbl)

        r = _group_layer(xs, l, past, state_conv[l], state_pool[l], nsa_s, P, final)
        xs = r[0]
        outs_s.append((r[1], r[2], r[6], r[4], r[5]))
    st = lambda outs, i: jnp.stack([o[i] for o in outs])
    return (xp, xs,
            st(outs_p, 0), st(outs_p, 1), st(outs_p, 2), st(outs_p, 3), st(outs_p, 4),
            st(outs_s, 0), st(outs_s, 1), st(outs_s, 2), st(outs_s, 3), st(outs_s, 4))
```

```python
import functools
import math

import jax
import jax.numpy as jnp
from jax import lax
from jax.experimental import pallas as pl
from jax.experimental.pallas import tpu as pltpu

D_MODEL = 1024
CONV_W = 256
POOL_W = 256
N_HEADS = 8
HEAD_DIM = 64
NSA_W = N_HEADS * HEAD_DIM
KV_HEADS = 2
GQA_GROUP = N_HEADS // KV_HEADS
KV_W = KV_HEADS * HEAD_DIM
CONV_WIDTH = 31
CONV_BUF = CONV_WIDTH - 1
POOL_WINDOWS = (2, 4, 8, 16)
POOL_GROUP_W = POOL_W // len(POOL_WINDOWS)
POOL_BUF = max(POOL_WINDOWS) - 1
CMP_STRIDE = 16
CMP_BLOCK = 2 * CMP_STRIDE
SLC_BLOCK = 64
SLC_TOPK = 16
WINDOW = 512
Q_BLOCK = 128
NUM_BUCKETS = 32
MAX_DISTANCE = 128
D_FF = 2816
EPS = 1e-6
NEG = -1e30
FORCE = 1e4

OFF_POOL = 2 * CONV_W
OFF_Q = OFF_POOL + POOL_W
OFF_KV = OFF_Q + NSA_W
OFF_GATE = OFF_KV + 6 * KV_W
IN_W = OFF_GATE + 3 * N_HEADS
LANE = 128
IN_W_PAD = -(-IN_W // LANE) * LANE
GATE_PAD = IN_W_PAD - OFF_GATE

VMEM_LIMIT = 56 * 1024 * 1024

F32 = jnp.float32
BF16 = jnp.bfloat16


def _rms(x, g):
    return x * lax.rsqrt(jnp.mean(x * x, axis=-1, keepdims=True) + EPS) * g


def _inproj_kernel(x_ref, g_ref, w_ref, glu_ref, u_ref, q_ref, kvc_ref, kvs_ref, kvw_ref, gate_ref):
    h = _rms(x_ref[...], g_ref[...]).astype(BF16)
    z = jnp.dot(h, w_ref[...], preferred_element_type=F32)
    glu_ref[...] = z[:, :CONV_W] * jax.nn.sigmoid(z[:, CONV_W:OFF_POOL])
    u_ref[...] = z[:, OFF_POOL:OFF_Q]
    q_ref[...] = z[:, OFF_Q:OFF_KV]
    kvc_ref[...] = z[:, OFF_KV:OFF_KV + 2 * KV_W]
    kvs_ref[...] = z[:, OFF_KV + 2 * KV_W:OFF_KV + 4 * KV_W]
    kvw_ref[...] = z[:, OFF_KV + 4 * KV_W:OFF_GATE]
    gate_ref[...] = jax.nn.sigmoid(z[:, OFF_GATE:])


def _inproj(x2d, g, w_pad, tm):
    n = x2d.shape[0]
    row = lambda w: pl.BlockSpec((tm, w), lambda i: (i, 0))
    full = lambda a: pl.BlockSpec(a.shape, lambda i: (0, 0))
    widths = (CONV_W, POOL_W, NSA_W, 2 * KV_W, 2 * KV_W, 2 * KV_W, GATE_PAD)
    return pl.pallas_call(
        _inproj_kernel,
        grid=(n // tm,),
        in_specs=[row(D_MODEL), full(g), full(w_pad)],
        out_specs=[row(w) for w in widths],
        out_shape=[jax.ShapeDtypeStruct((n, w), F32) for w in widths],
        compiler_params=pltpu.CompilerParams(dimension_semantics=("parallel",), vmem_limit_bytes=VMEM_LIMIT),
        name="inproj",
    )(x2d, g, w_pad)


def _ffn_kernel(x_ref, conv_ref, pool_ref, nsa_ref, wo_ref, g2_ref, wg_ref, wu_ref, wd_ref, gf_ref,
                y_ref, x1_ref, h2_ref, acc_ref, *, final_norm):
    j = pl.program_id(1)

    @pl.when(j == 0)
    def _():
        mix = jnp.concatenate([conv_ref[...], pool_ref[...], nsa_ref[...]], axis=-1).astype(BF16)
        x1 = x_ref[...] + jnp.dot(mix, wo_ref[...], preferred_element_type=F32)
        x1_ref[...] = x1
        h2_ref[...] = _rms(x1, g2_ref[...]).astype(BF16)
        acc_ref[...] = jnp.zeros_like(acc_ref)

    h2 = h2_ref[...]
    gate = jnp.dot(h2, wg_ref[...], preferred_element_type=F32)
    up = jnp.dot(h2, wu_ref[...], preferred_element_type=F32)
    act = (jax.nn.silu(gate) * up).astype(BF16)
    acc_ref[...] += jnp.dot(act, wd_ref[...], preferred_element_type=F32)

    @pl.when(j == pl.num_programs(1) - 1)
    def _():
        y = x1_ref[...] + acc_ref[...]
        if final_norm:
            y = _rms(y, gf_ref[...])
        y_ref[...] = y


def _ffn(x2d, conv_o, pool_o, nsa_o, w_out, g2, w_gu, w_down, gf, tm, tf, final_norm):
    n = x2d.shape[0]
    nf = D_FF // tf
    row = lambda w: pl.BlockSpec((tm, w), lambda i, j: (i, 0))
    full = lambda a: pl.BlockSpec(a.shape, lambda i, j: (0, 0))
    return pl.pallas_call(
        functools.partial(_ffn_kernel, final_norm=final_norm),
        grid=(n // tm, nf),
        in_specs=[row(D_MODEL), row(CONV_W), row(POOL_W), row(NSA_W), full(w_out), full(g2),
                  pl.BlockSpec((D_MODEL, tf), lambda i, j: (0, j)),
                  pl.BlockSpec((D_MODEL, tf), lambda i, j: (0, j + nf)),
                  pl.BlockSpec((tf, D_MODEL), lambda i, j: (j, 0)),
                  full(gf)],
        out_specs=row(D_MODEL),
        out_shape=jax.ShapeDtypeStruct((n, D_MODEL), F32),
        scratch_shapes=[pltpu.VMEM((tm, D_MODEL), F32), pltpu.VMEM((tm, D_MODEL), BF16),
                        pltpu.VMEM((tm, D_MODEL), F32)],
        compiler_params=pltpu.CompilerParams(dimension_semantics=("parallel", "arbitrary"),
                                             vmem_limit_bytes=VMEM_LIMIT),
        name="ffn",
    )(x2d, conv_o, pool_o, nsa_o, w_out, g2, w_gu, w_gu, w_down, gf)


def _bias_kernel(tbl_ref, dist_ref, out_ref, *, hi):
    g = pl.program_id(0)
    d = dist_ref[0]
    rows = d.shape[0]
    n = jnp.maximum(d, 0)
    max_exact = NUM_BUCKETS // 2
    nf = jnp.maximum(n, 1).astype(F32)
    large = max_exact + (jnp.log(nf / max_exact) / math.log(MAX_DISTANCE / max_exact)
                         * (NUM_BUCKETS - max_exact)).astype(jnp.int32)
    large = jnp.minimum(large, NUM_BUCKETS - 1)
    bucket = jnp.where(n < max_exact, n, large)
    valid = (d >= 0) & (d < hi)
    for r in range(GQA_GROUP):
        acc = jnp.zeros(d.shape, F32)
        for b in range(NUM_BUCKETS):
            acc = jnp.where(bucket == b, tbl_ref[(g * GQA_GROUP + r) * NUM_BUCKETS + b], acc)
        out_ref[0, 0, r * rows:(r + 1) * rows, :] = jnp.where(valid, acc, NEG)


def _bias_tables(tbl_flat, dist, hi):
    nblk, R, C = dist.shape
    return pl.pallas_call(
        functools.partial(_bias_kernel, hi=hi),
        grid_spec=pltpu.PrefetchScalarGridSpec(
            num_scalar_prefetch=1,
            grid=(KV_HEADS, nblk),
            in_specs=[pl.BlockSpec((1, R, C), lambda g, i, tbl: (i, 0, 0))],
            out_specs=pl.BlockSpec((1, 1, GQA_GROUP * R, C), lambda g, i, tbl: (g, i, 0, 0)),
        ),
        out_shape=jax.ShapeDtypeStruct((KV_HEADS, nblk, GQA_GROUP * R, C), F32),
        compiler_params=pltpu.CompilerParams(dimension_semantics=("parallel", "parallel")),
        name="bias_tables",
    )(tbl_flat, dist)


def _compress_kernel(x_ref, pea_ref, peb_ref, wa_ref, wb_ref, o_ref):
    x = x_ref[0]
    a = jnp.dot((x + pea_ref[...]).astype(BF16), wa_ref[...], preferred_element_type=F32)
    b = jnp.dot((x + peb_ref[...]).astype(BF16), wb_ref[...], preferred_element_type=F32)
    o_ref[0] = a + pltpu.roll(b, b.shape[0] - 1, axis=0)


def _compress_weights(pe_k, wk, pe_v, wv):
    eye = jnp.eye(2 * KV_HEADS, dtype=F32)

    def half(lo):
        w = jnp.stack([wk[lo:lo + CMP_STRIDE], wk[lo:lo + CMP_STRIDE], wv[lo:lo + CMP_STRIDE], wv[lo:lo + CMP_STRIDE]], 1)
        big = jnp.einsum('lcde,cf->lcdfe', w, eye).reshape(CMP_STRIDE * 2 * KV_W, 2 * KV_W)
        pe = jnp.stack([pe_k[lo:lo + CMP_STRIDE], pe_k[lo:lo + CMP_STRIDE], pe_v[lo:lo + CMP_STRIDE], pe_v[lo:lo + CMP_STRIDE]], 1)
        return pe.reshape(1, CMP_STRIDE * 2 * KV_W), big.astype(BF16)

    pea, wa = half(0)
    peb, wb = half(CMP_STRIDE)
    return pea, peb, wa, wb


def _compress_call(x3, cw):
    B, nc, cw_in = x3.shape
    full = lambda a: pl.BlockSpec(a.shape, lambda b: (0, 0))
    return pl.pallas_call(
        _compress_kernel,
        grid=(B,),
        in_specs=[pl.BlockSpec((1, nc, cw_in), lambda b: (b, 0, 0))] + [full(a) for a in cw],
        out_specs=pl.BlockSpec((1, nc, 2 * KV_W), lambda b: (b, 0, 0)),
        out_shape=jax.ShapeDtypeStruct((B, nc, 2 * KV_W), F32),
        compiler_params=pltpu.CompilerParams(dimension_semantics=("parallel",), vmem_limit_bytes=VMEM_LIMIT),
        name="compress",
    )(x3, *cw)


SLC_TK = 256
SLC_C = SLC_TK + LANE
WIN_TK = Q_BLOCK


def _dot_nt(a, b):
    return lax.dot_general(a, b, (((1,), (1,)), ((), ())), preferred_element_type=F32)


def _half_select(slab, want_half, half):
    return jnp.where(half == want_half, slab, pltpu.roll(slab, HEAD_DIM, axis=1))


def _nsa_prompt_kernel(q_ref, gate_ref, kcvc_ref, kvs_ref, kvw_ref, bcmp_ref, bslc_ref, bwin_ref, et_ref, ovt_ref,
                       o_ref):
    g = pl.program_id(1)
    i = pl.program_id(2)
    Q = Q_BLOCK
    R = GQA_GROUP
    q0 = i * Q
    half = lax.broadcasted_iota(jnp.int32, (Q, LANE), 1) // HEAD_DIM

    qb = q_ref[...] * (HEAD_DIM ** -0.5)
    slabs = []
    for r in range(R):
        slab = qb[:, (r // 2) * LANE:(r // 2 + 1) * LANE]
        slabs.append(jnp.where(half == g, _half_select(slab, r % 2, half), 0.0))
    q_pad = jnp.concatenate(slabs, axis=0).astype(BF16)

    kc = kcvc_ref[0, :, :KV_W].astype(BF16)
    vc = kcvc_ref[0, :, KV_W:].astype(BF16)
    s = _dot_nt(q_pad, kc) + bcmp_ref[0, 0]
    e = jnp.exp(s - jnp.max(s, axis=-1, keepdims=True))
    p = e / jnp.sum(e, axis=-1, keepdims=True)
    row_pos = q0 + lax.broadcasted_iota(jnp.int32, (R * Q, 1), 0) % Q
    p = jnp.where(row_pos >= CMP_BLOCK - 1, p, 0.0)
    pb = p.astype(BF16)
    o_cmp = jnp.dot(pb, vc, preferred_element_type=F32)

    ovt = ovt_ref[...]
    imp = _dot_nt(ovt, pb[:Q])
    for r in range(1, R):
        imp = imp + _dot_nt(ovt, pb[r * Q:(r + 1) * Q])
    ns = imp.shape[0]
    jidx = lax.broadcasted_iota(jnp.int32, (ns, Q), 0)
    cur = (q0 + lax.broadcasted_iota(jnp.int32, (ns, Q), 1)) // SLC_BLOCK
    forced = (jidx == 0) | (jidx == cur) | (jidx == cur - 1)
    score = jnp.where(forced, FORCE, jnp.where(jidx <= cur, imp, NEG))
    rank = jnp.zeros((ns, Q), F32)
    for jp in range(ns):
        row = score[jp:jp + 1, :]
        rank = rank + jnp.where(jidx > jp, jnp.where(row >= score, 1.0, 0.0), jnp.where(row > score, 1.0, 0.0))
    selneg_t = jnp.where(rank < SLC_TOPK, 0.0, NEG)
    selneg_t = jnp.concatenate([selneg_t, jnp.zeros((LANE - ns, Q), F32)], axis=0)
    selneg = selneg_t.T.astype(BF16)
    lhs = jnp.concatenate([q_pad, jnp.concatenate([selneg] * R, axis=0)], axis=1)

    def flash_step(s, v_tile, carry):
        m, l, acc = carry
        m_new = jnp.maximum(m, jnp.max(s, axis=-1, keepdims=True))
        alpha = jnp.exp(m - m_new)
        pt = jnp.exp(s - m_new)
        l = alpha * l + jnp.sum(pt, axis=-1, keepdims=True)
        acc = alpha * acc + jnp.dot(pt.astype(BF16), v_tile, preferred_element_type=F32)
        return m_new, l, acc

    init = (jnp.full((R * Q, 1), NEG, F32), jnp.zeros((R * Q, 1), F32), jnp.zeros((R * Q, LANE), F32))

    def slc_body(t, carry):
        k0 = pl.multiple_of(t * SLC_TK, SLC_TK)
        kt = kvs_ref[pl.ds(k0, SLC_TK), :KV_W].astype(BF16)
        vt = kvs_ref[pl.ds(k0, SLC_TK), KV_W:].astype(BF16)
        rhs = jnp.concatenate([kt, et_ref[pl.ds(k0, SLC_TK), :]], axis=1)
        c0 = jnp.maximum(SLC_C - (q0 - k0), 0) // LANE
        bias = jnp.concatenate([bslc_ref[0, c0 + c] for c in range(SLC_TK // LANE)], axis=1)
        return flash_step(_dot_nt(lhs, rhs) + bias, vt, carry)

    m, l, acc = lax.fori_loop(0, q0 // SLC_TK + 1, slc_body, init)
    o_slc = acc / l

    def win_body(t, carry):
        k0 = pl.multiple_of(t * WIN_TK, WIN_TK)
        kt = kvw_ref[pl.ds(k0, WIN_TK), :KV_W].astype(BF16)
        vt = kvw_ref[pl.ds(k0, WIN_TK), KV_W:].astype(BF16)
        c0 = (WINDOW - (q0 - k0)) // LANE
        return flash_step(_dot_nt(q_pad, kt) + bwin_ref[0, c0], vt, carry)

    m, l, acc = lax.fori_loop(jnp.maximum(i - WINDOW // WIN_TK, 0), i + 1, win_body, init)
    o_win = acc / l

    gates = gate_ref[...]
    glane = lax.broadcasted_iota(jnp.int32, (Q, LANE), 1)

    def gate_col(branch):
        cols = [jnp.sum(jnp.where(glane == branch * N_HEADS + g * R + r, gates, 0.0), axis=-1, keepdims=True)
                for r in range(R)]
        return jnp.concatenate(cols, axis=0)

    o = gate_col(0) * o_cmp + gate_col(1) * o_slc + gate_col(2) * o_win
    for pair in range(R // 2):
        lo = _half_select(o[(2 * pair) * Q:(2 * pair + 1) * Q], g, half)
        hi = _half_select(o[(2 * pair + 1) * Q:(2 * pair + 2) * Q], g, half)
        o_ref[:, pair * LANE:(pair + 1) * LANE] = jnp.where(half == 0, lo, hi)


def _prompt_tables(tbl_flat, T):
    nqb = T // Q_BLOCK
    nc = T // CMP_STRIDE
    qi = jnp.arange(Q_BLOCK, dtype=jnp.int32)
    q_pos = jnp.arange(nqb, dtype=jnp.int32)[:, None, None] * Q_BLOCK + qi[None, :, None]
    c_end = jnp.arange(nc, dtype=jnp.int32) * CMP_STRIDE + (CMP_BLOCK - 1)
    big = 1 << 30
    bcmp = _bias_tables(tbl_flat, q_pos - c_end[None, None, :], big)

    def toeplitz(c_off, ncols, hi):
        cols = jnp.arange(ncols, dtype=jnp.int32)
        dist = qi[None, :, None] + c_off - cols.reshape(ncols // LANE, 1, LANE)
        return _bias_tables(tbl_flat, dist, hi)

    bslc = toeplitz(SLC_C, SLC_C + SLC_TK, big)
    bwin = toeplitz(WINDOW, WINDOW + WIN_TK, WINDOW)
    keys = jnp.arange(T, dtype=jnp.int32)
    et = (keys[:, None] // SLC_BLOCK == jnp.arange(LANE, dtype=jnp.int32)[None, :]).astype(BF16)
    c_start = jnp.arange(nc, dtype=jnp.int32) * CMP_STRIDE
    s_start = jnp.arange(T // SLC_BLOCK, dtype=jnp.int32) * SLC_BLOCK
    ovt = ((c_start[None, :] < s_start[:, None] + SLC_BLOCK)
           & (c_start[None, :] + CMP_BLOCK > s_start[:, None])).astype(BF16)
    return bcmp, bslc, bwin, et, ovt


def _nsa_prompt_call(q, gates, kcvc, kvs, kvw, tables, B, T):
    bcmp, bslc, bwin, et, ovt = tables
    nqb = T // Q_BLOCK
    nc = T // CMP_STRIDE
    R = GQA_GROUP
    return pl.pallas_call(
        _nsa_prompt_kernel,
        grid=(B, KV_HEADS, nqb),
        in_specs=[
            pl.BlockSpec((Q_BLOCK, R * HEAD_DIM), lambda b, g, i: (b * nqb + i, g)),
            pl.BlockSpec((Q_BLOCK, GATE_PAD), lambda b, g, i: (b * nqb + i, 0)),
            pl.BlockSpec((1, nc, 2 * KV_W), lambda b, g, i: (b, 0, 0)),
            pl.BlockSpec((T, 2 * KV_W), lambda b, g, i: (b, 0)),
            pl.BlockSpec((T, 2 * KV_W), lambda b, g, i: (b, 0)),
            pl.BlockSpec((1, 1, R * Q_BLOCK, nc), lambda b, g, i: (g, i, 0, 0)),
            pl.BlockSpec((1,) + bslc.shape[1:], lambda b, g, i: (g, 0, 0, 0)),
            pl.BlockSpec((1,) + bwin.shape[1:], lambda b, g, i: (g, 0, 0, 0)),
            pl.BlockSpec(et.shape, lambda b, g, i: (0, 0)),
            pl.BlockSpec(ovt.shape, lambda b, g, i: (0, 0)),
        ],
        out_specs=pl.BlockSpec((Q_BLOCK, R * HEAD_DIM), lambda b, g, i: (b * nqb + i, g)),
        out_shape=jax.ShapeDtypeStruct((B * T, NSA_W), F32),
        compiler_params=pltpu.CompilerParams(dimension_semantics=("parallel", "parallel", "arbitrary"),
                                             vmem_limit_bytes=VMEM_LIMIT),
        name="nsa_prompt",
    )(q, gates, kcvc, kvs, kvw, bcmp, bslc, bwin, et, ovt)


SUB = 8
PAGES_PER_STEP = 64


def _compress_pages_kernel(x_ref, pea_ref, peb_ref, wa_ref, wb_ref, o_ref):
    pb, ch, w = x_ref.shape
    x = x_ref[...].reshape(pb * ch, w)
    a = jnp.dot((x + pea_ref[...]).astype(BF16), wa_ref[...], preferred_element_type=F32)
    b = jnp.dot((x + peb_ref[...]).astype(BF16), wb_ref[...], preferred_element_type=F32)
    o_ref[...] = jnp.concatenate([a, b], axis=1).reshape(pb, ch, 4 * KV_W)


def _compress_pages(pages, layer, n_phys, cw):
    _, ch, w = pages.shape
    steps = n_phys // PAGES_PER_STEP
    full = lambda a: pl.BlockSpec(a.shape, lambda i: (0, 0))
    return pl.pallas_call(
        _compress_pages_kernel,
        grid=(steps,),
        in_specs=[pl.BlockSpec((PAGES_PER_STEP, ch, w), lambda i: (layer * steps + i, 0, 0))] + [full(a) for a in cw],
        out_specs=pl.BlockSpec((PAGES_PER_STEP, ch, 4 * KV_W), lambda i: (i, 0, 0)),
        out_shape=jax.ShapeDtypeStruct((n_phys, ch, 4 * KV_W), F32),
        compiler_params=pltpu.CompilerParams(dimension_semantics=("parallel",), vmem_limit_bytes=VMEM_LIMIT),
        name="compress_pages",
    )(pages, *cw)


def _nsa_sample_kernel(pt_ref, q_ref, gate_ref, ks_new_ref, kw_new_ref, *rest, n_pages, past, dec_seq):
    ab_refs = rest[:n_pages]
    slc_refs = rest[n_pages:2 * n_pages]
    win_ref, bcmp_ref, bslc_ref, bwin_ref, et_ref, ov_ref, o_ref, win_out_ref = rest[2 * n_pages:]
    R, G = GQA_GROUP, KV_HEADS
    rows = R * G * SUB
    half = lax.broadcasted_iota(jnp.int32, (SUB, LANE), 1) // HEAD_DIM

    qb = q_ref[0] * (HEAD_DIM ** -0.5)
    slabs = []
    for r in range(R):
        for g in range(G):
            h = g * R + r
            slab = qb[:, (h // 2) * LANE:(h // 2 + 1) * LANE]
            slabs.append(jnp.where(half == g, _half_select(slab, h % 2, half), 0.0))
    q_pad = jnp.concatenate(slabs, axis=0).astype(BF16)

    ab = jnp.concatenate([r_[0] for r_ in ab_refs], axis=0)
    nc = ab.shape[0]
    kcvc = ab[:, :2 * KV_W] + pltpu.roll(ab[:, 2 * KV_W:], nc - 1, axis=0)
    kc = kcvc[:, :KV_W].astype(BF16)
    vc = kcvc[:, KV_W:].astype(BF16)
    s = _dot_nt(q_pad, kc) + bcmp_ref[...]
    e = jnp.exp(s - jnp.max(s, axis=-1, keepdims=True))
    p = e / jnp.sum(e, axis=-1, keepdims=True)
    row_pos = past + lax.broadcasted_iota(jnp.int32, (rows, 1), 0) % SUB
    p = jnp.where(row_pos >= CMP_BLOCK - 1, p, 0.0)
    pb = p.astype(BF16)
    o_cmp = jnp.dot(pb, vc, preferred_element_type=F32)

    ov = ov_ref[...]
    imp = jnp.dot(pb[:G * SUB], ov, preferred_element_type=F32)
    for r in range(1, R):
        imp = imp + jnp.dot(pb[r * G * SUB:(r + 1) * G * SUB], ov, preferred_element_type=F32)
    jidx = lax.broadcasted_iota(jnp.int32, (G * SUB, LANE), 1)
    cur = (past + lax.broadcasted_iota(jnp.int32, (G * SUB, LANE), 0) % SUB) // SLC_BLOCK
    forced = (jidx == 0) | (jidx == cur) | (jidx == cur - 1)
    score = jnp.where(forced, FORCE, jnp.where(jidx <= cur, imp, NEG))
    n_blocks = (past + dec_seq + SLC_BLOCK - 1) // SLC_BLOCK
    rank = jnp.zeros((G * SUB, LANE), F32)
    for jp in range(n_blocks):
        col = score[:, jp:jp + 1]
        rank = rank + jnp.where(jidx > jp, jnp.where(col >= score, 1.0, 0.0), jnp.where(col > score, 1.0, 0.0))
    selneg = jnp.where(rank < min(SLC_TOPK, n_blocks), 0.0, NEG).astype(BF16)
    lhs = jnp.concatenate([q_pad, jnp.concatenate([selneg] * R, axis=0)], axis=1)

    def new_rows(ref):
        x = ref[0]
        pad = jnp.zeros((LANE - SUB, KV_W), F32)
        return (jnp.concatenate([x[:, :KV_W], pad], axis=0).astype(BF16),
                jnp.concatenate([x[:, KV_W:], pad], axis=0).astype(BF16))

    def attend(lhs_, k_tiles, v_tiles, bias, ext):
        parts = []
        for t, kt in enumerate(k_tiles):
            rhs = jnp.concatenate([kt, et_ref[t * LANE:(t + 1) * LANE, :]], axis=1) if ext else kt
            parts.append(_dot_nt(lhs_, rhs))
        s_ = jnp.concatenate(parts, axis=1) + bias
        e_ = jnp.exp(s_ - jnp.max(s_, axis=-1, keepdims=True))
        p_ = (e_ / jnp.sum(e_, axis=-1, keepdims=True)).astype(BF16)
        o_ = jnp.dot(p_[:, :LANE], v_tiles[0], preferred_element_type=F32)
        for t in range(1, len(v_tiles)):
            o_ = o_ + jnp.dot(p_[:, t * LANE:(t + 1) * LANE], v_tiles[t], preferred_element_type=F32)
        return o_

    k_new, v_new = new_rows(ks_new_ref)
    k_tiles = [r_[0, :, :KV_W].astype(BF16) for r_ in slc_refs] + [k_new]
    v_tiles = [r_[0, :, KV_W:].astype(BF16) for r_ in slc_refs] + [v_new]
    o_slc = attend(lhs, k_tiles, v_tiles, bslc_ref[...], True)

    k_new, v_new = new_rows(kw_new_ref)
    wb = win_ref.shape[1]
    k_tiles = [win_ref[0, t * LANE:(t + 1) * LANE, :KV_W].astype(BF16) for t in range(wb // LANE)] + [k_new]
    v_tiles = [win_ref[0, t * LANE:(t + 1) * LANE, KV_W:].astype(BF16) for t in range(wb // LANE)] + [v_new]
    o_win = attend(q_pad, k_tiles, v_tiles, bwin_ref[...], False)

    gates = gate_ref[0]
    glane = lax.broadcasted_iota(jnp.int32, (SUB, LANE), 1)

    def gate_col(branch):
        cols = [jnp.sum(jnp.where(glane == branch * N_HEADS + g * R + r, gates, 0.0), axis=-1, keepdims=True)
                for r in range(R) for g in range(G)]
        return jnp.concatenate(cols, axis=0)

    o = gate_col(0) * o_cmp + gate_col(1) * o_slc + gate_col(2) * o_win
    for g in range(G):
        for pair in range(R // 2):
            r0 = ((2 * pair) * G + g) * SUB
            r1 = ((2 * pair + 1) * G + g) * SUB
            lo = _half_select(o[r0:r0 + SUB], g, half)
            hi = _half_select(o[r1:r1 + SUB], g, half)
            c = (g * R // 2 + pair) * LANE
            o_ref[0, :, c:c + LANE] = jnp.where(half == 0, lo, hi)

    win = win_ref[0]
    shifted = pltpu.roll(win, wb - dec_seq, axis=0)
    win_out_ref[0, :wb - SUB, :] = shifted[:wb - SUB]
    srow = lax.broadcasted_iota(jnp.int32, (SUB, 2 * KV_W), 0)
    win_out_ref[0, wb - SUB:, :] = jnp.where(srow >= SUB - dec_seq, pltpu.roll(kw_new_ref[0], SUB - dec_seq, axis=0),
                                             shifted[wb - SUB:])


def _sample_tables(tbl_flat, past, dec_seq, wb):
    q_pos = past + jnp.arange(SUB, dtype=jnp.int32)
    nc = past // CMP_STRIDE
    c_end = jnp.arange(nc, dtype=jnp.int32) * CMP_STRIDE + (CMP_BLOCK - 1)
    slc_keys = jnp.arange(past + LANE, dtype=jnp.int32)
    win_pos = jnp.concatenate([past - wb + jnp.arange(wb, dtype=jnp.int32), past + jnp.arange(LANE, dtype=jnp.int32)])
    big = 1 << 30

    def table(key_pos, hi):
        t = _bias_tables(tbl_flat, (q_pos[:, None] - key_pos[None, :])[None], hi)
        c = t.shape[-1]
        return t.reshape(KV_HEADS, GQA_GROUP, SUB, c).transpose(1, 0, 2, 3).reshape(GQA_GROUP * KV_HEADS * SUB, c)

    bcmp = table(c_end, big)
    bslc = table(slc_keys, big)
    bwin = table(win_pos, WINDOW)
    def kill_pad(b):
        lane = jnp.arange(b.shape[1]) - (b.shape[1] - LANE)
        return jnp.where((lane >= dec_seq)[None, :], NEG, b)
    bslc, bwin = kill_pad(bslc), kill_pad(bwin)
    et = (slc_keys[:, None] // SLC_BLOCK == jnp.arange(LANE, dtype=jnp.int32)[None, :]).astype(BF16)
    c_start = jnp.arange(nc, dtype=jnp.int32) * CMP_STRIDE
    s_start = jnp.arange(LANE, dtype=jnp.int32) * SLC_BLOCK
    ov = ((c_start[:, None] < s_start[None, :] + SLC_BLOCK)
          & (c_start[:, None] + CMP_BLOCK > s_start[None, :])).astype(BF16)
    return bcmp, bslc, bwin, et, ov


def _nsa_sample_call(page_table_flat, q, gates, ks_new, kw_new, ab_pages, slc_pages, win_buf, tables, layer, n_phys,
                     dec_seq):
    bcmp, bslc, bwin, et, ov = tables
    B = q.shape[0]
    n_pages = page_table_flat.shape[0] // B
    page = slc_pages.shape[1]
    past = n_pages * page
    wb = win_buf.shape[1]
    row3 = lambda a: pl.BlockSpec((1,) + a.shape[1:], lambda b, pt: (b, 0, 0))
    full = lambda a: pl.BlockSpec(a.shape, lambda b, pt: (0, 0))
    ab_specs = [pl.BlockSpec((1,) + ab_pages.shape[1:], lambda b, pt, p=p: (pt[b * n_pages + p], 0, 0))
                for p in range(n_pages)]
    slc_specs = [pl.BlockSpec((1,) + slc_pages.shape[1:], lambda b, pt, p=p: (layer * n_phys + pt[b * n_pages + p], 0, 0))
                 for p in range(n_pages)]
    return pl.pallas_call(
        functools.partial(_nsa_sample_kernel, n_pages=n_pages, past=past, dec_seq=dec_seq),
        grid_spec=pltpu.PrefetchScalarGridSpec(
            num_scalar_prefetch=1,
            grid=(B,),
            in_specs=[row3(q), row3(gates), row3(ks_new), row3(kw_new)] + ab_specs + slc_specs
                     + [pl.BlockSpec((1, wb, 2 * KV_W), lambda b, pt: (layer * B + b, 0, 0)),
                        full(bcmp), full(bslc), full(bwin), full(et), full(ov)],
            out_specs=[pl.BlockSpec((1, SUB, NSA_W), lambda b, pt: (b, 0, 0)),
                       pl.BlockSpec((1, wb, 2 * KV_W), lambda b, pt: (b, 0, 0))],
        ),
        out_shape=[jax.ShapeDtypeStruct((B, SUB, NSA_W), F32), jax.ShapeDtypeStruct((B, wb, 2 * KV_W), F32)],
        compiler_params=pltpu.CompilerParams(dimension_semantics=("parallel",), vmem_limit_bytes=VMEM_LIMIT),
        name="nsa_sample",
    )(page_table_flat, q, gates, ks_new, kw_new, *([ab_pages] * n_pages), *([slc_pages] * n_pages), win_buf,
      bcmp, bslc, bwin, et, ov)


CONV_HALO = 32
POOL_HALO = 16


def _mix_kernel(glu_ref, glu_halo_ref, glu_hist_ref, u_ref, u_halo_ref, u_hist_ref, dw_ref, cb_ref, lng_ref, lnb_ref,
                wp_ref, ps_ref, conv_ref, pool_ref, xs_ref, us_ref, *, pos0):
    j = pl.program_id(1)
    tq = glu_ref.shape[0]

    @pl.when(j == 0)
    def _():
        xs_ref[:CONV_HALO, :] = glu_hist_ref[0]
        us_ref[:POOL_HALO, :] = u_hist_ref[0]

    @pl.when(j > 0)
    def _():
        xs_ref[:CONV_HALO, :] = glu_halo_ref[...]
        us_ref[:POOL_HALO, :] = u_halo_ref[...]

    xs_ref[CONV_HALO:, :] = glu_ref[...]
    us_ref[POOL_HALO:, :] = u_ref[...]

    base = CONV_HALO - CONV_BUF
    y = xs_ref[base:base + tq, :] * dw_ref[0:1, :]
    for k in range(1, CONV_WIDTH):
        y = y + xs_ref[base + k:base + k + tq, :] * dw_ref[k:k + 1, :]
    y = y + cb_ref[...]
    mu = jnp.mean(y, axis=-1, keepdims=True)
    yc = y - mu
    var = jnp.mean(yc * yc, axis=-1, keepdims=True)
    conv_ref[...] = jax.nn.silu(yc * lax.rsqrt(var + EPS) * lng_ref[...] + lnb_ref[...])

    u = u_ref[...]
    pos = pos0 + j * tq + lax.broadcasted_iota(jnp.int32, (tq, 1), 0)
    grp = lax.broadcasted_iota(jnp.int32, (tq, POOL_W), 1) // POOL_GROUP_W
    run = u
    d = jnp.zeros_like(u)
    done = 1
    for gi, w in enumerate(POOL_WINDOWS):
        for back in range(done, w):
            run = run + us_ref[POOL_HALO - back:POOL_HALO - back + tq, :]
        done = w
        cnt = jnp.minimum(w, pos + 1).astype(F32)
        d = jnp.where(grp == gi, run / cnt - u, d)
    yp = jnp.dot(d.astype(BF16), wp_ref[...], preferred_element_type=F32)
    pool_ref[...] = yp * ps_ref[...]


def _mix(glu, u, glu_hist, u_hist, dw, cb, lng, lnb, wp_bd, ps, B, T, tq, pos0):
    nt = T // tq
    row = pl.BlockSpec((tq, CONV_W), lambda b, j: (b * nt + j, 0))
    full = lambda a: pl.BlockSpec(a.shape, lambda b, j: (0, 0))

    def halo(h):
        return pl.BlockSpec((h, CONV_W), lambda b, j: (jnp.maximum(b * (T // h) + j * (tq // h) - 1, 0), 0))

    hist = lambda h: pl.BlockSpec((1, h, CONV_W), lambda b, j: (b, 0, 0))
    if T < CONV_HALO:
        glu_halo_arr, u_halo_arr = glu_hist.reshape(-1, CONV_W), u_hist.reshape(-1, POOL_W)
        halo_c = pl.BlockSpec((CONV_HALO, CONV_W), lambda b, j: (0, 0))
        halo_p = pl.BlockSpec((POOL_HALO, POOL_W), lambda b, j: (0, 0))
    else:
        glu_halo_arr, u_halo_arr = glu, u
        halo_c, halo_p = halo(CONV_HALO), halo(POOL_HALO)
    return pl.pallas_call(
        functools.partial(_mix_kernel, pos0=pos0),
        grid=(B, nt),
        in_specs=[row, halo_c, hist(CONV_HALO), row, halo_p, hist(POOL_HALO),
                  full(dw), full(cb), full(lng), full(lnb), full(wp_bd), full(ps)],
        out_specs=[row, row],
        out_shape=[jax.ShapeDtypeStruct((B * T, CONV_W), F32), jax.ShapeDtypeStruct((B * T, POOL_W), F32)],
        scratch_shapes=[pltpu.VMEM((CONV_HALO + tq, CONV_W), F32), pltpu.VMEM((POOL_HALO + tq, POOL_W), F32)],
        compiler_params=pltpu.CompilerParams(dimension_semantics=("parallel", "parallel"),
                                             vmem_limit_bytes=VMEM_LIMIT),
        name="conv_pool",
    )(glu, glu_halo_arr, glu_hist, u, u_halo_arr, u_hist, dw, cb, lng, lnb, wp_bd, ps)


def _layer(x2d, B, T, l, P, glu_hist, u_hist, pos0, nsa_fn, final):
    n = B * T
    glu, u, q, kvc, kvs, kvw, gates = _inproj(x2d, P['norm1'][l], P['w_in'][l], min(n, 512))
    conv_o, pool_o = _mix(glu, u, glu_hist, u_hist, P['conv_dw'][l], P['conv_b'][l], P['conv_ln_g'][l],
                          P['conv_ln_b'][l], P['pool_w'][l], P['pool_scale'][l], B, T, min(T, 512), pos0)
    nsa_o, extra = nsa_fn(q, gates, kvc, kvs, kvw)
    y = _ffn(x2d, conv_o, pool_o, nsa_o, P['w_out'][l], P['norm2'][l], P['w_gu'][l], P['w_down'][l],
             P['final_norm'], min(n, 1024), 256, final)
    return y, glu, u, kvc, kvs, kvw, extra


def kernel(x_prompt, x_sample, cache_cmp_kv, cache_slc_kv, cache_win_kv, state_conv, state_pool, page_table, rel_bias, norm1, w_in, conv_dw, conv_b, conv_ln_g, conv_ln_b, pool_w, pool_scale, cmp_pe_k, cmp_wk, cmp_pe_v, cmp_wv, w_out, norm2, w_gu, w_down, final_norm):
    depth = w_in.shape[0]
    Bp, T, _ = x_prompt.shape
    Bs, S, _ = x_sample.shape
    n_phys, page = cache_cmp_kv.shape[1:3]
    n_pages = page_table.shape[1]
    past = n_pages * page
    wb = cache_win_kv.shape[2]
    tbl_flat = rel_bias.reshape(NUM_BUCKETS, KV_HEADS, GQA_GROUP).transpose(1, 2, 0).reshape(-1)
    eye = jnp.eye(len(POOL_WINDOWS), dtype=F32)
    row = lambda a: a[:, None, :]
    P = dict(
        norm1=row(norm1), norm2=row(norm2), final_norm=final_norm[None, :],
        w_in=jnp.pad(w_in, ((0, 0), (0, 0), (0, IN_W_PAD - IN_W))).astype(BF16),
        w_out=w_out.astype(BF16), w_gu=w_gu.astype(BF16), w_down=w_down.astype(BF16),
        conv_dw=conv_dw, conv_b=row(conv_b), conv_ln_g=row(conv_ln_g), conv_ln_b=row(conv_ln_b),
        pool_w=jnp.einsum('lgce,gh->lgche', pool_w, eye).reshape(depth, POOL_W, POOL_W).astype(BF16),
        pool_scale=row(pool_scale))
    ptables = _prompt_tables(tbl_flat, T)
    stables = _sample_tables(tbl_flat, past, S, wb)
    cmp_pages = cache_cmp_kv.reshape(depth * n_phys, page // CMP_STRIDE, CMP_STRIDE * 2 * KV_W)
    slc_pages = cache_slc_kv.reshape(depth * n_phys, page, 2 * KV_W)
    win_bufs = cache_win_kv.reshape(depth * Bs, wb, 2 * KV_W)
    pt_flat = page_table.reshape(-1)

    xp = x_prompt.reshape(Bp * T, D_MODEL)
    xs = jnp.pad(x_sample, ((0, 0), (0, SUB - S), (0, 0))).reshape(Bs * SUB, D_MODEL)
    zc = jnp.zeros((Bp, CONV_HALO, CONV_W), F32)
    zp = jnp.zeros((Bp, POOL_HALO, POOL_W), F32)
    kv5 = lambda a, b, t: a.reshape(b, t, 2, KV_HEADS, HEAD_DIM)
    outs_p, outs_s = [], []
    for l in range(depth):
        final = l == depth - 1
        cw = _compress_weights(cmp_pe_k[l], cmp_wk[l], cmp_pe_v[l], cmp_wv[l])

        def nsa_p(q, gates, kvc, kvs, kvw):
            kcvc = _compress_call(kvc.reshape(Bp, T // CMP_STRIDE, CMP_STRIDE * 2 * KV_W), cw)
            return _nsa_prompt_call(q, gates, kcvc, kvs, kvw, ptables, Bp, T), None

        xp, glu, u, kvc, kvs, kvw, _ = _layer(xp, Bp, T, l, P, zc, zp, 0, nsa_p, final)
        outs_p.append((kv5(kvc, Bp, T), kv5(kvs, Bp, T), kv5(kvw, Bp, T)[:, T - min(WINDOW, T):],
                       glu.reshape(Bp, T, CONV_W)[:, T - CONV_BUF:], u.reshape(Bp, T, POOL_W)[:, T - POOL_BUF:]))

        def nsa_s(q, gates, kvc, kvs, kvw):
            ab = _compress_pages(cmp_pages, l, n_phys, cw)
            r3 = lambda a: a.reshape(Bs, SUB, a.shape[-1])
            o, new_win = _nsa_sample_call(pt_flat, r3(q), r3(gates), r3(kvs), r3(kvw), ab, slc_pages, win_bufs,
                                          stables, l, n_phys, S)
            return o.reshape(Bs * SUB, NSA_W), new_win

        glu_hist = jnp.pad(state_conv[l], ((0, 0), (CONV_HALO - CONV_BUF, 0), (0, 0)))
        u_hist = jnp.pad(state_pool[l], ((0, 0), (POOL_HALO - POOL_BUF, 0), (0, 0)))
        xs, glu, u, kvc, kvs, kvw, new_win = _layer(xs, Bs, SUB, l, P, glu_hist, u_hist, past, nsa_s, final)
        new_conv = jnp.concatenate([state_conv[l], glu.reshape(Bs, SUB, CONV_W)[:, :S]], axis=1)[:, -CONV_BUF:]
        new_pool = jnp.concatenate([state_pool[l], u.reshape(Bs, SUB, POOL_W)[:, :S]], axis=1)[:, -POOL_BUF:]
        outs_s.append((kv5(kvc, Bs, SUB)[:, :S], kv5(kvs, Bs, SUB)[:, :S], kv5(new_win, Bs, wb), new_conv, new_pool))
    st = lambda outs, i: jnp.stack([o[i] for o in outs])
    return (xp.reshape(Bp, T, D_MODEL), xs.reshape(Bs, SUB, D_MODEL)[:, :S],
            st(outs_p, 0), st(outs_p, 1), st(outs_p, 2), st(outs_p, 3), st(outs_p, 4),
            st(outs_s, 0), st(outs_s, 1), st(outs_s, 2), st(outs_s, 3), st(outs_s, 4))
```

```python
import functools
import math

import jax
import jax.numpy as jnp
from jax import lax
from jax.experimental import pallas as pl
from jax.experimental.pallas import tpu as pltpu

D_MODEL = 1024
CONV_W = 256
POOL_W = 256
N_HEADS = 8
HEAD_DIM = 64
NSA_W = N_HEADS * HEAD_DIM
KV_HEADS = 2
GQA_GROUP = N_HEADS // KV_HEADS
KV_W = KV_HEADS * HEAD_DIM
CONV_WIDTH = 31
CONV_BUF = CONV_WIDTH - 1
POOL_WINDOWS = (2, 4, 8, 16)
POOL_GROUP_W = POOL_W // len(POOL_WINDOWS)
POOL_BUF = max(POOL_WINDOWS) - 1
CMP_STRIDE = 16
CMP_BLOCK = 2 * CMP_STRIDE
SLC_BLOCK = 64
SLC_TOPK = 16
WINDOW = 512
Q_BLOCK = 128
NUM_BUCKETS = 32
MAX_DISTANCE = 128
D_FF = 2816
EPS = 1e-6
NEG = -1e30
FORCE = 1e4

OFF_POOL = 2 * CONV_W
OFF_Q = OFF_POOL + POOL_W
OFF_KV = OFF_Q + NSA_W
OFF_GATE = OFF_KV + 6 * KV_W
IN_W = OFF_GATE + 3 * N_HEADS
LANE = 128
IN_W_PAD = -(-IN_W // LANE) * LANE
GATE_PAD = IN_W_PAD - OFF_GATE

VMEM_LIMIT = 56 * 1024 * 1024

F32 = jnp.float32
BF16 = jnp.bfloat16


def _rms(x, g):
    return x * lax.rsqrt(jnp.mean(x * x, axis=-1, keepdims=True) + EPS) * g


def _inproj_kernel(x_ref, g_ref, w_ref, glu_ref, u_ref, q_ref, kvc_ref, kvs_ref, kvw_ref, gate_ref, *vt_refs):
    h = _rms(x_ref[...], g_ref[...]).astype(BF16)
    z = jnp.dot(h, w_ref[...], preferred_element_type=F32)
    glu_ref[...] = z[:, :CONV_W] * jax.nn.sigmoid(z[:, CONV_W:OFF_POOL])
    u_ref[...] = z[:, OFF_POOL:OFF_Q]
    q_ref[...] = z[:, OFF_Q:OFF_KV]
    kvc_ref[...] = z[:, OFF_KV:OFF_KV + 2 * KV_W]
    kvs_ref[...] = z[:, OFF_KV + 2 * KV_W:OFF_KV + 4 * KV_W]
    kvw_ref[...] = z[:, OFF_KV + 4 * KV_W:OFF_GATE]
    gate_ref[...] = jax.nn.sigmoid(z[:, OFF_GATE:])
    for vt_ref, off in zip(vt_refs, (OFF_KV + 3 * KV_W, OFF_KV + 5 * KV_W)):
        for c in range(vt_ref.shape[0]):
            vt_ref[c] = z[c * LANE:(c + 1) * LANE, off:off + KV_W].T.astype(BF16)


def _inproj(x2d, g, w_pad, tm, emit_vt):
    n = x2d.shape[0]
    row = lambda w: pl.BlockSpec((tm, w), lambda i: (i, 0))
    full = lambda a: pl.BlockSpec(a.shape, lambda i: (0, 0))
    widths = (CONV_W, POOL_W, NSA_W, 2 * KV_W, 2 * KV_W, 2 * KV_W, GATE_PAD)
    out_specs = [row(w) for w in widths]
    out_shape = [jax.ShapeDtypeStruct((n, w), F32) for w in widths]
    if emit_vt:
        out_specs += [pl.BlockSpec((tm // LANE, KV_W, LANE), lambda i: (i, 0, 0))] * 2
        out_shape += [jax.ShapeDtypeStruct((n // LANE, KV_W, LANE), BF16)] * 2
    return pl.pallas_call(
        _inproj_kernel,
        grid=(n // tm,),
        in_specs=[row(D_MODEL), full(g), full(w_pad)],
        out_specs=out_specs,
        out_shape=out_shape,
        compiler_params=pltpu.CompilerParams(dimension_semantics=("parallel",), vmem_limit_bytes=VMEM_LIMIT),
        name="inproj",
    )(x2d, g, w_pad)


def _ffn_kernel(x_ref, conv_ref, pool_ref, nsa_ref, wo_ref, g2_ref, wg_ref, wu_ref, wd_ref, gf_ref,
                y_ref, x1_ref, h2_ref, acc_ref, *, final_norm):
    j = pl.program_id(1)

    @pl.when(j == 0)
    def _():
        mix = jnp.concatenate([conv_ref[...], pool_ref[...], nsa_ref[...]], axis=-1).astype(BF16)
        x1 = x_ref[...] + jnp.dot(mix, wo_ref[...], preferred_element_type=F32)
        x1_ref[...] = x1
        h2_ref[...] = _rms(x1, g2_ref[...]).astype(BF16)
        acc_ref[...] = jnp.zeros_like(acc_ref)

    h2 = h2_ref[...]
    gate = jnp.dot(h2, wg_ref[...], preferred_element_type=F32)
    up = jnp.dot(h2, wu_ref[...], preferred_element_type=F32)
    act = (jax.nn.silu(gate) * up).astype(BF16)
    acc_ref[...] += jnp.dot(act, wd_ref[...], preferred_element_type=F32)

    @pl.when(j == pl.num_programs(1) - 1)
    def _():
        y = x1_ref[...] + acc_ref[...]
        if final_norm:
            y = _rms(y, gf_ref[...])
        y_ref[...] = y


def _ffn(x2d, conv_o, pool_o, nsa_o, w_out, g2, w_gu, w_down, gf, tm, tf, final_norm):
    n = x2d.shape[0]
    nf = D_FF // tf
    row = lambda w: pl.BlockSpec((tm, w), lambda i, j: (i, 0))
    full = lambda a: pl.BlockSpec(a.shape, lambda i, j: (0, 0))
    return pl.pallas_call(
        functools.partial(_ffn_kernel, final_norm=final_norm),
        grid=(n // tm, nf),
        in_specs=[row(D_MODEL), row(CONV_W), row(POOL_W), row(NSA_W), full(w_out), full(g2),
                  pl.BlockSpec((D_MODEL, tf), lambda i, j: (0, j)),
                  pl.BlockSpec((D_MODEL, tf), lambda i, j: (0, j + nf)),
                  pl.BlockSpec((tf, D_MODEL), lambda i, j: (j, 0)),
                  full(gf)],
        out_specs=row(D_MODEL),
        out_shape=jax.ShapeDtypeStruct((n, D_MODEL), F32),
        scratch_shapes=[pltpu.VMEM((tm, D_MODEL), F32), pltpu.VMEM((tm, D_MODEL), BF16),
                        pltpu.VMEM((tm, D_MODEL), F32)],
        compiler_params=pltpu.CompilerParams(dimension_semantics=("parallel", "arbitrary"),
                                             vmem_limit_bytes=VMEM_LIMIT),
        name="ffn",
    )(x2d, conv_o, pool_o, nsa_o, w_out, g2, w_gu, w_gu, w_down, gf)


def _bias_kernel(tbl_ref, dist_ref, out_ref, *, hi, heads_on_lanes):
    g = pl.program_id(0)
    d = dist_ref[0]
    rows, cols = d.shape
    n = jnp.maximum(d, 0)
    max_exact = NUM_BUCKETS // 2
    nf = jnp.maximum(n, 1).astype(F32)
    large = max_exact + (jnp.log(nf / max_exact) / math.log(MAX_DISTANCE / max_exact)
                         * (NUM_BUCKETS - max_exact)).astype(jnp.int32)
    large = jnp.minimum(large, NUM_BUCKETS - 1)
    bucket = jnp.where(n < max_exact, n, large)
    valid = (d >= 0) & (d < hi)
    for r in range(GQA_GROUP):
        acc = jnp.zeros(d.shape, F32)
        for b in range(NUM_BUCKETS):
            acc = jnp.where(bucket == b, tbl_ref[(g * GQA_GROUP + r) * NUM_BUCKETS + b], acc)
        val = jnp.where(valid, acc, NEG)
        if heads_on_lanes:
            out_ref[0, 0, :, r * cols:(r + 1) * cols] = val
        else:
            out_ref[0, 0, r * rows:(r + 1) * rows, :] = val


def _bias_tables(tbl_flat, dist, hi, heads_on_lanes=False):
    nblk, R, C = dist.shape
    oshape = (R, GQA_GROUP * C) if heads_on_lanes else (GQA_GROUP * R, C)
    return pl.pallas_call(
        functools.partial(_bias_kernel, hi=hi, heads_on_lanes=heads_on_lanes),
        grid_spec=pltpu.PrefetchScalarGridSpec(
            num_scalar_prefetch=1,
            grid=(KV_HEADS, nblk),
            in_specs=[pl.BlockSpec((1, R, C), lambda g, i, tbl: (i, 0, 0))],
            out_specs=pl.BlockSpec((1, 1) + oshape, lambda g, i, tbl: (g, i, 0, 0)),
        ),
        out_shape=jax.ShapeDtypeStruct((KV_HEADS, nblk) + oshape, F32),
        compiler_params=pltpu.CompilerParams(dimension_semantics=("parallel", "parallel")),
        name="bias_tables",
    )(tbl_flat, dist)


def _compress_kernel(x_ref, pea_ref, peb_ref, wa_ref, wb_ref, k_ref, vt_ref):
    x = x_ref[0]
    a = jnp.dot((x + pea_ref[...]).astype(BF16), wa_ref[...], preferred_element_type=F32)
    b = jnp.dot((x + peb_ref[...]).astype(BF16), wb_ref[...], preferred_element_type=F32)
    kv = a + pltpu.roll(b, b.shape[0] - 1, axis=0)
    k_ref[0] = kv[:, :KV_W].astype(BF16)
    vt_ref[0] = kv[:, KV_W:].T.astype(BF16)


def _compress_weights(pe_k, wk, pe_v, wv):
    eye = jnp.eye(2 * KV_HEADS, dtype=F32)

    def half(lo):
        w = jnp.stack([wk[lo:lo + CMP_STRIDE], wk[lo:lo + CMP_STRIDE], wv[lo:lo + CMP_STRIDE], wv[lo:lo + CMP_STRIDE]], 1)
        big = jnp.einsum('lcde,cf->lcdfe', w, eye).reshape(CMP_STRIDE * 2 * KV_W, 2 * KV_W)
        pe = jnp.stack([pe_k[lo:lo + CMP_STRIDE], pe_k[lo:lo + CMP_STRIDE], pe_v[lo:lo + CMP_STRIDE], pe_v[lo:lo + CMP_STRIDE]], 1)
        return pe.reshape(1, CMP_STRIDE * 2 * KV_W), big.astype(BF16)

    pea, wa = half(0)
    peb, wb = half(CMP_STRIDE)
    return pea, peb, wa, wb


def _compress_call(x3, cw):
    B, nc, cw_in = x3.shape
    full = lambda a: pl.BlockSpec(a.shape, lambda b: (0, 0))
    return pl.pallas_call(
        _compress_kernel,
        grid=(B,),
        in_specs=[pl.BlockSpec((1, nc, cw_in), lambda b: (b, 0, 0))] + [full(a) for a in cw],
        out_specs=[pl.BlockSpec((1, nc, KV_W), lambda b: (b, 0, 0)), pl.BlockSpec((1, KV_W, nc), lambda b: (b, 0, 0))],
        out_shape=[jax.ShapeDtypeStruct((B, nc, KV_W), BF16), jax.ShapeDtypeStruct((B, KV_W, nc), BF16)],
        compiler_params=pltpu.CompilerParams(dimension_semantics=("parallel",), vmem_limit_bytes=VMEM_LIMIT),
        name="compress",
    )(x3, *cw)


SLC_TK = 256
SLC_C = SLC_TK + LANE
SLC_UNROLL = 2
WIN_KEYS = WINDOW + Q_BLOCK


def _dot_nt(a, b):
    return lax.dot_general(a, b, (((1,), (1,)), ((), ())), preferred_element_type=F32)


def _half_select(slab, want_half, half):
    return jnp.where(half == want_half, slab, pltpu.roll(slab, HEAD_DIM, axis=1))


def _nsa_prompt_kernel(q_ref, gate_ref, kc_ref, vct_ref, kvs_ref, vst_ref, kvw_ref, vwt_ref, bcmp_ref, bslc_ref,
                       bwin_ref, et_ref, ovt_ref, o_ref):
    i = pl.program_id(1)
    Q = Q_BLOCK
    R = GQA_GROUP
    G = KV_HEADS
    q0 = i * Q
    half = lax.broadcasted_iota(jnp.int32, (Q, LANE), 1) // HEAD_DIM
    qb = q_ref[...] * (HEAD_DIM ** -0.5)
    ovt = ovt_ref[...]
    ns = ovt.shape[0]
    jidx = lax.broadcasted_iota(jnp.int32, (ns, Q), 0)
    cur = (q0 + lax.broadcasted_iota(jnp.int32, (ns, Q), 1)) // SLC_BLOCK
    forced = (jidx == 0) | (jidx == cur) | (jidx == cur - 1)
    col_pos = q0 + lax.broadcasted_iota(jnp.int32, (1, R * Q), 1) % Q

    q_pads, lhss, o_cmps = [], [], []
    for g in range(G):
        slabs = []
        for r in range(R):
            h = g * R + r
            slab = qb[:, (h // 2) * LANE:(h // 2 + 1) * LANE]
            slabs.append(jnp.where(half == g, _half_select(slab, h % 2, half), 0.0))
        q_pad = jnp.concatenate(slabs, axis=0).astype(BF16)

        s = _dot_nt(kc_ref[0], q_pad) + bcmp_ref[g, 0]
        e = jnp.exp(s - jnp.max(s, axis=0, keepdims=True))
        p = e / jnp.sum(e, axis=0, keepdims=True)
        p = jnp.where(col_pos >= CMP_BLOCK - 1, p, 0.0)
        pb = p.astype(BF16)
        o_cmps.append(jnp.dot(vct_ref[0], pb, preferred_element_type=F32))

        imp = jnp.dot(ovt, pb[:, :Q], preferred_element_type=F32)
        for r in range(1, R):
            imp = imp + jnp.dot(ovt, pb[:, r * Q:(r + 1) * Q], preferred_element_type=F32)
        score = jnp.where(forced, FORCE, jnp.where(jidx <= cur, imp, NEG))
        rank = jnp.zeros((ns, Q), F32)
        for jp in range(ns):
            row = score[jp:jp + 1, :]
            rank = rank + jnp.where(jidx > jp, jnp.where(row >= score, 1.0, 0.0), jnp.where(row > score, 1.0, 0.0))
        selneg_t = jnp.where(rank < SLC_TOPK, 0.0, NEG)
        selneg_t = jnp.concatenate([selneg_t, jnp.zeros((LANE - ns, Q), F32)], axis=0)
        selneg = selneg_t.T.astype(BF16)
        q_pads.append(q_pad)
        lhss.append(jnp.concatenate([q_pad, jnp.concatenate([selneg] * R, axis=0)], axis=1))

    def flash_step(s, vt_tile, carry):
        m, l, acc = carry
        m_new = jnp.maximum(m, jnp.max(s, axis=0, keepdims=True))
        alpha = jnp.exp(m - m_new)
        pt = jnp.exp(s - m_new)
        l = alpha * l + jnp.sum(pt, axis=0, keepdims=True)
        acc = alpha * acc + jnp.dot(vt_tile, pt.astype(BF16), preferred_element_type=F32)
        return m_new, l, acc

    init = (jnp.full((1, R * Q), NEG, F32), jnp.zeros((1, R * Q), F32), jnp.zeros((LANE, R * Q), F32))

    def slc_tile(t, carry):
        k0 = pl.multiple_of(t * SLC_TK, SLC_TK)
        kt = kvs_ref[pl.ds(k0, SLC_TK), :].astype(BF16)
        keys = jnp.concatenate([kt, et_ref[pl.ds(k0, SLC_TK), :]], axis=1)
        c0 = pl.multiple_of(jnp.maximum(SLC_C - (q0 - k0), 0), LANE)
        ch = t * (SLC_TK // LANE)
        vt = jnp.concatenate([vst_ref[ch + c] for c in range(SLC_TK // LANE)], axis=1)
        return tuple(flash_step(_dot_nt(keys, lhss[g]) + bslc_ref[g, pl.ds(c0, SLC_TK), :], vt, carry[g])
                     for g in range(G))

    def slc_body(tp, carry):
        for u in range(SLC_UNROLL):
            carry = slc_tile(tp * SLC_UNROLL + u, carry)
        return carry

    n_tiles = q0 // SLC_TK + 1
    slc = lax.fori_loop(0, (n_tiles + SLC_UNROLL - 1) // SLC_UNROLL, slc_body, (init,) * G)

    k0 = pl.multiple_of(jnp.maximum(q0 - WINDOW, 0), LANE)
    kt = kvw_ref[pl.ds(k0, WIN_KEYS), :].astype(BF16)
    c0 = pl.multiple_of(WINDOW - (q0 - k0), LANE)
    ch = k0 // LANE
    vt = jnp.concatenate([vwt_ref[ch + c] for c in range(WIN_KEYS // LANE)], axis=1)
    o_wins = []
    for g in range(G):
        s = _dot_nt(kt, q_pads[g]) + bwin_ref[g, pl.ds(c0, WIN_KEYS), :]
        e = jnp.exp(s - jnp.max(s, axis=0, keepdims=True))
        o_wins.append(jnp.dot(vt, e.astype(BF16), preferred_element_type=F32) / jnp.sum(e, axis=0, keepdims=True))

    gates_t = gate_ref[...].T
    grow = lax.broadcasted_iota(jnp.int32, (GATE_PAD, Q), 0)
    for g in range(G):
        def gate_row(branch):
            rows = [jnp.sum(jnp.where(grow == branch * N_HEADS + g * R + r, gates_t, 0.0), axis=0, keepdims=True)
                    for r in range(R)]
            return jnp.concatenate(rows, axis=1)

        o = (gate_row(0) * o_cmps[g] + gate_row(1) * (slc[g][2] / slc[g][1])
             + gate_row(2) * o_wins[g])
        o = o[g * HEAD_DIM:(g + 1) * HEAD_DIM]
        for pair in range(R // 2):
            two = jnp.concatenate([o[:, (2 * pair) * Q:(2 * pair + 1) * Q],
                                   o[:, (2 * pair + 1) * Q:(2 * pair + 2) * Q]], axis=0)
            c = (g * R // 2 + pair) * LANE
            o_ref[:, c:c + LANE] = two.T


def _prompt_tables(tbl_flat, T):
    nqb = T // Q_BLOCK
    nc = T // CMP_STRIDE
    qi = jnp.arange(Q_BLOCK, dtype=jnp.int32)
    q_pos = jnp.arange(nqb, dtype=jnp.int32)[:, None, None] * Q_BLOCK + qi[None, :, None]
    c_end = jnp.arange(nc, dtype=jnp.int32) * CMP_STRIDE + (CMP_BLOCK - 1)
    big = 1 << 30
    bcmp = _bias_tables(tbl_flat, jnp.swapaxes(q_pos - c_end[None, None, :], 1, 2), big, True)

    def toeplitz(c_off, nrows, hi):
        dist = qi[None, None, :] + c_off - jnp.arange(nrows, dtype=jnp.int32)[None, :, None]
        return _bias_tables(tbl_flat, dist, hi, True)[:, 0]

    bslc = toeplitz(SLC_C, SLC_C + SLC_UNROLL * SLC_TK, big)
    bwin = toeplitz(WINDOW, WINDOW + WIN_KEYS, WINDOW)
    keys = jnp.arange(T, dtype=jnp.int32)
    et = (keys[:, None] // SLC_BLOCK == jnp.arange(LANE, dtype=jnp.int32)[None, :]).astype(BF16)
    c_start = jnp.arange(nc, dtype=jnp.int32) * CMP_STRIDE
    s_start = jnp.arange(T // SLC_BLOCK, dtype=jnp.int32) * SLC_BLOCK
    ovt = ((c_start[None, :] < s_start[:, None] + SLC_BLOCK)
           & (c_start[None, :] + CMP_BLOCK > s_start[:, None])).astype(BF16)
    return bcmp, bslc, bwin, et, ovt


def _nsa_prompt_call(q, gates, kc, vct, kvs, vst, kvw, vwt, tables, B, T):
    bcmp, bslc, bwin, et, ovt = tables
    nqb = T // Q_BLOCK
    nc = T // CMP_STRIDE
    R = GQA_GROUP
    seq = pl.BlockSpec((T, KV_W), lambda b, i: (b, 0))
    seq_t = pl.BlockSpec((T // LANE, KV_W, LANE), lambda b, i: (b, 0, 0))
    return pl.pallas_call(
        _nsa_prompt_kernel,
        grid=(B, nqb),
        in_specs=[
            pl.BlockSpec((Q_BLOCK, NSA_W), lambda b, i: (b * nqb + i, 0)),
            pl.BlockSpec((Q_BLOCK, GATE_PAD), lambda b, i: (b * nqb + i, 0)),
            pl.BlockSpec((1, nc, KV_W), lambda b, i: (b, 0, 0)),
            pl.BlockSpec((1, KV_W, nc), lambda b, i: (b, 0, 0)),
            seq, seq_t, seq, seq_t,
            pl.BlockSpec((KV_HEADS, 1, nc, R * Q_BLOCK), lambda b, i: (0, i, 0, 0)),
            pl.BlockSpec(bslc.shape, lambda b, i: (0, 0, 0)),
            pl.BlockSpec(bwin.shape, lambda b, i: (0, 0, 0)),
            pl.BlockSpec(et.shape, lambda b, i: (0, 0)),
            pl.BlockSpec(ovt.shape, lambda b, i: (0, 0)),
        ],
        out_specs=pl.BlockSpec((Q_BLOCK, NSA_W), lambda b, i: (b * nqb + i, 0)),
        out_shape=jax.ShapeDtypeStruct((B * T, NSA_W), F32),
        compiler_params=pltpu.CompilerParams(dimension_semantics=("parallel", "arbitrary"),
                                             vmem_limit_bytes=VMEM_LIMIT),
        name="nsa_prompt",
    )(q, gates, kc, vct, kvs, vst, kvw, vwt, bcmp, bslc, bwin, et, ovt)


SUB = 8
PAGES_PER_STEP = 64


def _compress_pages_kernel(xk_ref, xv_ref, pea_ref, peb_ref, wa_ref, wb_ref, o_ref):
    pb, page, hw = xk_ref.shape
    w = 2 * hw
    ch = page // CMP_STRIDE
    a = jnp.zeros((pb * ch, w), F32)
    b = jnp.zeros((pb * ch, w), F32)
    for t in range(CMP_STRIDE):
        x = jnp.concatenate([ref[:, pl.ds(t, ch, stride=CMP_STRIDE), :].reshape(pb * ch, hw)
                             for ref in (xk_ref, xv_ref)], axis=1)
        cols = slice(t * w, (t + 1) * w)
        a = a + jnp.dot((x + pea_ref[:, cols]).astype(BF16), wa_ref[cols, :], preferred_element_type=F32)
        b = b + jnp.dot((x + peb_ref[:, cols]).astype(BF16), wb_ref[cols, :], preferred_element_type=F32)
    o_ref[...] = jnp.concatenate([a, b], axis=1).reshape(pb, ch, 2 * w)


def _compress_pages(pages, layer, n_phys, cw):
    _, page, w = pages.shape
    ch = page // CMP_STRIDE
    steps = n_phys // PAGES_PER_STEP
    full = lambda a: pl.BlockSpec(a.shape, lambda i: (0, 0))
    return pl.pallas_call(
        _compress_pages_kernel,
        grid=(steps,),
        in_specs=[pl.BlockSpec((PAGES_PER_STEP, page, w // 2), lambda i: (layer * steps + i, 0, 0)),
                  pl.BlockSpec((PAGES_PER_STEP, page, w // 2), lambda i: (layer * steps + i, 0, 1))]
                 + [full(a) for a in cw],
        out_specs=pl.BlockSpec((PAGES_PER_STEP, ch, 4 * KV_W), lambda i: (i, 0, 0)),
        out_shape=jax.ShapeDtypeStruct((n_phys, ch, 4 * KV_W), F32),
        compiler_params=pltpu.CompilerParams(dimension_semantics=("parallel",), vmem_limit_bytes=VMEM_LIMIT),
        name="compress_pages",
    )(pages, pages, *cw)


def _nsa_sample_kernel(pt_ref, q_ref, gate_ref, ks_new_ref, kw_new_ref, *rest, n_pages, past, dec_seq):
    ab_refs = rest[:n_pages]
    slc_refs = rest[n_pages:2 * n_pages]
    win_ref, bcmp_ref, bslc_ref, bwin_ref, et_ref, ov_ref, o_ref, win_out_ref = rest[2 * n_pages:]
    R, G = GQA_GROUP, KV_HEADS
    rows = R * G * SUB
    half = lax.broadcasted_iota(jnp.int32, (SUB, LANE), 1) // HEAD_DIM

    qb = q_ref[0] * (HEAD_DIM ** -0.5)
    slabs = []
    for r in range(R):
        for g in range(G):
            h = g * R + r
            slab = qb[:, (h // 2) * LANE:(h // 2 + 1) * LANE]
            slabs.append(jnp.where(half == g, _half_select(slab, h % 2, half), 0.0))
    q_pad = jnp.concatenate(slabs, axis=0).astype(BF16)

    ab = jnp.concatenate([r_[0] for r_ in ab_refs], axis=0)
    nc = ab.shape[0]
    kcvc = ab[:, :2 * KV_W] + pltpu.roll(ab[:, 2 * KV_W:], nc - 1, axis=0)
    kc = kcvc[:, :KV_W].astype(BF16)
    vc = kcvc[:, KV_W:].astype(BF16)
    s = _dot_nt(q_pad, kc) + bcmp_ref[...]
    e = jnp.exp(s - jnp.max(s, axis=-1, keepdims=True))
    p = e / jnp.sum(e, axis=-1, keepdims=True)
    row_pos = past + lax.broadcasted_iota(jnp.int32, (rows, 1), 0) % SUB
    p = jnp.where(row_pos >= CMP_BLOCK - 1, p, 0.0)
    pb = p.astype(BF16)
    o_cmp = jnp.dot(pb, vc, preferred_element_type=F32)

    ov = ov_ref[...]
    imp = jnp.dot(pb[:G * SUB], ov, preferred_element_type=F32)
    for r in range(1, R):
        imp = imp + jnp.dot(pb[r * G * SUB:(r + 1) * G * SUB], ov, preferred_element_type=F32)
    jidx = lax.broadcasted_iota(jnp.int32, (G * SUB, LANE), 1)
    cur = (past + lax.broadcasted_iota(jnp.int32, (G * SUB, LANE), 0) % SUB) // SLC_BLOCK
    forced = (jidx == 0) | (jidx == cur) | (jidx == cur - 1)
    score = jnp.where(forced, FORCE, jnp.where(jidx <= cur, imp, NEG))
    n_blocks = (past + dec_seq + SLC_BLOCK - 1) // SLC_BLOCK
    rank = jnp.zeros((G * SUB, LANE), F32)
    for jp in range(n_blocks):
        col = score[:, jp:jp + 1]
        rank = rank + jnp.where(jidx > jp, jnp.where(col >= score, 1.0, 0.0), jnp.where(col > score, 1.0, 0.0))
    selneg = jnp.where(rank < min(SLC_TOPK, n_blocks), 0.0, NEG).astype(BF16)
    lhs = jnp.concatenate([q_pad, jnp.concatenate([selneg] * R, axis=0)], axis=1)

    def new_rows(ref):
        x = ref[0]
        pad = jnp.zeros((LANE - SUB, KV_W), F32)
        return (jnp.concatenate([x[:, :KV_W], pad], axis=0).astype(BF16),
                jnp.concatenate([x[:, KV_W:], pad], axis=0).astype(BF16))

    def attend(lhs_, k_tiles, v_tiles, bias, ext):
        parts = []
        for t, kt in enumerate(k_tiles):
            rhs = jnp.concatenate([kt, et_ref[t * LANE:(t + 1) * LANE, :]], axis=1) if ext else kt
            parts.append(_dot_nt(lhs_, rhs))
        s_ = jnp.concatenate(parts, axis=1) + bias
        e_ = jnp.exp(s_ - jnp.max(s_, axis=-1, keepdims=True))
        p_ = (e_ / jnp.sum(e_, axis=-1, keepdims=True)).astype(BF16)
        o_ = jnp.dot(p_[:, :LANE], v_tiles[0], preferred_element_type=F32)
        for t in range(1, len(v_tiles)):
            o_ = o_ + jnp.dot(p_[:, t * LANE:(t + 1) * LANE], v_tiles[t], preferred_element_type=F32)
        return o_

    k_new, v_new = new_rows(ks_new_ref)
    k_tiles = [r_[0, :, :KV_W].astype(BF16) for r_ in slc_refs] + [k_new]
    v_tiles = [r_[0, :, KV_W:].astype(BF16) for r_ in slc_refs] + [v_new]
    o_slc = attend(lhs, k_tiles, v_tiles, bslc_ref[...], True)

    k_new, v_new = new_rows(kw_new_ref)
    wb = win_ref.shape[1]
    k_tiles = [win_ref[0, t * LANE:(t + 1) * LANE, :KV_W].astype(BF16) for t in range(wb // LANE)] + [k_new]
    v_tiles = [win_ref[0, t * LANE:(t + 1) * LANE, KV_W:].astype(BF16) for t in range(wb // LANE)] + [v_new]
    o_win = attend(q_pad, k_tiles, v_tiles, bwin_ref[...], False)

    gates = gate_ref[0]
    glane = lax.broadcasted_iota(jnp.int32, (SUB, LANE), 1)

    def gate_col(branch):
        cols = [jnp.sum(jnp.where(glane == branch * N_HEADS + g * R + r, gates, 0.0), axis=-1, keepdims=True)
                for r in range(R) for g in range(G)]
        return jnp.concatenate(cols, axis=0)

    o = gate_col(0) * o_cmp + gate_col(1) * o_slc + gate_col(2) * o_win
    for g in range(G):
        for pair in range(R // 2):
            r0 = ((2 * pair) * G + g) * SUB
            r1 = ((2 * pair + 1) * G + g) * SUB
            lo = _half_select(o[r0:r0 + SUB], g, half)
            hi = _half_select(o[r1:r1 + SUB], g, half)
            c = (g * R // 2 + pair) * LANE
            o_ref[0, :, c:c + LANE] = jnp.where(half == 0, lo, hi)

    win = win_ref[0]
    shifted = pltpu.roll(win, wb - dec_seq, axis=0)
    win_out_ref[0, :wb - SUB, :] = shifted[:wb - SUB]
    srow = lax.broadcasted_iota(jnp.int32, (SUB, 2 * KV_W), 0)
    win_out_ref[0, wb - SUB:, :] = jnp.where(srow >= SUB - dec_seq, pltpu.roll(kw_new_ref[0], SUB - dec_seq, axis=0),
                                             shifted[wb - SUB:])


def _sample_tables(tbl_flat, past, dec_seq, wb):
    q_pos = past + jnp.arange(SUB, dtype=jnp.int32)
    nc = past // CMP_STRIDE
    c_end = jnp.arange(nc, dtype=jnp.int32) * CMP_STRIDE + (CMP_BLOCK - 1)
    slc_keys = jnp.arange(past + LANE, dtype=jnp.int32)
    win_pos = jnp.concatenate([past - wb + jnp.arange(wb, dtype=jnp.int32), past + jnp.arange(LANE, dtype=jnp.int32)])
    big = 1 << 30

    def table(key_pos, hi):
        t = _bias_tables(tbl_flat, (q_pos[:, None] - key_pos[None, :])[None], hi)
        c = t.shape[-1]
        return t.reshape(KV_HEADS, GQA_GROUP, SUB, c).transpose(1, 0, 2, 3).reshape(GQA_GROUP * KV_HEADS * SUB, c)

    bcmp = table(c_end, big)
    bslc = table(slc_keys, big)
    bwin = table(win_pos, WINDOW)
    def kill_pad(b):
        lane = jnp.arange(b.shape[1]) - (b.shape[1] - LANE)
        return jnp.where((lane >= dec_seq)[None, :], NEG, b)
    bslc, bwin = kill_pad(bslc), kill_pad(bwin)
    et = (slc_keys[:, None] // SLC_BLOCK == jnp.arange(LANE, dtype=jnp.int32)[None, :]).astype(BF16)
    c_start = jnp.arange(nc, dtype=jnp.int32) * CMP_STRIDE
    s_start = jnp.arange(LANE, dtype=jnp.int32) * SLC_BLOCK
    ov = ((c_start[:, None] < s_start[None, :] + SLC_BLOCK)
          & (c_start[:, None] + CMP_BLOCK > s_start[None, :])).astype(BF16)
    return bcmp, bslc, bwin, et, ov


def _nsa_sample_call(page_table_flat, q, gates, ks_new, kw_new, ab_pages, slc_pages, win_buf, tables, layer, n_phys,
                     dec_seq):
    bcmp, bslc, bwin, et, ov = tables
    B = q.shape[0]
    n_pages = page_table_flat.shape[0] // B
    page = slc_pages.shape[1]
    past = n_pages * page
    wb = win_buf.shape[1]
    row3 = lambda a: pl.BlockSpec((1,) + a.shape[1:], lambda b, pt: (b, 0, 0))
    full = lambda a: pl.BlockSpec(a.shape, lambda b, pt: (0, 0))
    ab_specs = [pl.BlockSpec((1,) + ab_pages.shape[1:], lambda b, pt, p=p: (pt[b * n_pages + p], 0, 0))
                for p in range(n_pages)]
    slc_specs = [pl.BlockSpec((1,) + slc_pages.shape[1:], lambda b, pt, p=p: (layer * n_phys + pt[b * n_pages + p], 0, 0))
                 for p in range(n_pages)]
    return pl.pallas_call(
        functools.partial(_nsa_sample_kernel, n_pages=n_pages, past=past, dec_seq=dec_seq),
        grid_spec=pltpu.PrefetchScalarGridSpec(
            num_scalar_prefetch=1,
            grid=(B,),
            in_specs=[row3(q), row3(gates), row3(ks_new), row3(kw_new)] + ab_specs + slc_specs
                     + [pl.BlockSpec((1, wb, 2 * KV_W), lambda b, pt: (layer * B + b, 0, 0)),
                        full(bcmp), full(bslc), full(bwin), full(et), full(ov)],
            out_specs=[pl.BlockSpec((1, SUB, NSA_W), lambda b, pt: (b, 0, 0)),
                       pl.BlockSpec((1, wb, 2 * KV_W), lambda b, pt: (b, 0, 0))],
        ),
        out_shape=[jax.ShapeDtypeStruct((B, SUB, NSA_W), F32), jax.ShapeDtypeStruct((B, wb, 2 * KV_W), F32)],
        compiler_params=pltpu.CompilerParams(dimension_semantics=("parallel",), vmem_limit_bytes=VMEM_LIMIT),
        name="nsa_sample",
    )(page_table_flat, q, gates, ks_new, kw_new, *([ab_pages] * n_pages), *([slc_pages] * n_pages), win_buf,
      bcmp, bslc, bwin, et, ov)


CONV_HALO = 32
POOL_HALO = 16


def _mix_kernel(glu_ref, glu_halo_ref, glu_hist_ref, u_ref, u_halo_ref, u_hist_ref, dw_ref, cb_ref, lng_ref, lnb_ref,
                wp_ref, ps_ref, conv_ref, pool_ref, xs_ref, us_ref, *, pos0):
    j = pl.program_id(1)
    tq = glu_ref.shape[0]

    @pl.when(j == 0)
    def _():
        xs_ref[:CONV_HALO, :] = glu_hist_ref[0]
        us_ref[:POOL_HALO, :] = u_hist_ref[0]

    @pl.when(j > 0)
    def _():
        xs_ref[:CONV_HALO, :] = glu_halo_ref[...]
        us_ref[:POOL_HALO, :] = u_halo_ref[...]

    xs_ref[CONV_HALO:, :] = glu_ref[...]
    us_ref[POOL_HALO:, :] = u_ref[...]

    base = CONV_HALO - CONV_BUF
    y = xs_ref[base:base + tq, :] * dw_ref[0:1, :]
    for k in range(1, CONV_WIDTH):
        y = y + xs_ref[base + k:base + k + tq, :] * dw_ref[k:k + 1, :]
    y = y + cb_ref[...]
    mu = jnp.mean(y, axis=-1, keepdims=True)
    yc = y - mu
    var = jnp.mean(yc * yc, axis=-1, keepdims=True)
    conv_ref[...] = jax.nn.silu(yc * lax.rsqrt(var + EPS) * lng_ref[...] + lnb_ref[...])

    u = u_ref[...]
    pos = pos0 + j * tq + lax.broadcasted_iota(jnp.int32, (tq, 1), 0)
    grp = lax.broadcasted_iota(jnp.int32, (tq, POOL_W), 1) // POOL_GROUP_W
    run = u
    d = jnp.zeros_like(u)
    done = 1
    for gi, w in enumerate(POOL_WINDOWS):
        for back in range(done, w):
            run = run + us_ref[POOL_HALO - back:POOL_HALO - back + tq, :]
        done = w
        cnt = jnp.minimum(w, pos + 1).astype(F32)
        d = jnp.where(grp == gi, run / cnt - u, d)
    yp = jnp.dot(d.astype(BF16), wp_ref[...], preferred_element_type=F32)
    pool_ref[...] = yp * ps_ref[...]


def _mix(glu, u, glu_hist, u_hist, dw, cb, lng, lnb, wp_bd, ps, B, T, tq, pos0):
    nt = T // tq
    row = pl.BlockSpec((tq, CONV_W), lambda b, j: (b * nt + j, 0))
    full = lambda a: pl.BlockSpec(a.shape, lambda b, j: (0, 0))

    def halo(h):
        return pl.BlockSpec((h, CONV_W), lambda b, j: (jnp.maximum(b * (T // h) + j * (tq // h) - 1, 0), 0))

    hist = lambda h: pl.BlockSpec((1, h, CONV_W), lambda b, j: (b, 0, 0))
    if T < CONV_HALO:
        glu_halo_arr, u_halo_arr = glu_hist.reshape(-1, CONV_W), u_hist.reshape(-1, POOL_W)
        halo_c = pl.BlockSpec((CONV_HALO, CONV_W), lambda b, j: (0, 0))
        halo_p = pl.BlockSpec((POOL_HALO, POOL_W), lambda b, j: (0, 0))
    else:
        glu_halo_arr, u_halo_arr = glu, u
        halo_c, halo_p = halo(CONV_HALO), halo(POOL_HALO)
    return pl.pallas_call(
        functools.partial(_mix_kernel, pos0=pos0),
        grid=(B, nt),
        in_specs=[row, halo_c, hist(CONV_HALO), row, halo_p, hist(POOL_HALO),
                  full(dw), full(cb), full(lng), full(lnb), full(wp_bd), full(ps)],
        out_specs=[row, row],
        out_shape=[jax.ShapeDtypeStruct((B * T, CONV_W), F32), jax.ShapeDtypeStruct((B * T, POOL_W), F32)],
        scratch_shapes=[pltpu.VMEM((CONV_HALO + tq, CONV_W), F32), pltpu.VMEM((POOL_HALO + tq, POOL_W), F32)],
        compiler_params=pltpu.CompilerParams(dimension_semantics=("parallel", "parallel"),
                                             vmem_limit_bytes=VMEM_LIMIT),
        name="conv_pool",
    )(glu, glu_halo_arr, glu_hist, u, u_halo_arr, u_hist, dw, cb, lng, lnb, wp_bd, ps)


def _layer(x2d, B, T, l, P, glu_hist, u_hist, pos0, nsa_fn, final, emit_vt):
    n = B * T
    glu, u, q, kvc, kvs, kvw, gates, *vts = _inproj(x2d, P['norm1'][l], P['w_in'][l], min(n, 512), emit_vt)
    conv_o, pool_o = _mix(glu, u, glu_hist, u_hist, P['conv_dw'][l], P['conv_b'][l], P['conv_ln_g'][l],
                          P['conv_ln_b'][l], P['pool_w'][l], P['pool_scale'][l], B, T, min(T, 512), pos0)
    nsa_o, extra = nsa_fn(q, gates, kvc, kvs, kvw, *vts)
    y = _ffn(x2d, conv_o, pool_o, nsa_o, P['w_out'][l], P['norm2'][l], P['w_gu'][l], P['w_down'][l],
             P['final_norm'], min(n, 1024), 256, final)
    return y, glu, u, kvc, kvs, kvw, extra


def kernel(x_prompt, x_sample, cache_cmp_kv, cache_slc_kv, cache_win_kv, state_conv, state_pool, page_table, rel_bias, norm1, w_in, conv_dw, conv_b, conv_ln_g, conv_ln_b, pool_w, pool_scale, cmp_pe_k, cmp_wk, cmp_pe_v, cmp_wv, w_out, norm2, w_gu, w_down, final_norm):
    depth = w_in.shape[0]
    Bp, T, _ = x_prompt.shape
    Bs, S, _ = x_sample.shape
    n_phys, page = cache_cmp_kv.shape[1:3]
    n_pages = page_table.shape[1]
    past = n_pages * page
    wb = cache_win_kv.shape[2]
    tbl_flat = rel_bias.reshape(NUM_BUCKETS, KV_HEADS, GQA_GROUP).transpose(1, 2, 0).reshape(-1)
    eye = jnp.eye(len(POOL_WINDOWS), dtype=F32)
    row = lambda a: a[:, None, :]
    P = dict(
        norm1=row(norm1), norm2=row(norm2), final_norm=final_norm[None, :],
        w_in=jnp.pad(w_in, ((0, 0), (0, 0), (0, IN_W_PAD - IN_W))).astype(BF16),
        w_out=w_out.astype(BF16), w_gu=w_gu.astype(BF16), w_down=w_down.astype(BF16),
        conv_dw=conv_dw, conv_b=row(conv_b), conv_ln_g=row(conv_ln_g), conv_ln_b=row(conv_ln_b),
        pool_w=jnp.einsum('lgce,gh->lgche', pool_w, eye).reshape(depth, POOL_W, POOL_W).astype(BF16),
        pool_scale=row(pool_scale))
    ptables = _prompt_tables(tbl_flat, T)
    stables = _sample_tables(tbl_flat, past, S, wb)
    cmp_pages = cache_cmp_kv.reshape(depth * n_phys, page, 2 * KV_W)
    slc_pages = cache_slc_kv.reshape(depth * n_phys, page, 2 * KV_W)
    win_bufs = cache_win_kv.reshape(depth * Bs, wb, 2 * KV_W)
    pt_flat = page_table.reshape(-1)

    xp = x_prompt.reshape(Bp * T, D_MODEL)
    xs = jnp.pad(x_sample, ((0, 0), (0, SUB - S), (0, 0))).reshape(Bs * SUB, D_MODEL)
    zc = jnp.zeros((Bp, CONV_HALO, CONV_W), F32)
    zp = jnp.zeros((Bp, POOL_HALO, POOL_W), F32)
    kv5 = lambda a, b, t: a.reshape(b, t, 2, KV_HEADS, HEAD_DIM)
    outs_p, outs_s = [], []
    for l in range(depth):
        final = l == depth - 1
        cw = _compress_weights(cmp_pe_k[l], cmp_wk[l], cmp_pe_v[l], cmp_wv[l])

        def nsa_p(q, gates, kvc, kvs, kvw, vst, vwt):
            kc, vct = _compress_call(kvc.reshape(Bp, T // CMP_STRIDE, CMP_STRIDE * 2 * KV_W), cw)
            return _nsa_prompt_call(q, gates, kc, vct, kvs, vst, kvw, vwt, ptables, Bp, T), None

        xp, glu, u, kvc, kvs, kvw, _ = _layer(xp, Bp, T, l, P, zc, zp, 0, nsa_p, final, True)
        outs_p.append((kv5(kvc, Bp, T), kv5(kvs, Bp, T), kv5(kvw, Bp, T)[:, T - min(WINDOW, T):],
                       glu.reshape(Bp, T, CONV_W)[:, T - CONV_BUF:], u.reshape(Bp, T, POOL_W)[:, T - POOL_BUF:]))

        def nsa_s(q, gates, kvc, kvs, kvw):
            ab = _compress_pages(cmp_pages, l, n_phys, cw)
            r3 = lambda a: a.reshape(Bs, SUB, a.shape[-1])
            o, new_win = _nsa_sample_call(pt_flat, r3(q), r3(gates), r3(kvs), r3(kvw), ab, slc_pages, win_bufs,
                                          stables, l, n_phys, S)
            return o.reshape(Bs * SUB, NSA_W), new_win

        glu_hist = jnp.pad(state_conv[l], ((0, 0), (CONV_HALO - CONV_BUF, 0), (0, 0)))
        u_hist = jnp.pad(state_pool[l], ((0, 0), (POOL_HALO - POOL_BUF, 0), (0, 0)))
        xs, glu, u, kvc, kvs, kvw, new_win = _layer(xs, Bs, SUB, l, P, glu_hist, u_hist, past, nsa_s, final, False)
        new_conv = jnp.concatenate([state_conv[l], glu.reshape(Bs, SUB, CONV_W)[:, :S]], axis=1)[:, -CONV_BUF:]
        new_pool = jnp.concatenate([state_pool[l], u.reshape(Bs, SUB, POOL_W)[:, :S]], axis=1)[:, -POOL_BUF:]
        outs_s.append((kv5(kvc, Bs, SUB)[:, :S], kv5(kvs, Bs, SUB)[:, :S], kv5(new_win, Bs, wb), new_conv, new_pool))
    st = lambda outs, i: jnp.stack([o[i] for o in outs])
    return (xp.reshape(Bp, T, D_MODEL), xs.reshape(Bs, SUB, D_MODEL)[:, :S],
            st(outs_p, 0), st(outs_p, 1), st(outs_p, 2), st(outs_p, 3), st(outs_p, 4),
            st(outs_s, 0), st(outs_s, 1), st(outs_s, 2), st(outs_s, 3), st(outs_s, 4))
```

```python
import functools
import math

import jax
import jax.numpy as jnp
from jax import lax
from jax.experimental import pallas as pl
from jax.experimental.pallas import tpu as pltpu

D_MODEL = 1024
CONV_W = 256
POOL_W = 256
N_HEADS = 8
HEAD_DIM = 64
NSA_W = N_HEADS * HEAD_DIM
KV_HEADS = 2
GQA_GROUP = N_HEADS // KV_HEADS
KV_W = KV_HEADS * HEAD_DIM
CONV_WIDTH = 31
CONV_BUF = CONV_WIDTH - 1
POOL_WINDOWS = (2, 4, 8, 16)
POOL_GROUP_W = POOL_W // len(POOL_WINDOWS)
POOL_BUF = max(POOL_WINDOWS) - 1
CMP_STRIDE = 16
CMP_BLOCK = 2 * CMP_STRIDE
SLC_BLOCK = 64
SLC_TOPK = 16
WINDOW = 512
Q_BLOCK = 128
NUM_BUCKETS = 32
MAX_DISTANCE = 128
D_FF = 2816
EPS = 1e-6
NEG = -1e30
FORCE = 1e4

OFF_POOL = 2 * CONV_W
OFF_Q = OFF_POOL + POOL_W
OFF_KV = OFF_Q + NSA_W
OFF_GATE = OFF_KV + 6 * KV_W
IN_W = OFF_GATE + 3 * N_HEADS
LANE = 128
IN_W_PAD = -(-IN_W // LANE) * LANE
GATE_PAD = IN_W_PAD - OFF_GATE

VMEM_LIMIT = 56 * 1024 * 1024

F32 = jnp.float32
BF16 = jnp.bfloat16


def _rms(x, g):
    return x * lax.rsqrt(jnp.mean(x * x, axis=-1, keepdims=True) + EPS) * g


def _inproj_kernel(x_ref, g_ref, w_ref, glu_ref, u_ref, q_ref, kvc_ref, kvs_ref, kvw_ref, gate_ref, *vt_refs):
    h = _rms(x_ref[...], g_ref[...]).astype(BF16)
    z = jnp.dot(h, w_ref[...], preferred_element_type=F32)
    glu_ref[...] = z[:, :CONV_W] * jax.nn.sigmoid(z[:, CONV_W:OFF_POOL])
    u_ref[...] = z[:, OFF_POOL:OFF_Q]
    q_ref[...] = z[:, OFF_Q:OFF_KV]
    kvc_ref[...] = z[:, OFF_KV:OFF_KV + 2 * KV_W]
    kvs_ref[...] = z[:, OFF_KV + 2 * KV_W:OFF_KV + 4 * KV_W]
    kvw_ref[...] = z[:, OFF_KV + 4 * KV_W:OFF_GATE]
    gate_ref[...] = jax.nn.sigmoid(z[:, OFF_GATE:])
    for vt_ref, off in zip(vt_refs, (OFF_KV + 3 * KV_W, OFF_KV + 5 * KV_W)):
        for c in range(vt_ref.shape[0]):
            vt_ref[c] = z[c * LANE:(c + 1) * LANE, off:off + KV_W].T.astype(BF16)


def _inproj(x2d, g, w_pad, tm, emit_vt):
    n = x2d.shape[0]
    row = lambda w: pl.BlockSpec((tm, w), lambda i: (i, 0))
    full = lambda a: pl.BlockSpec(a.shape, lambda i: (0, 0))
    widths = (CONV_W, POOL_W, NSA_W, 2 * KV_W, 2 * KV_W, 2 * KV_W, GATE_PAD)
    out_specs = [row(w) for w in widths]
    out_shape = [jax.ShapeDtypeStruct((n, w), F32) for w in widths]
    if emit_vt:
        out_specs += [pl.BlockSpec((tm // LANE, KV_W, LANE), lambda i: (i, 0, 0))] * 2
        out_shape += [jax.ShapeDtypeStruct((n // LANE, KV_W, LANE), BF16)] * 2
    return pl.pallas_call(
        _inproj_kernel,
        grid=(n // tm,),
        in_specs=[row(D_MODEL), full(g), full(w_pad)],
        out_specs=out_specs,
        out_shape=out_shape,
        compiler_params=pltpu.CompilerParams(dimension_semantics=("parallel",), vmem_limit_bytes=VMEM_LIMIT),
        name="inproj",
    )(x2d, g, w_pad)


def _ffn_kernel(x_ref, conv_ref, pool_ref, nsa_ref, wo_ref, g2_ref, wg_ref, wu_ref, wd_ref, gf_ref,
                y_ref, x1_ref, h2_ref, acc_ref, *, final_norm):
    j = pl.program_id(1)

    @pl.when(j == 0)
    def _():
        mix = jnp.concatenate([conv_ref[...], pool_ref[...], nsa_ref[...]], axis=-1).astype(BF16)
        x1 = x_ref[...] + jnp.dot(mix, wo_ref[...], preferred_element_type=F32)
        x1_ref[...] = x1
        h2_ref[...] = _rms(x1, g2_ref[...]).astype(BF16)
        acc_ref[...] = jnp.zeros_like(acc_ref)

    h2 = h2_ref[...]
    gate = jnp.dot(h2, wg_ref[...], preferred_element_type=F32)
    up = jnp.dot(h2, wu_ref[...], preferred_element_type=F32)
    act = (jax.nn.silu(gate) * up).astype(BF16)
    acc_ref[...] += jnp.dot(act, wd_ref[...], preferred_element_type=F32)

    @pl.when(j == pl.num_programs(1) - 1)
    def _():
        y = x1_ref[...] + acc_ref[...]
        if final_norm:
            y = _rms(y, gf_ref[...])
        y_ref[...] = y


def _ffn(x2d, conv_o, pool_o, nsa_o, w_out, g2, w_gu, w_down, gf, tm, tf, final_norm):
    n = x2d.shape[0]
    nf = D_FF // tf
    row = lambda w: pl.BlockSpec((tm, w), lambda i, j: (i, 0))
    full = lambda a: pl.BlockSpec(a.shape, lambda i, j: (0, 0))
    return pl.pallas_call(
        functools.partial(_ffn_kernel, final_norm=final_norm),
        grid=(n // tm, nf),
        in_specs=[row(D_MODEL), row(CONV_W), row(POOL_W), row(NSA_W), full(w_out), full(g2),
                  pl.BlockSpec((D_MODEL, tf), lambda i, j: (0, j)),
                  pl.BlockSpec((D_MODEL, tf), lambda i, j: (0, j + nf)),
                  pl.BlockSpec((tf, D_MODEL), lambda i, j: (j, 0)),
                  full(gf)],
        out_specs=row(D_MODEL),
        out_shape=jax.ShapeDtypeStruct((n, D_MODEL), F32),
        scratch_shapes=[pltpu.VMEM((tm, D_MODEL), F32), pltpu.VMEM((tm, D_MODEL), BF16),
                        pltpu.VMEM((tm, D_MODEL), F32)],
        compiler_params=pltpu.CompilerParams(dimension_semantics=("parallel", "arbitrary"),
                                             vmem_limit_bytes=VMEM_LIMIT),
        name="ffn",
    )(x2d, conv_o, pool_o, nsa_o, w_out, g2, w_gu, w_gu, w_down, gf)


def _bias_kernel(tbl_ref, dist_ref, out_ref, *, hi, heads_on_lanes):
    g = pl.program_id(0)
    d = dist_ref[0]
    rows, cols = d.shape
    n = jnp.maximum(d, 0)
    max_exact = NUM_BUCKETS // 2
    nf = jnp.maximum(n, 1).astype(F32)
    large = max_exact + (jnp.log(nf / max_exact) / math.log(MAX_DISTANCE / max_exact)
                         * (NUM_BUCKETS - max_exact)).astype(jnp.int32)
    large = jnp.minimum(large, NUM_BUCKETS - 1)
    bucket = jnp.where(n < max_exact, n, large)
    valid = (d >= 0) & (d < hi)
    for r in range(GQA_GROUP):
        acc = jnp.zeros(d.shape, F32)
        for b in range(NUM_BUCKETS):
            acc = jnp.where(bucket == b, tbl_ref[(g * GQA_GROUP + r) * NUM_BUCKETS + b], acc)
        val = jnp.where(valid, acc, NEG)
        if heads_on_lanes:
            out_ref[0, 0, :, r * cols:(r + 1) * cols] = val
        else:
            out_ref[0, 0, r * rows:(r + 1) * rows, :] = val


def _bias_tables(tbl_flat, dist, hi, heads_on_lanes=False):
    nblk, R, C = dist.shape
    oshape = (R, GQA_GROUP * C) if heads_on_lanes else (GQA_GROUP * R, C)
    return pl.pallas_call(
        functools.partial(_bias_kernel, hi=hi, heads_on_lanes=heads_on_lanes),
        grid_spec=pltpu.PrefetchScalarGridSpec(
            num_scalar_prefetch=1,
            grid=(KV_HEADS, nblk),
            in_specs=[pl.BlockSpec((1, R, C), lambda g, i, tbl: (i, 0, 0))],
            out_specs=pl.BlockSpec((1, 1) + oshape, lambda g, i, tbl: (g, i, 0, 0)),
        ),
        out_shape=jax.ShapeDtypeStruct((KV_HEADS, nblk) + oshape, F32),
        compiler_params=pltpu.CompilerParams(dimension_semantics=("parallel", "parallel")),
        name="bias_tables",
    )(tbl_flat, dist)


def _compress_kernel(x_ref, pea_ref, peb_ref, wa_ref, wb_ref, k_ref, vt_ref):
    x = x_ref[0]
    a = jnp.dot((x + pea_ref[...]).astype(BF16), wa_ref[...], preferred_element_type=F32)
    b = jnp.dot((x + peb_ref[...]).astype(BF16), wb_ref[...], preferred_element_type=F32)
    kv = a + pltpu.roll(b, b.shape[0] - 1, axis=0)
    k_ref[0] = kv[:, :KV_W].astype(BF16)
    vt_ref[0] = kv[:, KV_W:].T.astype(BF16)


def _compress_weights(pe_k, wk, pe_v, wv):
    eye = jnp.eye(2 * KV_HEADS, dtype=F32)

    def half(lo):
        w = jnp.stack([wk[lo:lo + CMP_STRIDE], wk[lo:lo + CMP_STRIDE], wv[lo:lo + CMP_STRIDE], wv[lo:lo + CMP_STRIDE]], 1)
        big = jnp.einsum('lcde,cf->lcdfe', w, eye).reshape(CMP_STRIDE * 2 * KV_W, 2 * KV_W)
        pe = jnp.stack([pe_k[lo:lo + CMP_STRIDE], pe_k[lo:lo + CMP_STRIDE], pe_v[lo:lo + CMP_STRIDE], pe_v[lo:lo + CMP_STRIDE]], 1)
        return pe.reshape(1, CMP_STRIDE * 2 * KV_W), big.astype(BF16)

    pea, wa = half(0)
    peb, wb = half(CMP_STRIDE)
    return pea, peb, wa, wb


def _compress_call(x3, cw):
    B, nc, cw_in = x3.shape
    full = lambda a: pl.BlockSpec(a.shape, lambda b: (0, 0))
    return pl.pallas_call(
        _compress_kernel,
        grid=(B,),
        in_specs=[pl.BlockSpec((1, nc, cw_in), lambda b: (b, 0, 0))] + [full(a) for a in cw],
        out_specs=[pl.BlockSpec((1, nc, KV_W), lambda b: (b, 0, 0)), pl.BlockSpec((1, KV_W, nc), lambda b: (b, 0, 0))],
        out_shape=[jax.ShapeDtypeStruct((B, nc, KV_W), BF16), jax.ShapeDtypeStruct((B, KV_W, nc), BF16)],
        compiler_params=pltpu.CompilerParams(dimension_semantics=("parallel",), vmem_limit_bytes=VMEM_LIMIT),
        name="compress",
    )(x3, *cw)


SLC_TK = 512
SLC_C = SLC_TK + LANE
SLC_UNROLL = 1
WIN_KEYS = WINDOW + Q_BLOCK


def _dot_nt(a, b):
    return lax.dot_general(a, b, (((1,), (1,)), ((), ())), preferred_element_type=F32)


def _half_select(slab, want_half, half):
    return jnp.where(half == want_half, slab, pltpu.roll(slab, HEAD_DIM, axis=1))


def _nsa_prompt_kernel(q_ref, gate_ref, kc_ref, vct_ref, kvs_ref, vst_ref, kvw_ref, vwt_ref, bcmp_ref, bslc_ref,
                       bwin_ref, et_ref, ovt_ref, o_ref):
    i = pl.program_id(1)
    Q = Q_BLOCK
    R = GQA_GROUP
    G = KV_HEADS
    q0 = i * Q
    half = lax.broadcasted_iota(jnp.int32, (Q, LANE), 1) // HEAD_DIM
    qb = q_ref[...] * (HEAD_DIM ** -0.5)
    ovt = ovt_ref[...]
    ns = ovt.shape[0]
    jidx = lax.broadcasted_iota(jnp.int32, (ns, Q), 0)
    cur = (q0 + lax.broadcasted_iota(jnp.int32, (ns, Q), 1)) // SLC_BLOCK
    forced = (jidx == 0) | (jidx == cur) | (jidx == cur - 1)
    col_pos = q0 + lax.broadcasted_iota(jnp.int32, (1, R * Q), 1) % Q

    q_pads, lhss, o_cmps = [], [], []
    for g in range(G):
        slabs = []
        for r in range(R):
            h = g * R + r
            slab = qb[:, (h // 2) * LANE:(h // 2 + 1) * LANE]
            slabs.append(jnp.where(half == g, _half_select(slab, h % 2, half), 0.0))
        q_pad = jnp.concatenate(slabs, axis=0).astype(BF16)

        s = _dot_nt(kc_ref[0], q_pad) + bcmp_ref[g, 0]
        e = jnp.exp(s - jnp.max(s, axis=0, keepdims=True))
        p = e / jnp.sum(e, axis=0, keepdims=True)
        p = jnp.where(col_pos >= CMP_BLOCK - 1, p, 0.0)
        pb = p.astype(BF16)
        o_cmps.append(jnp.dot(vct_ref[0], pb, preferred_element_type=F32))

        imp = jnp.dot(ovt, pb[:, :Q], preferred_element_type=F32)
        for r in range(1, R):
            imp = imp + jnp.dot(ovt, pb[:, r * Q:(r + 1) * Q], preferred_element_type=F32)
        score = jnp.where(forced, FORCE, jnp.where(jidx <= cur, imp, NEG))
        rank = jnp.zeros((ns, Q), F32)
        for jp in range(ns):
            row = score[jp:jp + 1, :]
            rank = rank + jnp.where(jidx > jp, jnp.where(row >= score, 1.0, 0.0), jnp.where(row > score, 1.0, 0.0))
        selneg_t = jnp.where(rank < SLC_TOPK, 0.0, NEG)
        selneg_t = jnp.concatenate([selneg_t, jnp.zeros((LANE - ns, Q), F32)], axis=0)
        selneg = selneg_t.T.astype(BF16)
        q_pads.append(q_pad)
        lhss.append(jnp.concatenate([q_pad, jnp.concatenate([selneg] * R, axis=0)], axis=1))

    def flash_step(s, vt_tile, carry):
        m, l, acc = carry
        m_new = jnp.maximum(m, jnp.max(s, axis=0, keepdims=True))
        alpha = jnp.exp(m - m_new)
        pt = jnp.exp(s - m_new)
        l = alpha * l + jnp.sum(pt, axis=0, keepdims=True)
        acc = alpha * acc + jnp.dot(vt_tile, pt.astype(BF16), preferred_element_type=F32)
        return m_new, l, acc

    init = (jnp.full((1, R * Q), NEG, F32), jnp.zeros((1, R * Q), F32), jnp.zeros((LANE, R * Q), F32))

    def slc_tile(t, carry):
        k0 = pl.multiple_of(t * SLC_TK, SLC_TK)
        kt = kvs_ref[pl.ds(k0, SLC_TK), :].astype(BF16)
        keys = jnp.concatenate([kt, et_ref[pl.ds(k0, SLC_TK), :]], axis=1)
        c0 = pl.multiple_of(jnp.maximum(SLC_C - (q0 - k0), 0), LANE)
        ch = t * (SLC_TK // LANE)
        vt = jnp.concatenate([vst_ref[ch + c] for c in range(SLC_TK // LANE)], axis=1)
        return tuple(flash_step(_dot_nt(keys, lhss[g]) + bslc_ref[g, pl.ds(c0, SLC_TK), :], vt, carry[g])
                     for g in range(G))

    def slc_body(tp, carry):
        for u in range(SLC_UNROLL):
            carry = slc_tile(tp * SLC_UNROLL + u, carry)
        return carry

    n_tiles = q0 // SLC_TK + 1
    slc = lax.fori_loop(0, (n_tiles + SLC_UNROLL - 1) // SLC_UNROLL, slc_body, (init,) * G)

    k0 = pl.multiple_of(jnp.maximum(q0 - WINDOW, 0), LANE)
    kt = kvw_ref[pl.ds(k0, WIN_KEYS), :].astype(BF16)
    c0 = pl.multiple_of(WINDOW - (q0 - k0), LANE)
    ch = k0 // LANE
    vt = jnp.concatenate([vwt_ref[ch + c] for c in range(WIN_KEYS // LANE)], axis=1)
    o_wins = []
    for g in range(G):
        s = _dot_nt(kt, q_pads[g]) + bwin_ref[g, pl.ds(c0, WIN_KEYS), :]
        e = jnp.exp(s - jnp.max(s, axis=0, keepdims=True))
        o_wins.append(jnp.dot(vt, e.astype(BF16), preferred_element_type=F32) / jnp.sum(e, axis=0, keepdims=True))

    gates_t = gate_ref[...].T
    grow = lax.broadcasted_iota(jnp.int32, (GATE_PAD, Q), 0)
    for g in range(G):
        def gate_row(branch):
            rows = [jnp.sum(jnp.where(grow == branch * N_HEADS + g * R + r, gates_t, 0.0), axis=0, keepdims=True)
                    for r in range(R)]
            return jnp.concatenate(rows, axis=1)

        o = (gate_row(0) * o_cmps[g] + gate_row(1) * (slc[g][2] / slc[g][1])
             + gate_row(2) * o_wins[g])
        o = o[g * HEAD_DIM:(g + 1) * HEAD_DIM]
        for pair in range(R // 2):
            two = jnp.concatenate([o[:, (2 * pair) * Q:(2 * pair + 1) * Q],
                                   o[:, (2 * pair + 1) * Q:(2 * pair + 2) * Q]], axis=0)
            c = (g * R // 2 + pair) * LANE
            o_ref[:, c:c + LANE] = two.T


def _prompt_tables(tbl_flat, T):
    nqb = T // Q_BLOCK
    nc = T // CMP_STRIDE
    qi = jnp.arange(Q_BLOCK, dtype=jnp.int32)
    q_pos = jnp.arange(nqb, dtype=jnp.int32)[:, None, None] * Q_BLOCK + qi[None, :, None]
    c_end = jnp.arange(nc, dtype=jnp.int32) * CMP_STRIDE + (CMP_BLOCK - 1)
    big = 1 << 30
    bcmp = _bias_tables(tbl_flat, jnp.swapaxes(q_pos - c_end[None, None, :], 1, 2), big, True)

    def toeplitz(c_off, nrows, hi):
        dist = qi[None, None, :] + c_off - jnp.arange(nrows, dtype=jnp.int32)[None, :, None]
        return _bias_tables(tbl_flat, dist, hi, True)[:, 0]

    bslc = toeplitz(SLC_C, SLC_C + SLC_UNROLL * SLC_TK, big)
    bwin = toeplitz(WINDOW, WINDOW + WIN_KEYS, WINDOW)
    keys = jnp.arange(T, dtype=jnp.int32)
    et = (keys[:, None] // SLC_BLOCK == jnp.arange(LANE, dtype=jnp.int32)[None, :]).astype(BF16)
    c_start = jnp.arange(nc, dtype=jnp.int32) * CMP_STRIDE
    s_start = jnp.arange(T // SLC_BLOCK, dtype=jnp.int32) * SLC_BLOCK
    ovt = ((c_start[None, :] < s_start[:, None] + SLC_BLOCK)
           & (c_start[None, :] + CMP_BLOCK > s_start[:, None])).astype(BF16)
    return bcmp, bslc, bwin, et, ovt


def _nsa_prompt_call(q, gates, kc, vct, kvs, vst, kvw, vwt, tables, B, T):
    bcmp, bslc, bwin, et, ovt = tables
    nqb = T // Q_BLOCK
    nc = T // CMP_STRIDE
    R = GQA_GROUP
    seq = pl.BlockSpec((T, KV_W), lambda b, i: (b, 0))
    seq_t = pl.BlockSpec((T // LANE, KV_W, LANE), lambda b, i: (b, 0, 0))
    return pl.pallas_call(
        _nsa_prompt_kernel,
        grid=(B, nqb),
        in_specs=[
            pl.BlockSpec((Q_BLOCK, NSA_W), lambda b, i: (b * nqb + i, 0)),
            pl.BlockSpec((Q_BLOCK, GATE_PAD), lambda b, i: (b * nqb + i, 0)),
            pl.BlockSpec((1, nc, KV_W), lambda b, i: (b, 0, 0)),
            pl.BlockSpec((1, KV_W, nc), lambda b, i: (b, 0, 0)),
            seq, seq_t, seq, seq_t,
            pl.BlockSpec((KV_HEADS, 1, nc, R * Q_BLOCK), lambda b, i: (0, i, 0, 0)),
            pl.BlockSpec(bslc.shape, lambda b, i: (0, 0, 0)),
            pl.BlockSpec(bwin.shape, lambda b, i: (0, 0, 0)),
            pl.BlockSpec(et.shape, lambda b, i: (0, 0)),
            pl.BlockSpec(ovt.shape, lambda b, i: (0, 0)),
        ],
        out_specs=pl.BlockSpec((Q_BLOCK, NSA_W), lambda b, i: (b * nqb + i, 0)),
        out_shape=jax.ShapeDtypeStruct((B * T, NSA_W), F32),
        compiler_params=pltpu.CompilerParams(dimension_semantics=("parallel", "arbitrary"),
                                             vmem_limit_bytes=VMEM_LIMIT),
        name="nsa_prompt",
    )(q, gates, kc, vct, kvs, vst, kvw, vwt, bcmp, bslc, bwin, et, ovt)


SUB = 8
PAGES_PER_STEP = 64


def _compress_pages_kernel(xk_ref, xv_ref, pea_ref, peb_ref, wa_ref, wb_ref, o_ref):
    pb, page, hw = xk_ref.shape
    w = 2 * hw
    ch = page // CMP_STRIDE
    a = jnp.zeros((pb * ch, w), F32)
    b = jnp.zeros((pb * ch, w), F32)
    for t in range(CMP_STRIDE):
        x = jnp.concatenate([ref[:, pl.ds(t, ch, stride=CMP_STRIDE), :].reshape(pb * ch, hw)
                             for ref in (xk_ref, xv_ref)], axis=1)
        cols = slice(t * w, (t + 1) * w)
        a = a + jnp.dot((x + pea_ref[:, cols]).astype(BF16), wa_ref[cols, :], preferred_element_type=F32)
        b = b + jnp.dot((x + peb_ref[:, cols]).astype(BF16), wb_ref[cols, :], preferred_element_type=F32)
    o_ref[...] = jnp.concatenate([a, b], axis=1).reshape(pb, ch, 2 * w)


def _compress_pages(pages, layer, n_phys, cw):
    _, page, w = pages.shape
    ch = page // CMP_STRIDE
    steps = n_phys // PAGES_PER_STEP
    full = lambda a: pl.BlockSpec(a.shape, lambda i: (0, 0))
    return pl.pallas_call(
        _compress_pages_kernel,
        grid=(steps,),
        in_specs=[pl.BlockSpec((PAGES_PER_STEP, page, w // 2), lambda i: (layer * steps + i, 0, 0)),
                  pl.BlockSpec((PAGES_PER_STEP, page, w // 2), lambda i: (layer * steps + i, 0, 1))]
                 + [full(a) for a in cw],
        out_specs=pl.BlockSpec((PAGES_PER_STEP, ch, 4 * KV_W), lambda i: (i, 0, 0)),
        out_shape=jax.ShapeDtypeStruct((n_phys, ch, 4 * KV_W), F32),
        compiler_params=pltpu.CompilerParams(dimension_semantics=("parallel",), vmem_limit_bytes=VMEM_LIMIT),
        name="compress_pages",
    )(pages, pages, *cw)


def _nsa_sample_kernel(pt_ref, q_ref, gate_ref, ks_new_ref, kw_new_ref, *rest, n_pages, past, dec_seq):
    ab_refs = rest[:n_pages]
    slc_refs = rest[n_pages:2 * n_pages]
    win_ref, bcmp_ref, bslc_ref, bwin_ref, et_ref, ov_ref, o_ref, win_out_ref = rest[2 * n_pages:]
    R, G = GQA_GROUP, KV_HEADS
    rows = R * G * SUB
    half = lax.broadcasted_iota(jnp.int32, (SUB, LANE), 1) // HEAD_DIM

    qb = q_ref[0] * (HEAD_DIM ** -0.5)
    slabs = []
    for r in range(R):
        for g in range(G):
            h = g * R + r
            slab = qb[:, (h // 2) * LANE:(h // 2 + 1) * LANE]
            slabs.append(jnp.where(half == g, _half_select(slab, h % 2, half), 0.0))
    q_pad = jnp.concatenate(slabs, axis=0).astype(BF16)

    ab = jnp.concatenate([r_[0] for r_ in ab_refs], axis=0)
    nc = ab.shape[0]
    kcvc = ab[:, :2 * KV_W] + pltpu.roll(ab[:, 2 * KV_W:], nc - 1, axis=0)
    kc = kcvc[:, :KV_W].astype(BF16)
    vc = kcvc[:, KV_W:].astype(BF16)
    s = _dot_nt(q_pad, kc) + bcmp_ref[...]
    e = jnp.exp(s - jnp.max(s, axis=-1, keepdims=True))
    p = e / jnp.sum(e, axis=-1, keepdims=True)
    row_pos = past + lax.broadcasted_iota(jnp.int32, (rows, 1), 0) % SUB
    p = jnp.where(row_pos >= CMP_BLOCK - 1, p, 0.0)
    pb = p.astype(BF16)
    o_cmp = jnp.dot(pb, vc, preferred_element_type=F32)

    ov = ov_ref[...]
    imp = jnp.dot(pb[:G * SUB], ov, preferred_element_type=F32)
    for r in range(1, R):
        imp = imp + jnp.dot(pb[r * G * SUB:(r + 1) * G * SUB], ov, preferred_element_type=F32)
    jidx = lax.broadcasted_iota(jnp.int32, (G * SUB, LANE), 1)
    cur = (past + lax.broadcasted_iota(jnp.int32, (G * SUB, LANE), 0) % SUB) // SLC_BLOCK
    forced = (jidx == 0) | (jidx == cur) | (jidx == cur - 1)
    score = jnp.where(forced, FORCE, jnp.where(jidx <= cur, imp, NEG))
    n_blocks = (past + dec_seq + SLC_BLOCK - 1) // SLC_BLOCK
    rank = jnp.zeros((G * SUB, LANE), F32)
    for jp in range(n_blocks):
        col = score[:, jp:jp + 1]
        rank = rank + jnp.where(jidx > jp, jnp.where(col >= score, 1.0, 0.0), jnp.where(col > score, 1.0, 0.0))
    selneg = jnp.where(rank < min(SLC_TOPK, n_blocks), 0.0, NEG).astype(BF16)
    lhs = jnp.concatenate([q_pad, jnp.concatenate([selneg] * R, axis=0)], axis=1)

    def new_cols(ref):
        x = ref[0]
        xt = jnp.concatenate([x, jnp.zeros((LANE - SUB, 2 * KV_W), F32)], axis=0).T
        return xt, xt[:KV_W].astype(BF16), xt[KV_W:].astype(BF16)

    def attend(lhs_, kt_tiles, vt_tiles, bias, ext):
        parts = []
        for t, kt in enumerate(kt_tiles):
            rhs = jnp.concatenate([kt, et_ref[:, t * LANE:(t + 1) * LANE]], axis=0) if ext else kt
            parts.append(jnp.dot(lhs_, rhs, preferred_element_type=F32))
        s_ = jnp.concatenate(parts, axis=1) + bias
        e_ = jnp.exp(s_ - jnp.max(s_, axis=-1, keepdims=True))
        p_ = (e_ / jnp.sum(e_, axis=-1, keepdims=True)).astype(BF16)
        o_ = _dot_nt(p_[:, :LANE], vt_tiles[0])
        for t in range(1, len(vt_tiles)):
            o_ = o_ + _dot_nt(p_[:, t * LANE:(t + 1) * LANE], vt_tiles[t])
        return o_

    _, kt_new, vt_new = new_cols(ks_new_ref)
    kt_tiles = [r_[0, :KV_W, :].astype(BF16) for r_ in slc_refs] + [kt_new]
    vt_tiles = [r_[0, KV_W:, :].astype(BF16) for r_ in slc_refs] + [vt_new]
    o_slc = attend(lhs, kt_tiles, vt_tiles, bslc_ref[...], True)

    win_new_t, kt_new, vt_new = new_cols(kw_new_ref)
    wb = win_ref.shape[2]
    kt_tiles = [win_ref[0, :KV_W, t * LANE:(t + 1) * LANE].astype(BF16) for t in range(wb // LANE)] + [kt_new]
    vt_tiles = [win_ref[0, KV_W:, t * LANE:(t + 1) * LANE].astype(BF16) for t in range(wb // LANE)] + [vt_new]
    o_win = attend(q_pad, kt_tiles, vt_tiles, bwin_ref[...], False)

    gates = gate_ref[0]
    glane = lax.broadcasted_iota(jnp.int32, (SUB, LANE), 1)

    def gate_col(branch):
        cols = [jnp.sum(jnp.where(glane == branch * N_HEADS + g * R + r, gates, 0.0), axis=-1, keepdims=True)
                for r in range(R) for g in range(G)]
        return jnp.concatenate(cols, axis=0)

    o = gate_col(0) * o_cmp + gate_col(1) * o_slc + gate_col(2) * o_win
    for g in range(G):
        for pair in range(R // 2):
            r0 = ((2 * pair) * G + g) * SUB
            r1 = ((2 * pair + 1) * G + g) * SUB
            lo = _half_select(o[r0:r0 + SUB], g, half)
            hi = _half_select(o[r1:r1 + SUB], g, half)
            c = (g * R // 2 + pair) * LANE
            o_ref[0, :, c:c + LANE] = jnp.where(half == 0, lo, hi)

    shifted = pltpu.roll(win_ref[0], wb - dec_seq, axis=1)
    win_out_ref[0, :, :wb - LANE] = shifted[:, :wb - LANE]
    klane = lax.broadcasted_iota(jnp.int32, (2 * KV_W, LANE), 1)
    win_out_ref[0, :, wb - LANE:] = jnp.where(klane >= LANE - dec_seq, pltpu.roll(win_new_t, LANE - dec_seq, axis=1),
                                              shifted[:, wb - LANE:])


def _sample_tables(tbl_flat, past, dec_seq, wb):
    q_pos = past + jnp.arange(SUB, dtype=jnp.int32)
    nc = past // CMP_STRIDE
    c_end = jnp.arange(nc, dtype=jnp.int32) * CMP_STRIDE + (CMP_BLOCK - 1)
    slc_keys = jnp.arange(past + LANE, dtype=jnp.int32)
    win_pos = jnp.concatenate([past - wb + jnp.arange(wb, dtype=jnp.int32), past + jnp.arange(LANE, dtype=jnp.int32)])
    big = 1 << 30

    def table(key_pos, hi):
        t = _bias_tables(tbl_flat, (q_pos[:, None] - key_pos[None, :])[None], hi)
        c = t.shape[-1]
        return t.reshape(KV_HEADS, GQA_GROUP, SUB, c).transpose(1, 0, 2, 3).reshape(GQA_GROUP * KV_HEADS * SUB, c)

    bcmp = table(c_end, big)
    bslc = table(slc_keys, big)
    bwin = table(win_pos, WINDOW)
    def kill_pad(b):
        lane = jnp.arange(b.shape[1]) - (b.shape[1] - LANE)
        return jnp.where((lane >= dec_seq)[None, :], NEG, b)
    bslc, bwin = kill_pad(bslc), kill_pad(bwin)
    et = (slc_keys[None, :] // SLC_BLOCK == jnp.arange(LANE, dtype=jnp.int32)[:, None]).astype(BF16)
    c_start = jnp.arange(nc, dtype=jnp.int32) * CMP_STRIDE
    s_start = jnp.arange(LANE, dtype=jnp.int32) * SLC_BLOCK
    ov = ((c_start[:, None] < s_start[None, :] + SLC_BLOCK)
          & (c_start[:, None] + CMP_BLOCK > s_start[None, :])).astype(BF16)
    return bcmp, bslc, bwin, et, ov


def _nsa_sample_call(page_table_flat, q, gates, ks_new, kw_new, ab_pages, slc_pages, win_buf, tables, layer, n_phys,
                     dec_seq):
    bcmp, bslc, bwin, et, ov = tables
    B = q.shape[0]
    n_pages = page_table_flat.shape[0] // B
    page = slc_pages.shape[2]
    past = n_pages * page
    wb = win_buf.shape[2]
    row3 = lambda a: pl.BlockSpec((1,) + a.shape[1:], lambda b, pt: (b, 0, 0))
    full = lambda a: pl.BlockSpec(a.shape, lambda b, pt: (0, 0))
    ab_specs = [pl.BlockSpec((1,) + ab_pages.shape[1:], lambda b, pt, p=p: (pt[b * n_pages + p], 0, 0))
                for p in range(n_pages)]
    slc_specs = [pl.BlockSpec((1,) + slc_pages.shape[1:], lambda b, pt, p=p: (layer * n_phys + pt[b * n_pages + p], 0, 0))
                 for p in range(n_pages)]
    return pl.pallas_call(
        functools.partial(_nsa_sample_kernel, n_pages=n_pages, past=past, dec_seq=dec_seq),
        grid_spec=pltpu.PrefetchScalarGridSpec(
            num_scalar_prefetch=1,
            grid=(B,),
            in_specs=[row3(q), row3(gates), row3(ks_new), row3(kw_new)] + ab_specs + slc_specs
                     + [pl.BlockSpec((1, 2 * KV_W, wb), lambda b, pt: (layer * B + b, 0, 0)),
                        full(bcmp), full(bslc), full(bwin), full(et), full(ov)],
            out_specs=[pl.BlockSpec((1, SUB, NSA_W), lambda b, pt: (b, 0, 0)),
                       pl.BlockSpec((1, 2 * KV_W, wb), lambda b, pt: (b, 0, 0))],
        ),
        out_shape=[jax.ShapeDtypeStruct((B, SUB, NSA_W), F32), jax.ShapeDtypeStruct((B, 2 * KV_W, wb), F32)],
        compiler_params=pltpu.CompilerParams(dimension_semantics=("parallel",), vmem_limit_bytes=VMEM_LIMIT),
        name="nsa_sample",
    )(page_table_flat, q, gates, ks_new, kw_new, *([ab_pages] * n_pages), *([slc_pages] * n_pages), win_buf,
      bcmp, bslc, bwin, et, ov)


CONV_HALO = 32
POOL_HALO = 16


def _mix_kernel(glu_ref, glu_halo_ref, glu_hist_ref, u_ref, u_halo_ref, u_hist_ref, dw_ref, cb_ref, lng_ref, lnb_ref,
                wp_ref, ps_ref, conv_ref, pool_ref, xs_ref, us_ref, *, pos0):
    j = pl.program_id(1)
    tq = glu_ref.shape[0]

    @pl.when(j == 0)
    def _():
        xs_ref[:CONV_HALO, :] = glu_hist_ref[0]
        us_ref[:POOL_HALO, :] = u_hist_ref[0]

    @pl.when(j > 0)
    def _():
        xs_ref[:CONV_HALO, :] = glu_halo_ref[...]
        us_ref[:POOL_HALO, :] = u_halo_ref[...]

    xs_ref[CONV_HALO:, :] = glu_ref[...]
    us_ref[POOL_HALO:, :] = u_ref[...]

    base = CONV_HALO - CONV_BUF
    y = xs_ref[base:base + tq, :] * dw_ref[0:1, :]
    for k in range(1, CONV_WIDTH):
        y = y + xs_ref[base + k:base + k + tq, :] * dw_ref[k:k + 1, :]
    y = y + cb_ref[...]
    mu = jnp.mean(y, axis=-1, keepdims=True)
    yc = y - mu
    var = jnp.mean(yc * yc, axis=-1, keepdims=True)
    conv_ref[...] = jax.nn.silu(yc * lax.rsqrt(var + EPS) * lng_ref[...] + lnb_ref[...])

    u = u_ref[...]
    pos = pos0 + j * tq + lax.broadcasted_iota(jnp.int32, (tq, 1), 0)
    grp = lax.broadcasted_iota(jnp.int32, (tq, POOL_W), 1) // POOL_GROUP_W
    run = u
    d = jnp.zeros_like(u)
    done = 1
    for gi, w in enumerate(POOL_WINDOWS):
        for back in range(done, w):
            run = run + us_ref[POOL_HALO - back:POOL_HALO - back + tq, :]
        done = w
        cnt = jnp.minimum(w, pos + 1).astype(F32)
        d = jnp.where(grp == gi, run / cnt - u, d)
    yp = jnp.dot(d.astype(BF16), wp_ref[...], preferred_element_type=F32)
    pool_ref[...] = yp * ps_ref[...]


def _mix(glu, u, glu_hist, u_hist, dw, cb, lng, lnb, wp_bd, ps, B, T, tq, pos0):
    nt = T // tq
    row = pl.BlockSpec((tq, CONV_W), lambda b, j: (b * nt + j, 0))
    full = lambda a: pl.BlockSpec(a.shape, lambda b, j: (0, 0))

    def halo(h):
        return pl.BlockSpec((h, CONV_W), lambda b, j: (jnp.maximum(b * (T // h) + j * (tq // h) - 1, 0), 0))

    hist = lambda h: pl.BlockSpec((1, h, CONV_W), lambda b, j: (b, 0, 0))
    if T < CONV_HALO:
        glu_halo_arr, u_halo_arr = glu_hist.reshape(-1, CONV_W), u_hist.reshape(-1, POOL_W)
        halo_c = pl.BlockSpec((CONV_HALO, CONV_W), lambda b, j: (0, 0))
        halo_p = pl.BlockSpec((POOL_HALO, POOL_W), lambda b, j: (0, 0))
    else:
        glu_halo_arr, u_halo_arr = glu, u
        halo_c, halo_p = halo(CONV_HALO), halo(POOL_HALO)
    return pl.pallas_call(
        functools.partial(_mix_kernel, pos0=pos0),
        grid=(B, nt),
        in_specs=[row, halo_c, hist(CONV_HALO), row, halo_p, hist(POOL_HALO),
                  full(dw), full(cb), full(lng), full(lnb), full(wp_bd), full(ps)],
        out_specs=[row, row],
        out_shape=[jax.ShapeDtypeStruct((B * T, CONV_W), F32), jax.ShapeDtypeStruct((B * T, POOL_W), F32)],
        scratch_shapes=[pltpu.VMEM((CONV_HALO + tq, CONV_W), F32), pltpu.VMEM((POOL_HALO + tq, POOL_W), F32)],
        compiler_params=pltpu.CompilerParams(dimension_semantics=("parallel", "parallel"),
                                             vmem_limit_bytes=VMEM_LIMIT),
        name="conv_pool",
    )(glu, glu_halo_arr, glu_hist, u, u_halo_arr, u_hist, dw, cb, lng, lnb, wp_bd, ps)


def _layer(x2d, B, T, l, P, glu_hist, u_hist, pos0, nsa_fn, final, emit_vt):
    n = B * T
    glu, u, q, kvc, kvs, kvw, gates, *vts = _inproj(x2d, P['norm1'][l], P['w_in'][l], min(n, 512), emit_vt)
    conv_o, pool_o = _mix(glu, u, glu_hist, u_hist, P['conv_dw'][l], P['conv_b'][l], P['conv_ln_g'][l],
                          P['conv_ln_b'][l], P['pool_w'][l], P['pool_scale'][l], B, T, min(T, 512), pos0)
    nsa_o, extra = nsa_fn(q, gates, kvc, kvs, kvw, *vts)
    y = _ffn(x2d, conv_o, pool_o, nsa_o, P['w_out'][l], P['norm2'][l], P['w_gu'][l], P['w_down'][l],
             P['final_norm'], min(n, 512), D_FF, final)
    return y, glu, u, kvc, kvs, kvw, extra


def kernel(x_prompt, x_sample, cache_cmp_kv, cache_slc_kv, cache_win_kv, state_conv, state_pool, page_table, rel_bias, norm1, w_in, conv_dw, conv_b, conv_ln_g, conv_ln_b, pool_w, pool_scale, cmp_pe_k, cmp_wk, cmp_pe_v, cmp_wv, w_out, norm2, w_gu, w_down, final_norm):
    depth = w_in.shape[0]
    Bp, T, _ = x_prompt.shape
    Bs, S, _ = x_sample.shape
    n_phys, page = cache_cmp_kv.shape[1:3]
    n_pages = page_table.shape[1]
    past = n_pages * page
    wb = cache_win_kv.shape[2]
    tbl_flat = rel_bias.reshape(NUM_BUCKETS, KV_HEADS, GQA_GROUP).transpose(1, 2, 0).reshape(-1)
    eye = jnp.eye(len(POOL_WINDOWS), dtype=F32)
    row = lambda a: a[:, None, :]
    P = dict(
        norm1=row(norm1), norm2=row(norm2), final_norm=final_norm[None, :],
        w_in=jnp.pad(w_in, ((0, 0), (0, 0), (0, IN_W_PAD - IN_W))).astype(BF16),
        w_out=w_out.astype(BF16), w_gu=w_gu.astype(BF16), w_down=w_down.astype(BF16),
        conv_dw=conv_dw, conv_b=row(conv_b), conv_ln_g=row(conv_ln_g), conv_ln_b=row(conv_ln_b),
        pool_w=jnp.einsum('lgce,gh->lgche', pool_w, eye).reshape(depth, POOL_W, POOL_W).astype(BF16),
        pool_scale=row(pool_scale))
    ptables = _prompt_tables(tbl_flat, T)
    stables = _sample_tables(tbl_flat, past, S, wb)
    cmp_pages = cache_cmp_kv.reshape(depth * n_phys, page, 2 * KV_W)
    slc_pages = jnp.swapaxes(cache_slc_kv.reshape(depth * n_phys, page, 2 * KV_W), 1, 2)
    win_bufs = jnp.swapaxes(cache_win_kv.reshape(depth * Bs, wb, 2 * KV_W), 1, 2)
    pt_flat = page_table.reshape(-1)

    xp = x_prompt.reshape(Bp * T, D_MODEL)
    xs = jnp.pad(x_sample, ((0, 0), (0, SUB - S), (0, 0))).reshape(Bs * SUB, D_MODEL)
    zc = jnp.zeros((Bp, CONV_HALO, CONV_W), F32)
    zp = jnp.zeros((Bp, POOL_HALO, POOL_W), F32)
    kv5 = lambda a, b, t: a.reshape(b, t, 2, KV_HEADS, HEAD_DIM)
    outs_p, outs_s = [], []
    for l in range(depth):
        final = l == depth - 1
        cw = _compress_weights(cmp_pe_k[l], cmp_wk[l], cmp_pe_v[l], cmp_wv[l])

        def nsa_p(q, gates, kvc, kvs, kvw, vst, vwt):
            kc, vct = _compress_call(kvc.reshape(Bp, T // CMP_STRIDE, CMP_STRIDE * 2 * KV_W), cw)
            return _nsa_prompt_call(q, gates, kc, vct, kvs, vst, kvw, vwt, ptables, Bp, T), None

        xp, glu, u, kvc, kvs, kvw, _ = _layer(xp, Bp, T, l, P, zc, zp, 0, nsa_p, final, True)
        outs_p.append((kv5(kvc, Bp, T), kv5(kvs, Bp, T), kv5(kvw, Bp, T)[:, T - min(WINDOW, T):],
                       glu.reshape(Bp, T, CONV_W)[:, T - CONV_BUF:], u.reshape(Bp, T, POOL_W)[:, T - POOL_BUF:]))

        def nsa_s(q, gates, kvc, kvs, kvw):
            ab = _compress_pages(cmp_pages, l, n_phys, cw)
            r3 = lambda a: a.reshape(Bs, SUB, a.shape[-1])
            o, new_win = _nsa_sample_call(pt_flat, r3(q), r3(gates), r3(kvs), r3(kvw), ab, slc_pages, win_bufs,
                                          stables, l, n_phys, S)
            return o.reshape(Bs * SUB, NSA_W), new_win

        glu_hist = jnp.pad(state_conv[l], ((0, 0), (CONV_HALO - CONV_BUF, 0), (0, 0)))
        u_hist = jnp.pad(state_pool[l], ((0, 0), (POOL_HALO - POOL_BUF, 0), (0, 0)))
        xs, glu, u, kvc, kvs, kvw, new_win = _layer(xs, Bs, SUB, l, P, glu_hist, u_hist, past, nsa_s, final, False)
        new_conv = jnp.concatenate([state_conv[l], glu.reshape(Bs, SUB, CONV_W)[:, :S]], axis=1)[:, -CONV_BUF:]
        new_pool = jnp.concatenate([state_pool[l], u.reshape(Bs, SUB, POOL_W)[:, :S]], axis=1)[:, -POOL_BUF:]
        outs_s.append((kv5(kvc, Bs, SUB)[:, :S], kv5(kvs, Bs, SUB)[:, :S], kv5(jnp.swapaxes(new_win, 1, 2), Bs, wb), new_conv, new_pool))
    st = lambda outs, i: jnp.stack([o[i] for o in outs])
    return (xp.reshape(Bp, T, D_MODEL), xs.reshape(Bs, SUB, D_MODEL)[:, :S],
            st(outs_p, 0), st(outs_p, 1), st(outs_p, 2), st(outs_p, 3), st(outs_p, 4),
            st(outs_s, 0), st(outs_s, 1), st(outs_s, 2), st(outs_s, 3), st(outs_s, 4))
```

```python
import functools
import math

import jax
import jax.numpy as jnp
from jax import lax
from jax.experimental import pallas as pl
from jax.experimental.pallas import tpu as pltpu

D_MODEL = 1024
CONV_W = 256
POOL_W = 256
N_HEADS = 8
HEAD_DIM = 64
NSA_W = N_HEADS * HEAD_DIM
KV_HEADS = 2
GQA_GROUP = N_HEADS // KV_HEADS
KV_W = KV_HEADS * HEAD_DIM
CONV_WIDTH = 31
CONV_BUF = CONV_WIDTH - 1
POOL_WINDOWS = (2, 4, 8, 16)
POOL_GROUP_W = POOL_W // len(POOL_WINDOWS)
POOL_BUF = max(POOL_WINDOWS) - 1
CMP_STRIDE = 16
CMP_BLOCK = 2 * CMP_STRIDE
SLC_BLOCK = 64
SLC_TOPK = 16
WINDOW = 512
Q_BLOCK = 128
NUM_BUCKETS = 32
MAX_DISTANCE = 128
D_FF = 2816
EPS = 1e-6
NEG = -1e30
FORCE = 1e4

OFF_POOL = 2 * CONV_W
OFF_Q = OFF_POOL + POOL_W
OFF_KV = OFF_Q + NSA_W
OFF_GATE = OFF_KV + 6 * KV_W
IN_W = OFF_GATE + 3 * N_HEADS
LANE = 128
IN_W_PAD = -(-IN_W // LANE) * LANE
GATE_PAD = IN_W_PAD - OFF_GATE

VMEM_LIMIT = 56 * 1024 * 1024

F32 = jnp.float32
BF16 = jnp.bfloat16


def _rms(x, g):
    return x * lax.rsqrt(jnp.mean(x * x, axis=-1, keepdims=True) + EPS) * g


def _inproj_kernel(x_ref, g_ref, w_ref, glu_ref, u_ref, q_ref, kvc_ref, kvs_ref, kvw_ref, gate_ref, *vt_refs):
    h = _rms(x_ref[...], g_ref[...]).astype(BF16)
    z = jnp.dot(h, w_ref[...], preferred_element_type=F32)
    glu_ref[...] = z[:, :CONV_W] * jax.nn.sigmoid(z[:, CONV_W:OFF_POOL])
    u_ref[...] = z[:, OFF_POOL:OFF_Q]
    q_ref[...] = z[:, OFF_Q:OFF_KV]
    kvc_ref[...] = z[:, OFF_KV:OFF_KV + 2 * KV_W]
    kvs_ref[...] = z[:, OFF_KV + 2 * KV_W:OFF_KV + 4 * KV_W]
    kvw_ref[...] = z[:, OFF_KV + 4 * KV_W:OFF_GATE]
    gate_ref[...] = jax.nn.sigmoid(z[:, OFF_GATE:])
    for vt_ref, off in zip(vt_refs, (OFF_KV + 3 * KV_W, OFF_KV + 5 * KV_W)):
        for c in range(vt_ref.shape[0]):
            vt_ref[c] = z[c * LANE:(c + 1) * LANE, off:off + KV_W].T.astype(BF16)


def _inproj(x2d, g, w_pad, tm, emit_vt):
    n = x2d.shape[0]
    row = lambda w: pl.BlockSpec((tm, w), lambda i: (i, 0))
    full = lambda a: pl.BlockSpec(a.shape, lambda i: (0, 0))
    widths = (CONV_W, POOL_W, NSA_W, 2 * KV_W, 2 * KV_W, 2 * KV_W, GATE_PAD)
    out_specs = [row(w) for w in widths]
    out_shape = [jax.ShapeDtypeStruct((n, w), F32) for w in widths]
    if emit_vt:
        out_specs += [pl.BlockSpec((tm // LANE, KV_W, LANE), lambda i: (i, 0, 0))] * 2
        out_shape += [jax.ShapeDtypeStruct((n // LANE, KV_W, LANE), BF16)] * 2
    return pl.pallas_call(
        _inproj_kernel,
        grid=(n // tm,),
        in_specs=[row(D_MODEL), full(g), full(w_pad)],
        out_specs=out_specs,
        out_shape=out_shape,
        compiler_params=pltpu.CompilerParams(dimension_semantics=("parallel",), vmem_limit_bytes=VMEM_LIMIT),
        name="inproj",
    )(x2d, g, w_pad)


def _ffn_kernel(x_ref, conv_ref, pool_ref, nsa_ref, wo_ref, g2_ref, wg_ref, wu_ref, wd_ref, gf_ref,
                y_ref, x1_ref, h2_ref, acc_ref, *, final_norm):
    j = pl.program_id(1)

    @pl.when(j == 0)
    def _():
        mix = jnp.concatenate([conv_ref[...], pool_ref[...], nsa_ref[...]], axis=-1).astype(BF16)
        x1 = x_ref[...] + jnp.dot(mix, wo_ref[...], preferred_element_type=F32)
        x1_ref[...] = x1
        h2_ref[...] = _rms(x1, g2_ref[...]).astype(BF16)
        acc_ref[...] = jnp.zeros_like(acc_ref)

    h2 = h2_ref[...]
    gate = jnp.dot(h2, wg_ref[...], preferred_element_type=F32)
    up = jnp.dot(h2, wu_ref[...], preferred_element_type=F32)
    act = (jax.nn.silu(gate) * up).astype(BF16)
    acc_ref[...] += jnp.dot(act, wd_ref[...], preferred_element_type=F32)

    @pl.when(j == pl.num_programs(1) - 1)
    def _():
        y = x1_ref[...] + acc_ref[...]
        if final_norm:
            y = _rms(y, gf_ref[...])
        y_ref[...] = y


def _ffn(x2d, conv_o, pool_o, nsa_o, w_out, g2, w_gu, w_down, gf, tm, tf, final_norm):
    n = x2d.shape[0]
    nf = D_FF // tf
    row = lambda w: pl.BlockSpec((tm, w), lambda i, j: (i, 0))
    full = lambda a: pl.BlockSpec(a.shape, lambda i, j: (0, 0))
    return pl.pallas_call(
        functools.partial(_ffn_kernel, final_norm=final_norm),
        grid=(n // tm, nf),
        in_specs=[row(D_MODEL), row(CONV_W), row(POOL_W), row(NSA_W), full(w_out), full(g2),
                  pl.BlockSpec((D_MODEL, tf), lambda i, j: (0, j)),
                  pl.BlockSpec((D_MODEL, tf), lambda i, j: (0, j + nf)),
                  pl.BlockSpec((tf, D_MODEL), lambda i, j: (j, 0)),
                  full(gf)],
        out_specs=row(D_MODEL),
        out_shape=jax.ShapeDtypeStruct((n, D_MODEL), F32),
        scratch_shapes=[pltpu.VMEM((tm, D_MODEL), F32), pltpu.VMEM((tm, D_MODEL), BF16),
                        pltpu.VMEM((tm, D_MODEL), F32)],
        compiler_params=pltpu.CompilerParams(dimension_semantics=("parallel", "arbitrary"),
                                             vmem_limit_bytes=VMEM_LIMIT),
        name="ffn",
    )(x2d, conv_o, pool_o, nsa_o, w_out, g2, w_gu, w_gu, w_down, gf)


def _bias_kernel(tbl_ref, dist_ref, out_ref, *, hi, heads_on_lanes):
    g = pl.program_id(0)
    d = dist_ref[0]
    rows, cols = d.shape
    n = jnp.maximum(d, 0)
    max_exact = NUM_BUCKETS // 2
    nf = jnp.maximum(n, 1).astype(F32)
    large = max_exact + (jnp.log(nf / max_exact) / math.log(MAX_DISTANCE / max_exact)
                         * (NUM_BUCKETS - max_exact)).astype(jnp.int32)
    large = jnp.minimum(large, NUM_BUCKETS - 1)
    bucket = jnp.where(n < max_exact, n, large)
    valid = (d >= 0) & (d < hi)
    for r in range(GQA_GROUP):
        acc = jnp.zeros(d.shape, F32)
        for b in range(NUM_BUCKETS):
            acc = jnp.where(bucket == b, tbl_ref[(g * GQA_GROUP + r) * NUM_BUCKETS + b], acc)
        val = jnp.where(valid, acc, NEG)
        if heads_on_lanes:
            out_ref[0, 0, :, r * cols:(r + 1) * cols] = val
        else:
            out_ref[0, 0, r * rows:(r + 1) * rows, :] = val


def _bias_tables(tbl_flat, dist, hi, heads_on_lanes=False):
    nblk, R, C = dist.shape
    oshape = (R, GQA_GROUP * C) if heads_on_lanes else (GQA_GROUP * R, C)
    return pl.pallas_call(
        functools.partial(_bias_kernel, hi=hi, heads_on_lanes=heads_on_lanes),
        grid_spec=pltpu.PrefetchScalarGridSpec(
            num_scalar_prefetch=1,
            grid=(KV_HEADS, nblk),
            in_specs=[pl.BlockSpec((1, R, C), lambda g, i, tbl: (i, 0, 0))],
            out_specs=pl.BlockSpec((1, 1) + oshape, lambda g, i, tbl: (g, i, 0, 0)),
        ),
        out_shape=jax.ShapeDtypeStruct((KV_HEADS, nblk) + oshape, F32),
        compiler_params=pltpu.CompilerParams(dimension_semantics=("parallel", "parallel")),
        name="bias_tables",
    )(tbl_flat, dist)


def _compress_kernel(x_ref, pea_ref, peb_ref, wa_ref, wb_ref, k_ref, vt_ref):
    x = x_ref[0]
    a = jnp.dot((x + pea_ref[...]).astype(BF16), wa_ref[...], preferred_element_type=F32)
    b = jnp.dot((x + peb_ref[...]).astype(BF16), wb_ref[...], preferred_element_type=F32)
    kv = a + pltpu.roll(b, b.shape[0] - 1, axis=0)
    k_ref[0] = kv[:, :KV_W].astype(BF16)
    vt_ref[0] = kv[:, KV_W:].T.astype(BF16)


def _compress_weights(pe_k, wk, pe_v, wv):
    eye = jnp.eye(2 * KV_HEADS, dtype=F32)

    def half(lo):
        w = jnp.stack([wk[lo:lo + CMP_STRIDE], wk[lo:lo + CMP_STRIDE], wv[lo:lo + CMP_STRIDE], wv[lo:lo + CMP_STRIDE]], 1)
        big = jnp.einsum('lcde,cf->lcdfe', w, eye).reshape(CMP_STRIDE * 2 * KV_W, 2 * KV_W)
        pe = jnp.stack([pe_k[lo:lo + CMP_STRIDE], pe_k[lo:lo + CMP_STRIDE], pe_v[lo:lo + CMP_STRIDE], pe_v[lo:lo + CMP_STRIDE]], 1)
        return pe.reshape(1, CMP_STRIDE * 2 * KV_W), big.astype(BF16)

    pea, wa = half(0)
    peb, wb = half(CMP_STRIDE)
    return pea, peb, wa, wb


def _compress_call(x3, cw):
    B, nc, cw_in = x3.shape
    full = lambda a: pl.BlockSpec(a.shape, lambda b: (0, 0))
    return pl.pallas_call(
        _compress_kernel,
        grid=(B,),
        in_specs=[pl.BlockSpec((1, nc, cw_in), lambda b: (b, 0, 0))] + [full(a) for a in cw],
        out_specs=[pl.BlockSpec((1, nc, KV_W), lambda b: (b, 0, 0)), pl.BlockSpec((1, KV_W, nc), lambda b: (b, 0, 0))],
        out_shape=[jax.ShapeDtypeStruct((B, nc, KV_W), BF16), jax.ShapeDtypeStruct((B, KV_W, nc), BF16)],
        compiler_params=pltpu.CompilerParams(dimension_semantics=("parallel",), vmem_limit_bytes=VMEM_LIMIT),
        name="compress",
    )(x3, *cw)


SLC_TK = 512
SLC_C = SLC_TK + LANE
SLC_UNROLL = 1
WIN_KEYS = WINDOW + Q_BLOCK


def _dot_nt(a, b):
    return lax.dot_general(a, b, (((1,), (1,)), ((), ())), preferred_element_type=F32)


def _half_select(slab, want_half, half):
    return jnp.where(half == want_half, slab, pltpu.roll(slab, HEAD_DIM, axis=1))


def _nsa_prompt_kernel(q_ref, gate_ref, kc_ref, vct_ref, kvs_ref, vst_ref, kvw_ref, vwt_ref, bcmp_ref, bslc_ref,
                       bwin_ref, et_ref, ovt_ref, o_ref):
    i = pl.program_id(1)
    Q = Q_BLOCK
    R = GQA_GROUP
    G = KV_HEADS
    q0 = i * Q
    half = lax.broadcasted_iota(jnp.int32, (Q, LANE), 1) // HEAD_DIM
    qb = q_ref[...] * (HEAD_DIM ** -0.5)
    ovt = ovt_ref[...]
    ns = ovt.shape[0]
    jidx = lax.broadcasted_iota(jnp.int32, (ns, Q), 0)
    cur = (q0 + lax.broadcasted_iota(jnp.int32, (ns, Q), 1)) // SLC_BLOCK
    forced = (jidx == 0) | (jidx == cur) | (jidx == cur - 1)
    col_pos = q0 + lax.broadcasted_iota(jnp.int32, (1, R * Q), 1) % Q

    q_pads, lhss, o_cmps = [], [], []
    for g in range(G):
        slabs = []
        for r in range(R):
            h = g * R + r
            slab = qb[:, (h // 2) * LANE:(h // 2 + 1) * LANE]
            slabs.append(jnp.where(half == g, _half_select(slab, h % 2, half), 0.0))
        q_pad = jnp.concatenate(slabs, axis=0).astype(BF16)

        s = _dot_nt(kc_ref[0], q_pad) + bcmp_ref[g, 0]
        e = jnp.exp(s - jnp.max(s, axis=0, keepdims=True))
        p = e / jnp.sum(e, axis=0, keepdims=True)
        p = jnp.where(col_pos >= CMP_BLOCK - 1, p, 0.0)
        pb = p.astype(BF16)
        o_cmps.append(jnp.dot(vct_ref[0], pb, preferred_element_type=F32))

        imp = jnp.dot(ovt, pb[:, :Q], preferred_element_type=F32)
        for r in range(1, R):
            imp = imp + jnp.dot(ovt, pb[:, r * Q:(r + 1) * Q], preferred_element_type=F32)
        score = jnp.where(forced, FORCE, jnp.where(jidx <= cur, imp, NEG))
        rank = jnp.zeros((ns, Q), F32)
        for jp in range(ns):
            row = score[jp:jp + 1, :]
            rank = rank + jnp.where(jidx > jp, jnp.where(row >= score, 1.0, 0.0), jnp.where(row > score, 1.0, 0.0))
        selneg_t = jnp.where(rank < SLC_TOPK, 0.0, NEG)
        selneg_t = jnp.concatenate([selneg_t, jnp.zeros((LANE - ns, Q), F32)], axis=0)
        selneg = selneg_t.T.astype(BF16)
        q_pads.append(q_pad)
        lhss.append(jnp.concatenate([q_pad, jnp.concatenate([selneg] * R, axis=0)], axis=1))

    def flash_step(s, vt_tile, carry):
        m, l, acc = carry
        m_new = jnp.maximum(m, jnp.max(s, axis=0, keepdims=True))
        alpha = jnp.exp(m - m_new)
        pt = jnp.exp(s - m_new)
        l = alpha * l + jnp.sum(pt, axis=0, keepdims=True)
        acc = alpha * acc + jnp.dot(vt_tile, pt.astype(BF16), preferred_element_type=F32)
        return m_new, l, acc

    init = (jnp.full((1, R * Q), NEG, F32), jnp.zeros((1, R * Q), F32), jnp.zeros((LANE, R * Q), F32))

    def slc_tile(t, carry):
        k0 = pl.multiple_of(t * SLC_TK, SLC_TK)
        kt = kvs_ref[pl.ds(k0, SLC_TK), :].astype(BF16)
        keys = jnp.concatenate([kt, et_ref[pl.ds(k0, SLC_TK), :]], axis=1)
        c0 = pl.multiple_of(jnp.maximum(SLC_C - (q0 - k0), 0), LANE)
        ch = t * (SLC_TK // LANE)
        vt = jnp.concatenate([vst_ref[ch + c] for c in range(SLC_TK // LANE)], axis=1)
        return tuple(flash_step(_dot_nt(keys, lhss[g]) + bslc_ref[g, pl.ds(c0, SLC_TK), :], vt, carry[g])
                     for g in range(G))

    def slc_body(tp, carry):
        for u in range(SLC_UNROLL):
            carry = slc_tile(tp * SLC_UNROLL + u, carry)
        return carry

    n_tiles = q0 // SLC_TK + 1
    slc = lax.fori_loop(0, (n_tiles + SLC_UNROLL - 1) // SLC_UNROLL, slc_body, (init,) * G)

    k0 = pl.multiple_of(jnp.maximum(q0 - WINDOW, 0), LANE)
    kt = kvw_ref[pl.ds(k0, WIN_KEYS), :].astype(BF16)
    c0 = pl.multiple_of(WINDOW - (q0 - k0), LANE)
    ch = k0 // LANE
    vt = jnp.concatenate([vwt_ref[ch + c] for c in range(WIN_KEYS // LANE)], axis=1)
    o_wins = []
    for g in range(G):
        s = _dot_nt(kt, q_pads[g]) + bwin_ref[g, pl.ds(c0, WIN_KEYS), :]
        e = jnp.exp(s - jnp.max(s, axis=0, keepdims=True))
        o_wins.append(jnp.dot(vt, e.astype(BF16), preferred_element_type=F32) / jnp.sum(e, axis=0, keepdims=True))

    gates_t = gate_ref[...].T
    grow = lax.broadcasted_iota(jnp.int32, (GATE_PAD, Q), 0)
    for g in range(G):
        def gate_row(branch):
            rows = [jnp.sum(jnp.where(grow == branch * N_HEADS + g * R + r, gates_t, 0.0), axis=0, keepdims=True)
                    for r in range(R)]
            return jnp.concatenate(rows, axis=1)

        o = (gate_row(0) * o_cmps[g] + gate_row(1) * (slc[g][2] / slc[g][1])
             + gate_row(2) * o_wins[g])
        o = o[g * HEAD_DIM:(g + 1) * HEAD_DIM]
        for pair in range(R // 2):
            two = jnp.concatenate([o[:, (2 * pair) * Q:(2 * pair + 1) * Q],
                                   o[:, (2 * pair + 1) * Q:(2 * pair + 2) * Q]], axis=0)
            c = (g * R // 2 + pair) * LANE
            o_ref[:, c:c + LANE] = two.T


def _prompt_tables(tbl_flat, T):
    nqb = T // Q_BLOCK
    nc = T // CMP_STRIDE
    qi = jnp.arange(Q_BLOCK, dtype=jnp.int32)
    q_pos = jnp.arange(nqb, dtype=jnp.int32)[:, None, None] * Q_BLOCK + qi[None, :, None]
    c_end = jnp.arange(nc, dtype=jnp.int32) * CMP_STRIDE + (CMP_BLOCK - 1)
    big = 1 << 30
    bcmp = _bias_tables(tbl_flat, jnp.swapaxes(q_pos - c_end[None, None, :], 1, 2), big, True)

    def toeplitz(c_off, nrows, hi):
        dist = qi[None, None, :] + c_off - jnp.arange(nrows, dtype=jnp.int32)[None, :, None]
        return _bias_tables(tbl_flat, dist, hi, True)[:, 0]

    bslc = toeplitz(SLC_C, SLC_C + SLC_UNROLL * SLC_TK, big)
    bwin = toeplitz(WINDOW, WINDOW + WIN_KEYS, WINDOW)
    keys = jnp.arange(T, dtype=jnp.int32)
    et = (keys[:, None] // SLC_BLOCK == jnp.arange(LANE, dtype=jnp.int32)[None, :]).astype(BF16)
    c_start = jnp.arange(nc, dtype=jnp.int32) * CMP_STRIDE
    s_start = jnp.arange(T // SLC_BLOCK, dtype=jnp.int32) * SLC_BLOCK
    ovt = ((c_start[None, :] < s_start[:, None] + SLC_BLOCK)
           & (c_start[None, :] + CMP_BLOCK > s_start[:, None])).astype(BF16)
    return bcmp, bslc, bwin, et, ovt


def _nsa_prompt_call(q, gates, kc, vct, kvs, vst, kvw, vwt, tables, B, T):
    bcmp, bslc, bwin, et, ovt = tables
    nqb = T // Q_BLOCK
    nc = T // CMP_STRIDE
    R = GQA_GROUP
    seq = pl.BlockSpec((T, KV_W), lambda b, i: (b, 0))
    seq_t = pl.BlockSpec((T // LANE, KV_W, LANE), lambda b, i: (b, 0, 0))
    return pl.pallas_call(
        _nsa_prompt_kernel,
        grid=(B, nqb),
        in_specs=[
            pl.BlockSpec((Q_BLOCK, NSA_W), lambda b, i: (b * nqb + i, 0)),
            pl.BlockSpec((Q_BLOCK, GATE_PAD), lambda b, i: (b * nqb + i, 0)),
            pl.BlockSpec((1, nc, KV_W), lambda b, i: (b, 0, 0)),
            pl.BlockSpec((1, KV_W, nc), lambda b, i: (b, 0, 0)),
            seq, seq_t, seq, seq_t,
            pl.BlockSpec((KV_HEADS, 1, nc, R * Q_BLOCK), lambda b, i: (0, i, 0, 0)),
            pl.BlockSpec(bslc.shape, lambda b, i: (0, 0, 0)),
            pl.BlockSpec(bwin.shape, lambda b, i: (0, 0, 0)),
            pl.BlockSpec(et.shape, lambda b, i: (0, 0)),
            pl.BlockSpec(ovt.shape, lambda b, i: (0, 0)),
        ],
        out_specs=pl.BlockSpec((Q_BLOCK, NSA_W), lambda b, i: (b * nqb + i, 0)),
        out_shape=jax.ShapeDtypeStruct((B * T, NSA_W), F32),
        compiler_params=pltpu.CompilerParams(dimension_semantics=("parallel", "arbitrary"),
                                             vmem_limit_bytes=VMEM_LIMIT),
        name="nsa_prompt",
    )(q, gates, kc, vct, kvs, vst, kvw, vwt, bcmp, bslc, bwin, et, ovt)


SUB = 8
PAGES_PER_STEP = 64
UNTRANSPOSE_UNROLL = 8
SAMPLE_BATCHES = 2


def _compress_pages_kernel(xt_ref, pea_ref, peb_ref, wa_ref, wb_ref, o_ref, xk_ref, xv_ref):
    pb, w, page = xt_ref.shape
    hw = w // 2
    ch = page // CMP_STRIDE

    def untranspose(i, carry):
        for u in range(UNTRANSPOSE_UNROLL):
            p = i * UNTRANSPOSE_UNROLL + u
            xk_ref[p] = xt_ref[p, :hw, :].T
            xv_ref[p] = xt_ref[p, hw:, :].T
        return carry

    lax.fori_loop(0, pb // UNTRANSPOSE_UNROLL, untranspose, 0)
    a = jnp.zeros((pb * ch, w), F32)
    b = jnp.zeros((pb * ch, w), F32)
    for t in range(CMP_STRIDE):
        x = jnp.concatenate([ref[:, pl.ds(t, ch, stride=CMP_STRIDE), :].reshape(pb * ch, hw)
                             for ref in (xk_ref, xv_ref)], axis=1)
        cols = slice(t * w, (t + 1) * w)
        a = a + jnp.dot((x + pea_ref[:, cols]).astype(BF16), wa_ref[cols, :], preferred_element_type=F32)
        b = b + jnp.dot((x + peb_ref[:, cols]).astype(BF16), wb_ref[cols, :], preferred_element_type=F32)
    o_ref[...] = jnp.concatenate([a, b], axis=1).reshape(pb, ch, 2 * w)


def _compress_pages(pages, layer, n_phys, cw):
    _, w, page = pages.shape
    ch = page // CMP_STRIDE
    steps = n_phys // PAGES_PER_STEP
    full = lambda a: pl.BlockSpec(a.shape, lambda i: (0, 0))
    return pl.pallas_call(
        _compress_pages_kernel,
        grid=(steps,),
        in_specs=[pl.BlockSpec((PAGES_PER_STEP, w, page), lambda i: (layer * steps + i, 0, 0))] + [full(a) for a in cw],
        out_specs=pl.BlockSpec((PAGES_PER_STEP, ch, 4 * KV_W), lambda i: (i, 0, 0)),
        out_shape=jax.ShapeDtypeStruct((n_phys, ch, 4 * KV_W), F32),
        scratch_shapes=[pltpu.VMEM((PAGES_PER_STEP, page, w // 2), F32)] * 2,
        compiler_params=pltpu.CompilerParams(dimension_semantics=("parallel",), vmem_limit_bytes=VMEM_LIMIT),
        name="compress_pages",
    )(pages, *cw)


def _nsa_sample_kernel(pt_ref, q_ref, gate_ref, ks_new_ref, kw_new_ref, *rest, n_pages, past, dec_seq):
    n = SAMPLE_BATCHES * n_pages
    ab_refs, slc_refs = rest[:n], rest[n:2 * n]
    win_ref, bcmp_ref, bslc_ref, bwin_ref, et_ref, ov_ref, o_ref, win_out_ref = rest[2 * n:]
    for bb in range(SAMPLE_BATCHES):
        one = lambda ref: ref.at[pl.ds(bb, 1)]
        pages = slice(bb * n_pages, (bb + 1) * n_pages)
        _nsa_sample_one(one(q_ref), one(gate_ref), one(ks_new_ref), one(kw_new_ref), ab_refs[pages], slc_refs[pages],
                        one(win_ref), bcmp_ref, bslc_ref, bwin_ref, et_ref, ov_ref, one(o_ref), one(win_out_ref),
                        past, dec_seq)


def _nsa_sample_one(q_ref, gate_ref, ks_new_ref, kw_new_ref, ab_refs, slc_refs, win_ref, bcmp_ref, bslc_ref, bwin_ref,
                    et_ref, ov_ref, o_ref, win_out_ref, past, dec_seq):
    R, G = GQA_GROUP, KV_HEADS
    rows = R * G * SUB
    half = lax.broadcasted_iota(jnp.int32, (SUB, LANE), 1) // HEAD_DIM

    qb = q_ref[0] * (HEAD_DIM ** -0.5)
    slabs = []
    for r in range(R):
        for g in range(G):
            h = g * R + r
            slab = qb[:, (h // 2) * LANE:(h // 2 + 1) * LANE]
            slabs.append(jnp.where(half == g, _half_select(slab, h % 2, half), 0.0))
    q_pad = jnp.concatenate(slabs, axis=0).astype(BF16)

    ab = jnp.concatenate([r_[0] for r_ in ab_refs], axis=0)
    nc = ab.shape[0]
    kcvc = ab[:, :2 * KV_W] + pltpu.roll(ab[:, 2 * KV_W:], nc - 1, axis=0)
    kc = kcvc[:, :KV_W].astype(BF16)
    vc = kcvc[:, KV_W:].astype(BF16)
    s = _dot_nt(q_pad, kc) + bcmp_ref[...]
    e = jnp.exp(s - jnp.max(s, axis=-1, keepdims=True))
    p = e / jnp.sum(e, axis=-1, keepdims=True)
    row_pos = past + lax.broadcasted_iota(jnp.int32, (rows, 1), 0) % SUB
    p = jnp.where(row_pos >= CMP_BLOCK - 1, p, 0.0)
    pb = p.astype(BF16)
    o_cmp = jnp.dot(pb, vc, preferred_element_type=F32)

    ov = ov_ref[...]
    imp = jnp.dot(pb[:G * SUB], ov, preferred_element_type=F32)
    for r in range(1, R):
        imp = imp + jnp.dot(pb[r * G * SUB:(r + 1) * G * SUB], ov, preferred_element_type=F32)
    jidx = lax.broadcasted_iota(jnp.int32, (G * SUB, LANE), 1)
    cur = (past + lax.broadcasted_iota(jnp.int32, (G * SUB, LANE), 0) % SUB) // SLC_BLOCK
    forced = (jidx == 0) | (jidx == cur) | (jidx == cur - 1)
    score = jnp.where(forced, FORCE, jnp.where(jidx <= cur, imp, NEG))
    n_blocks = (past + dec_seq + SLC_BLOCK - 1) // SLC_BLOCK
    rank = jnp.zeros((G * SUB, LANE), F32)
    for jp in range(n_blocks):
        col = score[:, jp:jp + 1]
        rank = rank + jnp.where(jidx > jp, jnp.where(col >= score, 1.0, 0.0), jnp.where(col > score, 1.0, 0.0))
    selneg = jnp.where(rank < min(SLC_TOPK, n_blocks), 0.0, NEG).astype(BF16)
    lhs = jnp.concatenate([q_pad, jnp.concatenate([selneg] * R, axis=0)], axis=1)

    def new_cols(ref):
        x = ref[0]
        xt = jnp.concatenate([x, jnp.zeros((LANE - SUB, 2 * KV_W), F32)], axis=0).T
        return xt, xt[:KV_W].astype(BF16), xt[KV_W:].astype(BF16)

    def attend(lhs_, kt_tiles, vt_tiles, bias, ext):
        parts = []
        for t, kt in enumerate(kt_tiles):
            rhs = jnp.concatenate([kt, et_ref[:, t * LANE:(t + 1) * LANE]], axis=0) if ext else kt
            parts.append(jnp.dot(lhs_, rhs, preferred_element_type=F32))
        s_ = jnp.concatenate(parts, axis=1) + bias
        e_ = jnp.exp(s_ - jnp.max(s_, axis=-1, keepdims=True))
        p_ = (e_ / jnp.sum(e_, axis=-1, keepdims=True)).astype(BF16)
        o_ = _dot_nt(p_[:, :LANE], vt_tiles[0])
        for t in range(1, len(vt_tiles)):
            o_ = o_ + _dot_nt(p_[:, t * LANE:(t + 1) * LANE], vt_tiles[t])
        return o_

    _, kt_new, vt_new = new_cols(ks_new_ref)
    kt_tiles = [r_[0, :KV_W, :].astype(BF16) for r_ in slc_refs] + [kt_new]
    vt_tiles = [r_[0, KV_W:, :].astype(BF16) for r_ in slc_refs] + [vt_new]
    o_slc = attend(lhs, kt_tiles, vt_tiles, bslc_ref[...], True)

    win_new_t, kt_new, vt_new = new_cols(kw_new_ref)
    wb = win_ref.shape[2]
    kt_tiles = [win_ref[0, :KV_W, t * LANE:(t + 1) * LANE].astype(BF16) for t in range(wb // LANE)] + [kt_new]
    vt_tiles = [win_ref[0, KV_W:, t * LANE:(t + 1) * LANE].astype(BF16) for t in range(wb // LANE)] + [vt_new]
    o_win = attend(q_pad, kt_tiles, vt_tiles, bwin_ref[...], False)

    gates = gate_ref[0]
    glane = lax.broadcasted_iota(jnp.int32, (SUB, LANE), 1)

    def gate_col(branch):
        cols = [jnp.sum(jnp.where(glane == branch * N_HEADS + g * R + r, gates, 0.0), axis=-1, keepdims=True)
                for r in range(R) for g in range(G)]
        return jnp.concatenate(cols, axis=0)

    o = gate_col(0) * o_cmp + gate_col(1) * o_slc + gate_col(2) * o_win
    for g in range(G):
        for pair in range(R // 2):
            r0 = ((2 * pair) * G + g) * SUB
            r1 = ((2 * pair + 1) * G + g) * SUB
            lo = _half_select(o[r0:r0 + SUB], g, half)
            hi = _half_select(o[r1:r1 + SUB], g, half)
            c = (g * R // 2 + pair) * LANE
            o_ref[0, :, c:c + LANE] = jnp.where(half == 0, lo, hi)

    shifted = pltpu.roll(win_ref[0], wb - dec_seq, axis=1)
    win_out_ref[0, :, :wb - LANE] = shifted[:, :wb - LANE]
    klane = lax.broadcasted_iota(jnp.int32, (2 * KV_W, LANE), 1)
    win_out_ref[0, :, wb - LANE:] = jnp.where(klane >= LANE - dec_seq, pltpu.roll(win_new_t, LANE - dec_seq, axis=1),
                                              shifted[:, wb - LANE:])


def _sample_tables(tbl_flat, past, dec_seq, wb):
    q_pos = past + jnp.arange(SUB, dtype=jnp.int32)
    nc = past // CMP_STRIDE
    c_end = jnp.arange(nc, dtype=jnp.int32) * CMP_STRIDE + (CMP_BLOCK - 1)
    slc_keys = jnp.arange(past + LANE, dtype=jnp.int32)
    win_pos = jnp.concatenate([past - wb + jnp.arange(wb, dtype=jnp.int32), past + jnp.arange(LANE, dtype=jnp.int32)])
    big = 1 << 30

    def table(key_pos, hi):
        t = _bias_tables(tbl_flat, (q_pos[:, None] - key_pos[None, :])[None], hi)
        c = t.shape[-1]
        return t.reshape(KV_HEADS, GQA_GROUP, SUB, c).transpose(1, 0, 2, 3).reshape(GQA_GROUP * KV_HEADS * SUB, c)

    bcmp = table(c_end, big)
    bslc = table(slc_keys, big)
    bwin = table(win_pos, WINDOW)
    def kill_pad(b):
        lane = jnp.arange(b.shape[1]) - (b.shape[1] - LANE)
        return jnp.where((lane >= dec_seq)[None, :], NEG, b)
    bslc, bwin = kill_pad(bslc), kill_pad(bwin)
    et = (slc_keys[None, :] // SLC_BLOCK == jnp.arange(LANE, dtype=jnp.int32)[:, None]).astype(BF16)
    c_start = jnp.arange(nc, dtype=jnp.int32) * CMP_STRIDE
    s_start = jnp.arange(LANE, dtype=jnp.int32) * SLC_BLOCK
    ov = ((c_start[:, None] < s_start[None, :] + SLC_BLOCK)
          & (c_start[:, None] + CMP_BLOCK > s_start[None, :])).astype(BF16)
    return bcmp, bslc, bwin, et, ov


def _nsa_sample_call(page_table_flat, q, gates, ks_new, kw_new, ab_pages, slc_pages, win_buf, tables, layer, n_phys,
                     dec_seq):
    bcmp, bslc, bwin, et, ov = tables
    B = q.shape[0]
    n_pages = page_table_flat.shape[0] // B
    page = slc_pages.shape[2]
    past = n_pages * page
    wb = win_buf.shape[2]
    nb = SAMPLE_BATCHES
    row3 = lambda a: pl.BlockSpec((nb,) + a.shape[1:], lambda b, pt: (b, 0, 0))
    full = lambda a: pl.BlockSpec(a.shape, lambda b, pt: (0, 0))
    slots = [(bb, p) for bb in range(nb) for p in range(n_pages)]
    ab_specs = [pl.BlockSpec((1,) + ab_pages.shape[1:],
                             lambda b, pt, bb=bb, p=p: (pt[(b * nb + bb) * n_pages + p], 0, 0)) for bb, p in slots]
    slc_specs = [pl.BlockSpec((1,) + slc_pages.shape[1:],
                              lambda b, pt, bb=bb, p=p: (layer * n_phys + pt[(b * nb + bb) * n_pages + p], 0, 0))
                 for bb, p in slots]
    return pl.pallas_call(
        functools.partial(_nsa_sample_kernel, n_pages=n_pages, past=past, dec_seq=dec_seq),
        grid_spec=pltpu.PrefetchScalarGridSpec(
            num_scalar_prefetch=1,
            grid=(B // nb,),
            in_specs=[row3(q), row3(gates), row3(ks_new), row3(kw_new)] + ab_specs + slc_specs
                     + [pl.BlockSpec((nb, 2 * KV_W, wb), lambda b, pt: (layer * (B // nb) + b, 0, 0)),
                        full(bcmp), full(bslc), full(bwin), full(et), full(ov)],
            out_specs=[pl.BlockSpec((nb, SUB, NSA_W), lambda b, pt: (b, 0, 0)),
                       pl.BlockSpec((nb, 2 * KV_W, wb), lambda b, pt: (b, 0, 0))],
        ),
        out_shape=[jax.ShapeDtypeStruct((B, SUB, NSA_W), F32), jax.ShapeDtypeStruct((B, 2 * KV_W, wb), F32)],
        compiler_params=pltpu.CompilerParams(dimension_semantics=("parallel",), vmem_limit_bytes=VMEM_LIMIT),
        name="nsa_sample",
    )(page_table_flat, q, gates, ks_new, kw_new, *([ab_pages] * len(slots)), *([slc_pages] * len(slots)), win_buf,
      bcmp, bslc, bwin, et, ov)


CONV_HALO = 32
POOL_HALO = 16


def _mix_kernel(glu_ref, glu_halo_ref, glu_hist_ref, u_ref, u_halo_ref, u_hist_ref, dw_ref, cb_ref, lng_ref, lnb_ref,
                wp_ref, ps_ref, conv_ref, pool_ref, xs_ref, us_ref, *, pos0):
    j = pl.program_id(1)
    tq = glu_ref.shape[0]

    @pl.when(j == 0)
    def _():
        xs_ref[:CONV_HALO, :] = glu_hist_ref[0]
        us_ref[:POOL_HALO, :] = u_hist_ref[0]

    @pl.when(j > 0)
    def _():
        xs_ref[:CONV_HALO, :] = glu_halo_ref[...]
        us_ref[:POOL_HALO, :] = u_halo_ref[...]

    xs_ref[CONV_HALO:, :] = glu_ref[...]
    us_ref[POOL_HALO:, :] = u_ref[...]

    base = CONV_HALO - CONV_BUF
    y = xs_ref[base:base + tq, :] * dw_ref[0:1, :]
    for k in range(1, CONV_WIDTH):
        y = y + xs_ref[base + k:base + k + tq, :] * dw_ref[k:k + 1, :]
    y = y + cb_ref[...]
    mu = jnp.mean(y, axis=-1, keepdims=True)
    yc = y - mu
    var = jnp.mean(yc * yc, axis=-1, keepdims=True)
    conv_ref[...] = jax.nn.silu(yc * lax.rsqrt(var + EPS) * lng_ref[...] + lnb_ref[...])

    u = u_ref[...]
    pos = pos0 + j * tq + lax.broadcasted_iota(jnp.int32, (tq, 1), 0)
    grp = lax.broadcasted_iota(jnp.int32, (tq, POOL_W), 1) // POOL_GROUP_W
    run = u
    d = jnp.zeros_like(u)
    done = 1
    for gi, w in enumerate(POOL_WINDOWS):
        for back in range(done, w):
            run = run + us_ref[POOL_HALO - back:POOL_HALO - back + tq, :]
        done = w
        cnt = jnp.minimum(w, pos + 1).astype(F32)
        d = jnp.where(grp == gi, run / cnt - u, d)
    yp = jnp.dot(d.astype(BF16), wp_ref[...], preferred_element_type=F32)
    pool_ref[...] = yp * ps_ref[...]


def _mix(glu, u, glu_hist, u_hist, dw, cb, lng, lnb, wp_bd, ps, B, T, tq, pos0):
    nt = T // tq
    row = pl.BlockSpec((tq, CONV_W), lambda b, j: (b * nt + j, 0))
    full = lambda a: pl.BlockSpec(a.shape, lambda b, j: (0, 0))

    def halo(h):
        return pl.BlockSpec((h, CONV_W), lambda b, j: (jnp.maximum(b * (T // h) + j * (tq // h) - 1, 0), 0))

    hist = lambda h: pl.BlockSpec((1, h, CONV_W), lambda b, j: (b, 0, 0))
    if T < CONV_HALO:
        glu_halo_arr, u_halo_arr = glu_hist.reshape(-1, CONV_W), u_hist.reshape(-1, POOL_W)
        halo_c = pl.BlockSpec((CONV_HALO, CONV_W), lambda b, j: (0, 0))
        halo_p = pl.BlockSpec((POOL_HALO, POOL_W), lambda b, j: (0, 0))
    else:
        glu_halo_arr, u_halo_arr = glu, u
        halo_c, halo_p = halo(CONV_HALO), halo(POOL_HALO)
    return pl.pallas_call(
        functools.partial(_mix_kernel, pos0=pos0),
        grid=(B, nt),
        in_specs=[row, halo_c, hist(CONV_HALO), row, halo_p, hist(POOL_HALO),
                  full(dw), full(cb), full(lng), full(lnb), full(wp_bd), full(ps)],
        out_specs=[row, row],
        out_shape=[jax.ShapeDtypeStruct((B * T, CONV_W), F32), jax.ShapeDtypeStruct((B * T, POOL_W), F32)],
        scratch_shapes=[pltpu.VMEM((CONV_HALO + tq, CONV_W), F32), pltpu.VMEM((POOL_HALO + tq, POOL_W), F32)],
        compiler_params=pltpu.CompilerParams(dimension_semantics=("parallel", "parallel"),
                                             vmem_limit_bytes=VMEM_LIMIT),
        name="conv_pool",
    )(glu, glu_halo_arr, glu_hist, u, u_halo_arr, u_hist, dw, cb, lng, lnb, wp_bd, ps)


def _layer(x2d, B, T, l, P, glu_hist, u_hist, pos0, nsa_fn, final, emit_vt):
    n = B * T
    glu, u, q, kvc, kvs, kvw, gates, *vts = _inproj(x2d, P['norm1'][l], P['w_in'][l], min(n, 512), emit_vt)
    conv_o, pool_o = _mix(glu, u, glu_hist, u_hist, P['conv_dw'][l], P['conv_b'][l], P['conv_ln_g'][l],
                          P['conv_ln_b'][l], P['pool_w'][l], P['pool_scale'][l], B, T, min(T, 512), pos0)
    nsa_o, extra = nsa_fn(q, gates, kvc, kvs, kvw, *vts)
    y = _ffn(x2d, conv_o, pool_o, nsa_o, P['w_out'][l], P['norm2'][l], P['w_gu'][l], P['w_down'][l],
             P['final_norm'], min(n, 512), D_FF, final)
    return y, glu, u, kvc, kvs, kvw, extra


def kernel(x_prompt, x_sample, cache_cmp_kv, cache_slc_kv, cache_win_kv, state_conv, state_pool, page_table, rel_bias, norm1, w_in, conv_dw, conv_b, conv_ln_g, conv_ln_b, pool_w, pool_scale, cmp_pe_k, cmp_wk, cmp_pe_v, cmp_wv, w_out, norm2, w_gu, w_down, final_norm):
    depth = w_in.shape[0]
    Bp, T, _ = x_prompt.shape
    Bs, S, _ = x_sample.shape
    n_phys, page = cache_cmp_kv.shape[1:3]
    n_pages = page_table.shape[1]
    past = n_pages * page
    wb = cache_win_kv.shape[2]
    tbl_flat = rel_bias.reshape(NUM_BUCKETS, KV_HEADS, GQA_GROUP).transpose(1, 2, 0).reshape(-1)
    eye = jnp.eye(len(POOL_WINDOWS), dtype=F32)
    row = lambda a: a[:, None, :]
    P = dict(
        norm1=row(norm1), norm2=row(norm2), final_norm=final_norm[None, :],
        w_in=jnp.pad(w_in, ((0, 0), (0, 0), (0, IN_W_PAD - IN_W))).astype(BF16),
        w_out=w_out.astype(BF16), w_gu=w_gu.astype(BF16), w_down=w_down.astype(BF16),
        conv_dw=conv_dw, conv_b=row(conv_b), conv_ln_g=row(conv_ln_g), conv_ln_b=row(conv_ln_b),
        pool_w=jnp.einsum('lgce,gh->lgche', pool_w, eye).reshape(depth, POOL_W, POOL_W).astype(BF16),
        pool_scale=row(pool_scale))
    ptables = _prompt_tables(tbl_flat, T)
    stables = _sample_tables(tbl_flat, past, S, wb)
    cmp_pages = jnp.swapaxes(cache_cmp_kv.reshape(depth * n_phys, page, 2 * KV_W), 1, 2)
    slc_pages = jnp.swapaxes(cache_slc_kv.reshape(depth * n_phys, page, 2 * KV_W), 1, 2)
    win_bufs = jnp.swapaxes(cache_win_kv.reshape(depth * Bs, wb, 2 * KV_W), 1, 2)
    pt_flat = page_table.reshape(-1)

    xp = x_prompt.reshape(Bp * T, D_MODEL)
    xs = jnp.pad(x_sample, ((0, 0), (0, SUB - S), (0, 0))).reshape(Bs * SUB, D_MODEL)
    zc = jnp.zeros((Bp, CONV_HALO, CONV_W), F32)
    zp = jnp.zeros((Bp, POOL_HALO, POOL_W), F32)
    kv5 = lambda a, b, t: a.reshape(b, t, 2, KV_HEADS, HEAD_DIM)
    outs_p, outs_s = [], []
    for l in range(depth):
        final = l == depth - 1
        cw = _compress_weights(cmp_pe_k[l], cmp_wk[l], cmp_pe_v[l], cmp_wv[l])

        def nsa_p(q, gates, kvc, kvs, kvw, vst, vwt):
            kc, vct = _compress_call(kvc.reshape(Bp, T // CMP_STRIDE, CMP_STRIDE * 2 * KV_W), cw)
            return _nsa_prompt_call(q, gates, kc, vct, kvs, vst, kvw, vwt, ptables, Bp, T), None

        xp, glu, u, kvc, kvs, kvw, _ = _layer(xp, Bp, T, l, P, zc, zp, 0, nsa_p, final, True)
        outs_p.append((kv5(kvc, Bp, T), kv5(kvs, Bp, T), kv5(kvw, Bp, T)[:, T - min(WINDOW, T):],
                       glu.reshape(Bp, T, CONV_W)[:, T - CONV_BUF:], u.reshape(Bp, T, POOL_W)[:, T - POOL_BUF:]))

        def nsa_s(q, gates, kvc, kvs, kvw):
            ab = _compress_pages(cmp_pages, l, n_phys, cw)
            r3 = lambda a: a.reshape(Bs, SUB, a.shape[-1])
            o, new_win = _nsa_sample_call(pt_flat, r3(q), r3(gates), r3(kvs), r3(kvw), ab, slc_pages, win_bufs,
                                          stables, l, n_phys, S)
            return o.reshape(Bs * SUB, NSA_W), new_win

        glu_hist = jnp.pad(state_conv[l], ((0, 0), (CONV_HALO - CONV_BUF, 0), (0, 0)))
        u_hist = jnp.pad(state_pool[l], ((0, 0), (POOL_HALO - POOL_BUF, 0), (0, 0)))
        xs, glu, u, kvc, kvs, kvw, new_win = _layer(xs, Bs, SUB, l, P, glu_hist, u_hist, past, nsa_s, final, False)
        new_conv = jnp.concatenate([state_conv[l], glu.reshape(Bs, SUB, CONV_W)[:, :S]], axis=1)[:, -CONV_BUF:]
        new_pool = jnp.concatenate([state_pool[l], u.reshape(Bs, SUB, POOL_W)[:, :S]], axis=1)[:, -POOL_BUF:]
        outs_s.append((kv5(kvc, Bs, SUB)[:, :S], kv5(kvs, Bs, SUB)[:, :S], kv5(jnp.swapaxes(new_win, 1, 2), Bs, wb), new_conv, new_pool))
    st = lambda outs, i: jnp.stack([o[i] for o in outs])
    return (xp.reshape(Bp, T, D_MODEL), xs.reshape(Bs, SUB, D_MODEL)[:, :S],
            st(outs_p, 0), st(outs_p, 1), st(outs_p, 2), st(outs_p, 3), st(outs_p, 4),
            st(outs_s, 0), st(outs_s, 1), st(outs_s, 2), st(outs_s, 3), st(outs_s, 4))
```

```python
import functools
import math

import jax
import jax.numpy as jnp
from jax import lax
from jax.experimental import pallas as pl
from jax.experimental.pallas import tpu as pltpu

D_MODEL = 1024
CONV_W = 256
POOL_W = 256
N_HEADS = 8
HEAD_DIM = 64
NSA_W = N_HEADS * HEAD_DIM
KV_HEADS = 2
GQA_GROUP = N_HEADS // KV_HEADS
KV_W = KV_HEADS * HEAD_DIM
CONV_WIDTH = 31
CONV_BUF = CONV_WIDTH - 1
POOL_WINDOWS = (2, 4, 8, 16)
POOL_GROUP_W = POOL_W // len(POOL_WINDOWS)
POOL_BUF = max(POOL_WINDOWS) - 1
CMP_STRIDE = 16
CMP_BLOCK = 2 * CMP_STRIDE
SLC_BLOCK = 64
SLC_TOPK = 16
WINDOW = 512
Q_BLOCK = 128
NUM_BUCKETS = 32
MAX_DISTANCE = 128
D_FF = 2816
EPS = 1e-6
NEG = -1e30
FORCE = 1e4

OFF_POOL = 2 * CONV_W
OFF_Q = OFF_POOL + POOL_W
OFF_KV = OFF_Q + NSA_W
OFF_GATE = OFF_KV + 6 * KV_W
IN_W = OFF_GATE + 3 * N_HEADS
LANE = 128
IN_W_PAD = -(-IN_W // LANE) * LANE
GATE_PAD = IN_W_PAD - OFF_GATE

VMEM_LIMIT = 56 * 1024 * 1024

F32 = jnp.float32
BF16 = jnp.bfloat16


def _rms(x, g):
    return x * lax.rsqrt(jnp.mean(x * x, axis=-1, keepdims=True) + EPS) * g


def _inproj_kernel(x_ref, g_ref, w_ref, glu_ref, u_ref, q_ref, kvc_ref, kvs_ref, kvw_ref, gate_ref, *vt_refs):
    h = _rms(x_ref[...], g_ref[...]).astype(BF16)
    z = jnp.dot(h, w_ref[...], preferred_element_type=F32)
    glu_ref[...] = z[:, :CONV_W] * jax.nn.sigmoid(z[:, CONV_W:OFF_POOL])
    u_ref[...] = z[:, OFF_POOL:OFF_Q]
    q_ref[...] = z[:, OFF_Q:OFF_KV]
    kvc_ref[...] = z[:, OFF_KV:OFF_KV + 2 * KV_W]
    kvs_ref[...] = z[:, OFF_KV + 2 * KV_W:OFF_KV + 4 * KV_W]
    kvw_ref[...] = z[:, OFF_KV + 4 * KV_W:OFF_GATE]
    gate_ref[...] = jax.nn.sigmoid(z[:, OFF_GATE:])
    if vt_refs:
        for vt_ref, off in zip(vt_refs[:2], (OFF_KV + 3 * KV_W, OFF_KV + 5 * KV_W)):
            for c in range(vt_ref.shape[0]):
                vt_ref[c] = z[c * LANE:(c + 1) * LANE, off:off + KV_W].T.astype(BF16)
        for b, kvt_ref in enumerate(vt_refs[2:]):
            kvt_ref[0] = z[:, OFF_KV + 2 * b * KV_W:OFF_KV + 2 * (b + 1) * KV_W].T


def _inproj(x2d, g, w_pad, tm, seq_len, emit_vt):
    n = x2d.shape[0]
    nt = max(seq_len // tm, 1)
    row = lambda w: pl.BlockSpec((tm, w), lambda i: (i, 0))
    full = lambda a: pl.BlockSpec(a.shape, lambda i: (0, 0))
    widths = (CONV_W, POOL_W, NSA_W, 2 * KV_W, 2 * KV_W, 2 * KV_W, GATE_PAD)
    out_specs = [row(w) for w in widths]
    out_shape = [jax.ShapeDtypeStruct((n, w), F32) for w in widths]
    if emit_vt:
        out_specs += [pl.BlockSpec((tm // LANE, KV_W, LANE), lambda i: (i, 0, 0))] * 2
        out_shape += [jax.ShapeDtypeStruct((n // LANE, KV_W, LANE), BF16)] * 2
        out_specs += [pl.BlockSpec((1, 2 * KV_W, tm), lambda i: (i // nt, 0, i % nt))] * 3
        out_shape += [jax.ShapeDtypeStruct((n // seq_len, 2 * KV_W, seq_len), F32)] * 3
    return pl.pallas_call(
        _inproj_kernel,
        grid=(n // tm,),
        in_specs=[row(D_MODEL), full(g), full(w_pad)],
        out_specs=out_specs,
        out_shape=out_shape,
        compiler_params=pltpu.CompilerParams(dimension_semantics=("parallel",), vmem_limit_bytes=VMEM_LIMIT),
        name="inproj",
    )(x2d, g, w_pad)


def _ffn_kernel(x_ref, conv_ref, pool_ref, nsa_ref, wo_ref, g2_ref, wg_ref, wu_ref, wd_ref, gf_ref,
                y_ref, x1_ref, h2_ref, acc_ref, *, final_norm):
    j = pl.program_id(1)

    @pl.when(j == 0)
    def _():
        mix = jnp.concatenate([conv_ref[...], pool_ref[...], nsa_ref[...]], axis=-1).astype(BF16)
        x1 = x_ref[...] + jnp.dot(mix, wo_ref[...], preferred_element_type=F32)
        x1_ref[...] = x1
        h2_ref[...] = _rms(x1, g2_ref[...]).astype(BF16)
        acc_ref[...] = jnp.zeros_like(acc_ref)

    h2 = h2_ref[...]
    gate = jnp.dot(h2, wg_ref[...], preferred_element_type=F32)
    up = jnp.dot(h2, wu_ref[...], preferred_element_type=F32)
    act = (jax.nn.silu(gate) * up).astype(BF16)
    acc_ref[...] += jnp.dot(act, wd_ref[...], preferred_element_type=F32)

    @pl.when(j == pl.num_programs(1) - 1)
    def _():
        y = x1_ref[...] + acc_ref[...]
        if final_norm:
            y = _rms(y, gf_ref[...])
        y_ref[...] = y


def _ffn(x2d, conv_o, pool_o, nsa_o, w_out, g2, w_gu, w_down, gf, tm, tf, final_norm):
    n = x2d.shape[0]
    nf = D_FF // tf
    row = lambda w: pl.BlockSpec((tm, w), lambda i, j: (i, 0))
    full = lambda a: pl.BlockSpec(a.shape, lambda i, j: (0, 0))
    return pl.pallas_call(
        functools.partial(_ffn_kernel, final_norm=final_norm),
        grid=(n // tm, nf),
        in_specs=[row(D_MODEL), row(CONV_W), row(POOL_W), row(NSA_W), full(w_out), full(g2),
                  pl.BlockSpec((D_MODEL, tf), lambda i, j: (0, j)),
                  pl.BlockSpec((D_MODEL, tf), lambda i, j: (0, j + nf)),
                  pl.BlockSpec((tf, D_MODEL), lambda i, j: (j, 0)),
                  full(gf)],
        out_specs=row(D_MODEL),
        out_shape=jax.ShapeDtypeStruct((n, D_MODEL), F32),
        scratch_shapes=[pltpu.VMEM((tm, D_MODEL), F32), pltpu.VMEM((tm, D_MODEL), BF16),
                        pltpu.VMEM((tm, D_MODEL), F32)],
        compiler_params=pltpu.CompilerParams(dimension_semantics=("parallel", "arbitrary"),
                                             vmem_limit_bytes=VMEM_LIMIT),
        name="ffn",
    )(x2d, conv_o, pool_o, nsa_o, w_out, g2, w_gu, w_gu, w_down, gf)


def _bias_kernel(tbl_ref, dist_ref, out_ref, *, hi, heads_on_lanes):
    g = pl.program_id(0)
    d = dist_ref[0]
    rows, cols = d.shape
    n = jnp.maximum(d, 0)
    max_exact = NUM_BUCKETS // 2
    nf = jnp.maximum(n, 1).astype(F32)
    large = max_exact + (jnp.log(nf / max_exact) / math.log(MAX_DISTANCE / max_exact)
                         * (NUM_BUCKETS - max_exact)).astype(jnp.int32)
    large = jnp.minimum(large, NUM_BUCKETS - 1)
    bucket = jnp.where(n < max_exact, n, large)
    valid = (d >= 0) & (d < hi)
    for r in range(GQA_GROUP):
        acc = jnp.zeros(d.shape, F32)
        for b in range(NUM_BUCKETS):
            acc = jnp.where(bucket == b, tbl_ref[(g * GQA_GROUP + r) * NUM_BUCKETS + b], acc)
        val = jnp.where(valid, acc, NEG)
        if heads_on_lanes:
            out_ref[0, 0, :, r * cols:(r + 1) * cols] = val
        else:
            out_ref[0, 0, r * rows:(r + 1) * rows, :] = val


def _bias_tables(tbl_flat, dist, hi, heads_on_lanes=False):
    nblk, R, C = dist.shape
    oshape = (R, GQA_GROUP * C) if heads_on_lanes else (GQA_GROUP * R, C)
    return pl.pallas_call(
        functools.partial(_bias_kernel, hi=hi, heads_on_lanes=heads_on_lanes),
        grid_spec=pltpu.PrefetchScalarGridSpec(
            num_scalar_prefetch=1,
            grid=(KV_HEADS, nblk),
            in_specs=[pl.BlockSpec((1, R, C), lambda g, i, tbl: (i, 0, 0))],
            out_specs=pl.BlockSpec((1, 1) + oshape, lambda g, i, tbl: (g, i, 0, 0)),
        ),
        out_shape=jax.ShapeDtypeStruct((KV_HEADS, nblk) + oshape, F32),
        compiler_params=pltpu.CompilerParams(dimension_semantics=("parallel", "parallel")),
        name="bias_tables",
    )(tbl_flat, dist)


def _compress_kernel(x_ref, pea_ref, peb_ref, wa_ref, wb_ref, k_ref, vt_ref):
    x = x_ref[0]
    a = jnp.dot((x + pea_ref[...]).astype(BF16), wa_ref[...], preferred_element_type=F32)
    b = jnp.dot((x + peb_ref[...]).astype(BF16), wb_ref[...], preferred_element_type=F32)
    kv = a + pltpu.roll(b, b.shape[0] - 1, axis=0)
    k_ref[0] = kv[:, :KV_W].astype(BF16)
    vt_ref[0] = kv[:, KV_W:].T.astype(BF16)


def _compress_weights(pe_k, wk, pe_v, wv):
    eye = jnp.eye(2 * KV_HEADS, dtype=F32)

    def half(lo):
        w = jnp.stack([wk[lo:lo + CMP_STRIDE], wk[lo:lo + CMP_STRIDE], wv[lo:lo + CMP_STRIDE], wv[lo:lo + CMP_STRIDE]], 1)
        big = jnp.einsum('lcde,cf->lcdfe', w, eye).reshape(CMP_STRIDE * 2 * KV_W, 2 * KV_W)
        pe = jnp.stack([pe_k[lo:lo + CMP_STRIDE], pe_k[lo:lo + CMP_STRIDE], pe_v[lo:lo + CMP_STRIDE], pe_v[lo:lo + CMP_STRIDE]], 1)
        return pe.reshape(1, CMP_STRIDE * 2 * KV_W), big.astype(BF16)

    pea, wa = half(0)
    peb, wb = half(CMP_STRIDE)
    return pea, peb, wa, wb


def _compress_call(x3, cw):
    B, nc, cw_in = x3.shape
    full = lambda a: pl.BlockSpec(a.shape, lambda b: (0, 0))
    return pl.pallas_call(
        _compress_kernel,
        grid=(B,),
        in_specs=[pl.BlockSpec((1, nc, cw_in), lambda b: (b, 0, 0))] + [full(a) for a in cw],
        out_specs=[pl.BlockSpec((1, nc, KV_W), lambda b: (b, 0, 0)), pl.BlockSpec((1, KV_W, nc), lambda b: (b, 0, 0))],
        out_shape=[jax.ShapeDtypeStruct((B, nc, KV_W), BF16), jax.ShapeDtypeStruct((B, KV_W, nc), BF16)],
        compiler_params=pltpu.CompilerParams(dimension_semantics=("parallel",), vmem_limit_bytes=VMEM_LIMIT),
        name="compress",
    )(x3, *cw)


SLC_TK = 1024
SLC_C = SLC_TK + LANE
SLC_UNROLL = 1
WIN_KEYS = WINDOW + Q_BLOCK


def _dot_nt(a, b):
    return lax.dot_general(a, b, (((1,), (1,)), ((), ())), preferred_element_type=F32)


def _half_select(slab, want_half, half):
    return jnp.where(half == want_half, slab, pltpu.roll(slab, HEAD_DIM, axis=1))


def _nsa_prompt_kernel(q_ref, gate_ref, kc_ref, vct_ref, kvs_ref, vst_ref, kvw_ref, vwt_ref, bcmp_ref, bslc_ref,
                       bwin_ref, et_ref, ovt_ref, o_ref):
    i = pl.program_id(1)
    Q = Q_BLOCK
    R = GQA_GROUP
    G = KV_HEADS
    q0 = i * Q
    half = lax.broadcasted_iota(jnp.int32, (Q, LANE), 1) // HEAD_DIM
    qb = q_ref[...] * (HEAD_DIM ** -0.5)
    ovt = ovt_ref[...]
    ns = ovt.shape[0]
    jidx = lax.broadcasted_iota(jnp.int32, (ns, Q), 0)
    cur = (q0 + lax.broadcasted_iota(jnp.int32, (ns, Q), 1)) // SLC_BLOCK
    forced = (jidx == 0) | (jidx == cur) | (jidx == cur - 1)
    col_pos = q0 + lax.broadcasted_iota(jnp.int32, (1, R * Q), 1) % Q

    q_pads, lhss, o_cmps = [], [], []
    for g in range(G):
        slabs = []
        for r in range(R):
            h = g * R + r
            slab = qb[:, (h // 2) * LANE:(h // 2 + 1) * LANE]
            slabs.append(jnp.where(half == g, _half_select(slab, h % 2, half), 0.0))
        q_pad = jnp.concatenate(slabs, axis=0).astype(BF16)

        s = _dot_nt(kc_ref[0], q_pad) + bcmp_ref[g, 0]
        e = jnp.exp(s - jnp.max(s, axis=0, keepdims=True))
        p = e / jnp.sum(e, axis=0, keepdims=True)
        p = jnp.where(col_pos >= CMP_BLOCK - 1, p, 0.0)
        pb = p.astype(BF16)
        o_cmps.append(jnp.dot(vct_ref[0], pb, preferred_element_type=F32))

        imp = jnp.dot(ovt, pb[:, :Q], preferred_element_type=F32)
        for r in range(1, R):
            imp = imp + jnp.dot(ovt, pb[:, r * Q:(r + 1) * Q], preferred_element_type=F32)
        score = jnp.where(forced, FORCE, jnp.where(jidx <= cur, imp, NEG))
        rank = jnp.zeros((ns, Q), F32)
        for jp in range(ns):
            row = score[jp:jp + 1, :]
            rank = rank + jnp.where(jidx > jp, jnp.where(row >= score, 1.0, 0.0), jnp.where(row > score, 1.0, 0.0))
        selneg_t = jnp.where(rank < SLC_TOPK, 0.0, NEG)
        selneg_t = jnp.concatenate([selneg_t, jnp.zeros((LANE - ns, Q), F32)], axis=0)
        selneg = selneg_t.T.astype(BF16)
        q_pads.append(q_pad)
        lhss.append(jnp.concatenate([q_pad, jnp.concatenate([selneg] * R, axis=0)], axis=1))

    def flash_step(s, vt_tile, carry):
        m, l, acc = carry
        m_new = jnp.maximum(m, jnp.max(s, axis=0, keepdims=True))
        alpha = jnp.exp(m - m_new)
        pt = jnp.exp(s - m_new)
        l = alpha * l + jnp.sum(pt, axis=0, keepdims=True)
        acc = alpha * acc + jnp.dot(vt_tile, pt.astype(BF16), preferred_element_type=F32)
        return m_new, l, acc

    init = (jnp.full((1, R * Q), NEG, F32), jnp.zeros((1, R * Q), F32), jnp.zeros((LANE, R * Q), F32))

    def slc_tile(t, carry):
        k0 = pl.multiple_of(t * SLC_TK, SLC_TK)
        kt = kvs_ref[pl.ds(k0, SLC_TK), :].astype(BF16)
        keys = jnp.concatenate([kt, et_ref[pl.ds(k0, SLC_TK), :]], axis=1)
        c0 = pl.multiple_of(jnp.maximum(SLC_C - (q0 - k0), 0), LANE)
        ch = t * (SLC_TK // LANE)
        vt = jnp.concatenate([vst_ref[ch + c] for c in range(SLC_TK // LANE)], axis=1)
        return tuple(flash_step(_dot_nt(keys, lhss[g]) + bslc_ref[g, pl.ds(c0, SLC_TK), :], vt, carry[g])
                     for g in range(G))

    def slc_body(tp, carry):
        for u in range(SLC_UNROLL):
            carry = slc_tile(tp * SLC_UNROLL + u, carry)
        return carry

    n_tiles = q0 // SLC_TK + 1
    slc = lax.fori_loop(0, (n_tiles + SLC_UNROLL - 1) // SLC_UNROLL, slc_body, (init,) * G)

    k0 = pl.multiple_of(jnp.maximum(q0 - WINDOW, 0), LANE)
    kt = kvw_ref[pl.ds(k0, WIN_KEYS), :].astype(BF16)
    c0 = pl.multiple_of(WINDOW - (q0 - k0), LANE)
    ch = k0 // LANE
    vt = jnp.concatenate([vwt_ref[ch + c] for c in range(WIN_KEYS // LANE)], axis=1)
    o_wins = []
    for g in range(G):
        s = _dot_nt(kt, q_pads[g]) + bwin_ref[g, pl.ds(c0, WIN_KEYS), :]
        e = jnp.exp(s - jnp.max(s, axis=0, keepdims=True))
        o_wins.append(jnp.dot(vt, e.astype(BF16), preferred_element_type=F32) / jnp.sum(e, axis=0, keepdims=True))

    gates_t = gate_ref[...].T
    grow = lax.broadcasted_iota(jnp.int32, (GATE_PAD, Q), 0)
    for g in range(G):
        def gate_row(branch):
            rows = [jnp.sum(jnp.where(grow == branch * N_HEADS + g * R + r, gates_t, 0.0), axis=0, keepdims=True)
                    for r in range(R)]
            return jnp.concatenate(rows, axis=1)

        o = (gate_row(0) * o_cmps[g] + gate_row(1) * (slc[g][2] / slc[g][1])
             + gate_row(2) * o_wins[g])
        o = o[g * HEAD_DIM:(g + 1) * HEAD_DIM]
        for pair in range(R // 2):
            two = jnp.concatenate([o[:, (2 * pair) * Q:(2 * pair + 1) * Q],
                                   o[:, (2 * pair + 1) * Q:(2 * pair + 2) * Q]], axis=0)
            c = (g * R // 2 + pair) * LANE
            o_ref[:, c:c + LANE] = two.T


def _prompt_tables(tbl_flat, T):
    nqb = T // Q_BLOCK
    nc = T // CMP_STRIDE
    qi = jnp.arange(Q_BLOCK, dtype=jnp.int32)
    q_pos = jnp.arange(nqb, dtype=jnp.int32)[:, None, None] * Q_BLOCK + qi[None, :, None]
    c_end = jnp.arange(nc, dtype=jnp.int32) * CMP_STRIDE + (CMP_BLOCK - 1)
    big = 1 << 30
    bcmp = _bias_tables(tbl_flat, jnp.swapaxes(q_pos - c_end[None, None, :], 1, 2), big, True)

    def toeplitz(c_off, nrows, hi):
        dist = qi[None, None, :] + c_off - jnp.arange(nrows, dtype=jnp.int32)[None, :, None]
        return _bias_tables(tbl_flat, dist, hi, True)[:, 0]

    bslc = toeplitz(SLC_C, SLC_C + SLC_UNROLL * SLC_TK, big)
    bwin = toeplitz(WINDOW, WINDOW + WIN_KEYS, WINDOW)
    keys = jnp.arange(T, dtype=jnp.int32)
    et = (keys[:, None] // SLC_BLOCK == jnp.arange(LANE, dtype=jnp.int32)[None, :]).astype(BF16)
    c_start = jnp.arange(nc, dtype=jnp.int32) * CMP_STRIDE
    s_start = jnp.arange(T // SLC_BLOCK, dtype=jnp.int32) * SLC_BLOCK
    ovt = ((c_start[None, :] < s_start[:, None] + SLC_BLOCK)
           & (c_start[None, :] + CMP_BLOCK > s_start[:, None])).astype(BF16)
    return bcmp, bslc, bwin, et, ovt


def _nsa_prompt_call(q, gates, kc, vct, kvs, vst, kvw, vwt, tables, B, T):
    bcmp, bslc, bwin, et, ovt = tables
    nqb = T // Q_BLOCK
    nc = T // CMP_STRIDE
    R = GQA_GROUP
    seq = pl.BlockSpec((T, KV_W), lambda b, i: (b, 0))
    seq_t = pl.BlockSpec((T // LANE, KV_W, LANE), lambda b, i: (b, 0, 0))
    return pl.pallas_call(
        _nsa_prompt_kernel,
        grid=(B, nqb),
        in_specs=[
            pl.BlockSpec((Q_BLOCK, NSA_W), lambda b, i: (b * nqb + i, 0)),
            pl.BlockSpec((Q_BLOCK, GATE_PAD), lambda b, i: (b * nqb + i, 0)),
            pl.BlockSpec((1, nc, KV_W), lambda b, i: (b, 0, 0)),
            pl.BlockSpec((1, KV_W, nc), lambda b, i: (b, 0, 0)),
            seq, seq_t, seq, seq_t,
            pl.BlockSpec((KV_HEADS, 1, nc, R * Q_BLOCK), lambda b, i: (0, i, 0, 0)),
            pl.BlockSpec(bslc.shape, lambda b, i: (0, 0, 0)),
            pl.BlockSpec(bwin.shape, lambda b, i: (0, 0, 0)),
            pl.BlockSpec(et.shape, lambda b, i: (0, 0)),
            pl.BlockSpec(ovt.shape, lambda b, i: (0, 0)),
        ],
        out_specs=pl.BlockSpec((Q_BLOCK, NSA_W), lambda b, i: (b * nqb + i, 0)),
        out_shape=jax.ShapeDtypeStruct((B * T, NSA_W), F32),
        compiler_params=pltpu.CompilerParams(dimension_semantics=("parallel", "arbitrary"),
                                             vmem_limit_bytes=VMEM_LIMIT),
        name="nsa_prompt",
    )(q, gates, kc, vct, kvs, vst, kvw, vwt, bcmp, bslc, bwin, et, ovt)


SUB = 8
PAGES_PER_STEP = 64
UNTRANSPOSE_UNROLL = 8
SAMPLE_BATCHES = 2


def _compress_pages_kernel(xt_ref, pea_ref, peb_ref, wa_ref, wb_ref, o_ref, xk_ref, xv_ref):
    pb, w, page = xt_ref.shape
    hw = w // 2
    ch = page // CMP_STRIDE

    def untranspose(i, carry):
        for u in range(UNTRANSPOSE_UNROLL):
            p = i * UNTRANSPOSE_UNROLL + u
            xk_ref[p] = xt_ref[p, :hw, :].T
            xv_ref[p] = xt_ref[p, hw:, :].T
        return carry

    lax.fori_loop(0, pb // UNTRANSPOSE_UNROLL, untranspose, 0)
    a = jnp.zeros((pb * ch, w), F32)
    b = jnp.zeros((pb * ch, w), F32)
    for t in range(CMP_STRIDE):
        x = jnp.concatenate([ref[:, pl.ds(t, ch, stride=CMP_STRIDE), :].reshape(pb * ch, hw)
                             for ref in (xk_ref, xv_ref)], axis=1)
        cols = slice(t * w, (t + 1) * w)
        a = a + jnp.dot((x + pea_ref[:, cols]).astype(BF16), wa_ref[cols, :], preferred_element_type=F32)
        b = b + jnp.dot((x + peb_ref[:, cols]).astype(BF16), wb_ref[cols, :], preferred_element_type=F32)
    o_ref[...] = jnp.concatenate([a, b], axis=1).reshape(pb, ch, 2 * w)


def _compress_pages(pages, layer, n_phys, cw):
    _, w, page = pages.shape
    ch = page // CMP_STRIDE
    steps = n_phys // PAGES_PER_STEP
    full = lambda a: pl.BlockSpec(a.shape, lambda i: (0, 0))
    return pl.pallas_call(
        _compress_pages_kernel,
        grid=(steps,),
        in_specs=[pl.BlockSpec((PAGES_PER_STEP, w, page), lambda i: (layer * steps + i, 0, 0))] + [full(a) for a in cw],
        out_specs=pl.BlockSpec((PAGES_PER_STEP, ch, 4 * KV_W), lambda i: (i, 0, 0)),
        out_shape=jax.ShapeDtypeStruct((n_phys, ch, 4 * KV_W), F32),
        scratch_shapes=[pltpu.VMEM((PAGES_PER_STEP, page, w // 2), F32)] * 2,
        compiler_params=pltpu.CompilerParams(dimension_semantics=("parallel",), vmem_limit_bytes=VMEM_LIMIT),
        name="compress_pages",
    )(pages, *cw)


def _nsa_sample_kernel(pt_ref, q_ref, gate_ref, ks_new_ref, kw_new_ref, *rest, n_pages, past, dec_seq):
    n = SAMPLE_BATCHES * n_pages
    ab_refs, slc_refs = rest[:n], rest[n:2 * n]
    win_ref, bcmp_ref, bslc_ref, bwin_ref, et_ref, ov_ref, o_ref, win_out_ref = rest[2 * n:]
    for bb in range(SAMPLE_BATCHES):
        one = lambda ref: ref.at[pl.ds(bb, 1)]
        pages = slice(bb * n_pages, (bb + 1) * n_pages)
        _nsa_sample_one(one(q_ref), one(gate_ref), one(ks_new_ref), one(kw_new_ref), ab_refs[pages], slc_refs[pages],
                        one(win_ref), bcmp_ref, bslc_ref, bwin_ref, et_ref, ov_ref, one(o_ref), one(win_out_ref),
                        past, dec_seq)


def _nsa_sample_one(q_ref, gate_ref, ks_new_ref, kw_new_ref, ab_refs, slc_refs, win_ref, bcmp_ref, bslc_ref, bwin_ref,
                    et_ref, ov_ref, o_ref, win_out_ref, past, dec_seq):
    R, G = GQA_GROUP, KV_HEADS
    rows = R * G * SUB
    half = lax.broadcasted_iota(jnp.int32, (SUB, LANE), 1) // HEAD_DIM

    qb = q_ref[0] * (HEAD_DIM ** -0.5)
    slabs = []
    for r in range(R):
        for g in range(G):
            h = g * R + r
            slab = qb[:, (h // 2) * LANE:(h // 2 + 1) * LANE]
            slabs.append(jnp.where(half == g, _half_select(slab, h % 2, half), 0.0))
    q_pad = jnp.concatenate(slabs, axis=0).astype(BF16)

    ab = jnp.concatenate([r_[0] for r_ in ab_refs], axis=0)
    nc = ab.shape[0]
    kcvc = ab[:, :2 * KV_W] + pltpu.roll(ab[:, 2 * KV_W:], nc - 1, axis=0)
    kc = kcvc[:, :KV_W].astype(BF16)
    vc = kcvc[:, KV_W:].astype(BF16)
    s = _dot_nt(q_pad, kc) + bcmp_ref[...]
    e = jnp.exp(s - jnp.max(s, axis=-1, keepdims=True))
    p = e / jnp.sum(e, axis=-1, keepdims=True)
    row_pos = past + lax.broadcasted_iota(jnp.int32, (rows, 1), 0) % SUB
    p = jnp.where(row_pos >= CMP_BLOCK - 1, p, 0.0)
    pb = p.astype(BF16)
    o_cmp = jnp.dot(pb, vc, preferred_element_type=F32)

    ov = ov_ref[...]
    imp = jnp.dot(pb[:G * SUB], ov, preferred_element_type=F32)
    for r in range(1, R):
        imp = imp + jnp.dot(pb[r * G * SUB:(r + 1) * G * SUB], ov, preferred_element_type=F32)
    jidx = lax.broadcasted_iota(jnp.int32, (G * SUB, LANE), 1)
    cur = (past + lax.broadcasted_iota(jnp.int32, (G * SUB, LANE), 0) % SUB) // SLC_BLOCK
    forced = (jidx == 0) | (jidx == cur) | (jidx == cur - 1)
    score = jnp.where(forced, FORCE, jnp.where(jidx <= cur, imp, NEG))
    n_blocks = (past + dec_seq + SLC_BLOCK - 1) // SLC_BLOCK
    rank = jnp.zeros((G * SUB, LANE), F32)
    for jp in range(n_blocks):
        col = score[:, jp:jp + 1]
        rank = rank + jnp.where(jidx > jp, jnp.where(col >= score, 1.0, 0.0), jnp.where(col > score, 1.0, 0.0))
    selneg = jnp.where(rank < min(SLC_TOPK, n_blocks), 0.0, NEG).astype(BF16)
    lhs = jnp.concatenate([q_pad, jnp.concatenate([selneg] * R, axis=0)], axis=1)

    def new_cols(ref):
        x = ref[0]
        xt = jnp.concatenate([x, jnp.zeros((LANE - SUB, 2 * KV_W), F32)], axis=0).T
        return xt, xt[:KV_W].astype(BF16), xt[KV_W:].astype(BF16)

    def attend(lhs_, kt_tiles, vt_tiles, bias, ext):
        parts = []
        for t, kt in enumerate(kt_tiles):
            rhs = jnp.concatenate([kt, et_ref[:, t * LANE:(t + 1) * LANE]], axis=0) if ext else kt
            parts.append(jnp.dot(lhs_, rhs, preferred_element_type=F32))
        s_ = jnp.concatenate(parts, axis=1) + bias
        e_ = jnp.exp(s_ - jnp.max(s_, axis=-1, keepdims=True))
        p_ = (e_ / jnp.sum(e_, axis=-1, keepdims=True)).astype(BF16)
        o_ = _dot_nt(p_[:, :LANE], vt_tiles[0])
        for t in range(1, len(vt_tiles)):
            o_ = o_ + _dot_nt(p_[:, t * LANE:(t + 1) * LANE], vt_tiles[t])
        return o_

    _, kt_new, vt_new = new_cols(ks_new_ref)
    kt_tiles = [r_[0, :KV_W, :].astype(BF16) for r_ in slc_refs] + [kt_new]
    vt_tiles = [r_[0, KV_W:, :].astype(BF16) for r_ in slc_refs] + [vt_new]
    o_slc = attend(lhs, kt_tiles, vt_tiles, bslc_ref[...], True)

    win_new_t, kt_new, vt_new = new_cols(kw_new_ref)
    wb = win_ref.shape[2]
    kt_tiles = [win_ref[0, :KV_W, t * LANE:(t + 1) * LANE].astype(BF16) for t in range(wb // LANE)] + [kt_new]
    vt_tiles = [win_ref[0, KV_W:, t * LANE:(t + 1) * LANE].astype(BF16) for t in range(wb // LANE)] + [vt_new]
    o_win = attend(q_pad, kt_tiles, vt_tiles, bwin_ref[...], False)

    gates = gate_ref[0]
    glane = lax.broadcasted_iota(jnp.int32, (SUB, LANE), 1)

    def gate_col(branch):
        cols = [jnp.sum(jnp.where(glane == branch * N_HEADS + g * R + r, gates, 0.0), axis=-1, keepdims=True)
                for r in range(R) for g in range(G)]
        return jnp.concatenate(cols, axis=0)

    o = gate_col(0) * o_cmp + gate_col(1) * o_slc + gate_col(2) * o_win
    for g in range(G):
        for pair in range(R // 2):
            r0 = ((2 * pair) * G + g) * SUB
            r1 = ((2 * pair + 1) * G + g) * SUB
            lo = _half_select(o[r0:r0 + SUB], g, half)
            hi = _half_select(o[r1:r1 + SUB], g, half)
            c = (g * R // 2 + pair) * LANE
            o_ref[0, :, c:c + LANE] = jnp.where(half == 0, lo, hi)

    shifted = pltpu.roll(win_ref[0], wb - dec_seq, axis=1)
    win_out_ref[0, :, :wb - LANE] = shifted[:, :wb - LANE]
    klane = lax.broadcasted_iota(jnp.int32, (2 * KV_W, LANE), 1)
    win_out_ref[0, :, wb - LANE:] = jnp.where(klane >= LANE - dec_seq, pltpu.roll(win_new_t, LANE - dec_seq, axis=1),
                                              shifted[:, wb - LANE:])


def _sample_tables(tbl_flat, past, dec_seq, wb):
    q_pos = past + jnp.arange(SUB, dtype=jnp.int32)
    nc = past // CMP_STRIDE
    c_end = jnp.arange(nc, dtype=jnp.int32) * CMP_STRIDE + (CMP_BLOCK - 1)
    slc_keys = jnp.arange(past + LANE, dtype=jnp.int32)
    win_pos = jnp.concatenate([past - wb + jnp.arange(wb, dtype=jnp.int32), past + jnp.arange(LANE, dtype=jnp.int32)])
    big = 1 << 30

    def table(key_pos, hi):
        t = _bias_tables(tbl_flat, (q_pos[:, None] - key_pos[None, :])[None], hi)
        c = t.shape[-1]
        return t.reshape(KV_HEADS, GQA_GROUP, SUB, c).transpose(1, 0, 2, 3).reshape(GQA_GROUP * KV_HEADS * SUB, c)

    bcmp = table(c_end, big)
    bslc = table(slc_keys, big)
    bwin = table(win_pos, WINDOW)
    def kill_pad(b):
        lane = jnp.arange(b.shape[1]) - (b.shape[1] - LANE)
        return jnp.where((lane >= dec_seq)[None, :], NEG, b)
    bslc, bwin = kill_pad(bslc), kill_pad(bwin)
    et = (slc_keys[None, :] // SLC_BLOCK == jnp.arange(LANE, dtype=jnp.int32)[:, None]).astype(BF16)
    c_start = jnp.arange(nc, dtype=jnp.int32) * CMP_STRIDE
    s_start = jnp.arange(LANE, dtype=jnp.int32) * SLC_BLOCK
    ov = ((c_start[:, None] < s_start[None, :] + SLC_BLOCK)
          & (c_start[:, None] + CMP_BLOCK > s_start[None, :])).astype(BF16)
    return bcmp, bslc, bwin, et, ov


def _nsa_sample_call(page_table_flat, q, gates, ks_new, kw_new, ab_pages, slc_pages, win_buf, tables, layer, n_phys,
                     dec_seq):
    bcmp, bslc, bwin, et, ov = tables
    B = q.shape[0]
    n_pages = page_table_flat.shape[0] // B
    page = slc_pages.shape[2]
    past = n_pages * page
    wb = win_buf.shape[2]
    nb = SAMPLE_BATCHES
    row3 = lambda a: pl.BlockSpec((nb,) + a.shape[1:], lambda b, pt: (b, 0, 0))
    full = lambda a: pl.BlockSpec(a.shape, lambda b, pt: (0, 0))
    slots = [(bb, p) for bb in range(nb) for p in range(n_pages)]
    ab_specs = [pl.BlockSpec((1,) + ab_pages.shape[1:],
                             lambda b, pt, bb=bb, p=p: (pt[(b * nb + bb) * n_pages + p], 0, 0)) for bb, p in slots]
    slc_specs = [pl.BlockSpec((1,) + slc_pages.shape[1:],
                              lambda b, pt, bb=bb, p=p: (layer * n_phys + pt[(b * nb + bb) * n_pages + p], 0, 0))
                 for bb, p in slots]
    return pl.pallas_call(
        functools.partial(_nsa_sample_kernel, n_pages=n_pages, past=past, dec_seq=dec_seq),
        grid_spec=pltpu.PrefetchScalarGridSpec(
            num_scalar_prefetch=1,
            grid=(B // nb,),
            in_specs=[row3(q), row3(gates), row3(ks_new), row3(kw_new)] + ab_specs + slc_specs
                     + [pl.BlockSpec((nb, 2 * KV_W, wb), lambda b, pt: (layer * (B // nb) + b, 0, 0)),
                        full(bcmp), full(bslc), full(bwin), full(et), full(ov)],
            out_specs=[pl.BlockSpec((nb, SUB, NSA_W), lambda b, pt: (b, 0, 0)),
                       pl.BlockSpec((nb, 2 * KV_W, wb), lambda b, pt: (b, 0, 0))],
        ),
        out_shape=[jax.ShapeDtypeStruct((B, SUB, NSA_W), F32), jax.ShapeDtypeStruct((B, 2 * KV_W, wb), F32)],
        compiler_params=pltpu.CompilerParams(dimension_semantics=("parallel",), vmem_limit_bytes=VMEM_LIMIT),
        name="nsa_sample",
    )(page_table_flat, q, gates, ks_new, kw_new, *([ab_pages] * len(slots)), *([slc_pages] * len(slots)), win_buf,
      bcmp, bslc, bwin, et, ov)


CONV_HALO = 32
POOL_HALO = 16
MIX_ROWS = 512


def _mix_kernel(glu_ref, glu_halo_ref, glu_hist_ref, u_ref, u_halo_ref, u_hist_ref, dw_ref, cb_ref, lng_ref, lnb_ref,
                wp_ref, ps_ref, conv_ref, pool_ref, xs_ref, us_ref, y_ref, d_ref, *, pos0, ns):
    j = pl.program_id(1)
    tq = glu_ref.shape[0] // ns
    pos = pos0 + j * tq + lax.broadcasted_iota(jnp.int32, (tq, 1), 0)
    grp = lax.broadcasted_iota(jnp.int32, (tq, POOL_W), 1) // POOL_GROUP_W

    def one_sequence(s, rows):
        @pl.when(j == 0)
        def _():
            xs_ref[:CONV_HALO, :] = glu_hist_ref[s]
            us_ref[:POOL_HALO, :] = u_hist_ref[s]

        @pl.when(j > 0)
        def _():
            xs_ref[:CONV_HALO, :] = glu_halo_ref[...]
            us_ref[:POOL_HALO, :] = u_halo_ref[...]

        u = u_ref[rows, :]
        xs_ref[CONV_HALO:, :] = glu_ref[rows, :]
        us_ref[POOL_HALO:, :] = u

        base = CONV_HALO - CONV_BUF
        y = xs_ref[base:base + tq, :] * dw_ref[0:1, :]
        for k in range(1, CONV_WIDTH):
            y = y + xs_ref[base + k:base + k + tq, :] * dw_ref[k:k + 1, :]
        y_ref[rows, :] = y

        run = u
        d = jnp.zeros_like(u)
        done = 1
        for gi, w in enumerate(POOL_WINDOWS):
            for back in range(done, w):
                run = run + us_ref[POOL_HALO - back:POOL_HALO - back + tq, :]
            done = w
            cnt = jnp.minimum(w, pos + 1).astype(F32)
            d = jnp.where(grp == gi, run / cnt - u, d)
        d_ref[rows, :] = d

    if ns == 1:
        one_sequence(0, slice(None))
    else:
        def body(s, carry):
            one_sequence(s, pl.ds(pl.multiple_of(s * tq, SUB), tq))
            return carry

        lax.fori_loop(0, ns, body, 0)

    y = y_ref[...] + cb_ref[...]
    mu = jnp.mean(y, axis=-1, keepdims=True)
    yc = y - mu
    var = jnp.mean(yc * yc, axis=-1, keepdims=True)
    conv_ref[...] = jax.nn.silu(yc * lax.rsqrt(var + EPS) * lng_ref[...] + lnb_ref[...])
    yp = jnp.dot(d_ref[...].astype(BF16), wp_ref[...], preferred_element_type=F32)
    pool_ref[...] = yp * ps_ref[...]


def _mix(glu, u, glu_hist, u_hist, dw, cb, lng, lnb, wp_bd, ps, B, T, tq, pos0):
    nt = T // tq
    ns = min(B, MIX_ROWS // tq) if nt == 1 else 1
    row = pl.BlockSpec((ns * tq, CONV_W), lambda b, j: (b * nt + j, 0))
    full = lambda a: pl.BlockSpec(a.shape, lambda b, j: (0, 0))

    def halo(h):
        return pl.BlockSpec((h, CONV_W), lambda b, j: (jnp.maximum(b * (T // h) + j * (tq // h) - 1, 0), 0))

    hist = lambda h: pl.BlockSpec((ns, h, CONV_W), lambda b, j: (b, 0, 0))
    if nt == 1:
        glu_halo_arr, u_halo_arr = glu_hist.reshape(-1, CONV_W), u_hist.reshape(-1, POOL_W)
        halo_c = pl.BlockSpec((CONV_HALO, CONV_W), lambda b, j: (0, 0))
        halo_p = pl.BlockSpec((POOL_HALO, POOL_W), lambda b, j: (0, 0))
    else:
        glu_halo_arr, u_halo_arr = glu, u
        halo_c, halo_p = halo(CONV_HALO), halo(POOL_HALO)
    return pl.pallas_call(
        functools.partial(_mix_kernel, pos0=pos0, ns=ns),
        grid=(B // ns, nt),
        in_specs=[row, halo_c, hist(CONV_HALO), row, halo_p, hist(POOL_HALO),
                  full(dw), full(cb), full(lng), full(lnb), full(wp_bd), full(ps)],
        out_specs=[row, row],
        out_shape=[jax.ShapeDtypeStruct((B * T, CONV_W), F32), jax.ShapeDtypeStruct((B * T, POOL_W), F32)],
        scratch_shapes=[pltpu.VMEM((CONV_HALO + tq, CONV_W), F32), pltpu.VMEM((POOL_HALO + tq, POOL_W), F32),
                        pltpu.VMEM((ns * tq, CONV_W), F32), pltpu.VMEM((ns * tq, POOL_W), F32)],
        compiler_params=pltpu.CompilerParams(dimension_semantics=("parallel", "parallel"),
                                             vmem_limit_bytes=VMEM_LIMIT),
        name="conv_pool",
    )(glu, glu_halo_arr, glu_hist, u, u_halo_arr, u_hist, dw, cb, lng, lnb, wp_bd, ps)


def _layer(x2d, B, T, l, P, glu_hist, u_hist, pos0, nsa_fn, final, emit_vt):
    n = B * T
    glu, u, q, kvc, kvs, kvw, gates, *vts = _inproj(x2d, P['norm1'][l], P['w_in'][l], min(n, 512), T, emit_vt)
    conv_o, pool_o = _mix(glu, u, glu_hist, u_hist, P['conv_dw'][l], P['conv_b'][l], P['conv_ln_g'][l],
                          P['conv_ln_b'][l], P['pool_w'][l], P['pool_scale'][l], B, T, min(T, MIX_ROWS), pos0)
    nsa_o, extra = nsa_fn(q, gates, kvc, kvs, kvw, *vts[:2])
    y = _ffn(x2d, conv_o, pool_o, nsa_o, P['w_out'][l], P['norm2'][l], P['w_gu'][l], P['w_down'][l],
             P['final_norm'], min(n, 512), D_FF, final)
    return y, glu, u, kvc, kvs, kvw, extra, vts[2:]


def kernel(x_prompt, x_sample, cache_cmp_kv, cache_slc_kv, cache_win_kv, state_conv, state_pool, page_table, rel_bias, norm1, w_in, conv_dw, conv_b, conv_ln_g, conv_ln_b, pool_w, pool_scale, cmp_pe_k, cmp_wk, cmp_pe_v, cmp_wv, w_out, norm2, w_gu, w_down, final_norm):
    depth = w_in.shape[0]
    Bp, T, _ = x_prompt.shape
    Bs, S, _ = x_sample.shape
    n_phys, page = cache_cmp_kv.shape[1:3]
    n_pages = page_table.shape[1]
    past = n_pages * page
    wb = cache_win_kv.shape[2]
    tbl_flat = rel_bias.reshape(NUM_BUCKETS, KV_HEADS, GQA_GROUP).transpose(1, 2, 0).reshape(-1)
    eye = jnp.eye(len(POOL_WINDOWS), dtype=F32)
    row = lambda a: a[:, None, :]
    P = dict(
        norm1=row(norm1), norm2=row(norm2), final_norm=final_norm[None, :],
        w_in=jnp.pad(w_in, ((0, 0), (0, 0), (0, IN_W_PAD - IN_W))).astype(BF16),
        w_out=w_out.astype(BF16), w_gu=w_gu.astype(BF16), w_down=w_down.astype(BF16),
        conv_dw=conv_dw, conv_b=row(conv_b), conv_ln_g=row(conv_ln_g), conv_ln_b=row(conv_ln_b),
        pool_w=jnp.einsum('lgce,gh->lgche', pool_w, eye).reshape(depth, POOL_W, POOL_W).astype(BF16),
        pool_scale=row(pool_scale))
    ptables = _prompt_tables(tbl_flat, T)
    stables = _sample_tables(tbl_flat, past, S, wb)
    cmp_pages = jnp.swapaxes(cache_cmp_kv.reshape(depth * n_phys, page, 2 * KV_W), 1, 2)
    slc_pages = jnp.swapaxes(cache_slc_kv.reshape(depth * n_phys, page, 2 * KV_W), 1, 2)
    win_bufs = jnp.swapaxes(cache_win_kv.reshape(depth * Bs, wb, 2 * KV_W), 1, 2)
    pt_flat = page_table.reshape(-1)

    xp = x_prompt.reshape(Bp * T, D_MODEL)
    xs = jnp.pad(x_sample, ((0, 0), (0, SUB - S), (0, 0))).reshape(Bs * SUB, D_MODEL)
    zc = jnp.zeros((Bp, CONV_HALO, CONV_W), F32)
    zp = jnp.zeros((Bp, POOL_HALO, POOL_W), F32)
    kv5 = lambda a, b, t: a.reshape(b, t, 2, KV_HEADS, HEAD_DIM)
    outs_p, outs_s = [], []
    for l in range(depth):
        final = l == depth - 1
        cw = _compress_weights(cmp_pe_k[l], cmp_wk[l], cmp_pe_v[l], cmp_wv[l])

        def nsa_p(q, gates, kvc, kvs, kvw, vst, vwt):
            kc, vct = _compress_call(kvc.reshape(Bp, T // CMP_STRIDE, CMP_STRIDE * 2 * KV_W), cw)
            return _nsa_prompt_call(q, gates, kc, vct, kvs, vst, kvw, vwt, ptables, Bp, T), None

        xp, glu, u, _, _, _, _, kvts = _layer(xp, Bp, T, l, P, zc, zp, 0, nsa_p, final, True)
        from_t = lambda a: kv5(jnp.swapaxes(a, 1, 2), Bp, a.shape[2])
        outs_p.append((from_t(kvts[0]), from_t(kvts[1]), from_t(kvts[2][:, :, T - min(WINDOW, T):]),
                       glu.reshape(Bp, T, CONV_W)[:, T - CONV_BUF:], u.reshape(Bp, T, POOL_W)[:, T - POOL_BUF:]))

        def nsa_s(q, gates, kvc, kvs, kvw):
            ab = _compress_pages(cmp_pages, l, n_phys, cw)
            r3 = lambda a: a.reshape(Bs, SUB, a.shape[-1])
            o, new_win = _nsa_sample_call(pt_flat, r3(q), r3(gates), r3(kvs), r3(kvw), ab, slc_pages, win_bufs,
                                          stables, l, n_phys, S)
            return o.reshape(Bs * SUB, NSA_W), new_win

        glu_hist = jnp.pad(state_conv[l], ((0, 0), (CONV_HALO - CONV_BUF, 0), (0, 0)))
        u_hist = jnp.pad(state_pool[l], ((0, 0), (POOL_HALO - POOL_BUF, 0), (0, 0)))
        xs, glu, u, kvc, kvs, kvw, new_win, _ = _layer(xs, Bs, SUB, l, P, glu_hist, u_hist, past, nsa_s, final, False)
        new_conv = jnp.concatenate([state_conv[l], glu.reshape(Bs, SUB, CONV_W)[:, :S]], axis=1)[:, -CONV_BUF:]
        new_pool = jnp.concatenate([state_pool[l], u.reshape(Bs, SUB, POOL_W)[:, :S]], axis=1)[:, -POOL_BUF:]
        outs_s.append((kv5(kvc, Bs, SUB)[:, :S], kv5(kvs, Bs, SUB)[:, :S], kv5(jnp.swapaxes(new_win, 1, 2), Bs, wb), new_conv, new_pool))
    st = lambda outs, i: jnp.stack([o[i] for o in outs])
    return (xp.reshape(Bp, T, D_MODEL), xs.reshape(Bs, SUB, D_MODEL)[:, :S],
            st(outs_p, 0), st(outs_p, 1), st(outs_p, 2), st(outs_p, 3), st(outs_p, 4),
            st(outs_s, 0), st(outs_s, 1), st(outs_s, 2), st(outs_s, 3), st(outs_s, 4))
```

```python
import functools
import math

import jax
import jax.numpy as jnp
from jax import lax
from jax.experimental import pallas as pl
from jax.experimental.pallas import tpu as pltpu

D_MODEL = 1024
CONV_W = 256
POOL_W = 256
N_HEADS = 8
HEAD_DIM = 64
NSA_W = N_HEADS * HEAD_DIM
KV_HEADS = 2
GQA_GROUP = N_HEADS // KV_HEADS
KV_W = KV_HEADS * HEAD_DIM
CONV_WIDTH = 31
CONV_BUF = CONV_WIDTH - 1
POOL_WINDOWS = (2, 4, 8, 16)
POOL_GROUP_W = POOL_W // len(POOL_WINDOWS)
POOL_BUF = max(POOL_WINDOWS) - 1
CMP_STRIDE = 16
CMP_BLOCK = 2 * CMP_STRIDE
SLC_BLOCK = 64
SLC_TOPK = 16
WINDOW = 512
Q_BLOCK = 128
NUM_BUCKETS = 32
MAX_DISTANCE = 128
D_FF = 2816
EPS = 1e-6
NEG = -1e30
LOG2E = math.log2(math.e)
FORCE = 1e4

OFF_POOL = 2 * CONV_W
OFF_Q = OFF_POOL + POOL_W
OFF_KV = OFF_Q + NSA_W
OFF_GATE = OFF_KV + 6 * KV_W
IN_W = OFF_GATE + 3 * N_HEADS
LANE = 128
IN_W_PAD = -(-IN_W // LANE) * LANE
GATE_PAD = IN_W_PAD - OFF_GATE

VMEM_LIMIT = 56 * 1024 * 1024

F32 = jnp.float32
BF16 = jnp.bfloat16


def _rms(x, g):
    return x * lax.rsqrt(jnp.mean(x * x, axis=-1, keepdims=True) + EPS) * g


def _inproj_kernel(x_ref, g_ref, w_ref, glu_ref, u_ref, q_ref, kvc_ref, kvs_ref, kvw_ref, gate_ref, *vt_refs):
    h = _rms(x_ref[...], g_ref[...]).astype(BF16)
    z = jnp.dot(h, w_ref[...], preferred_element_type=F32)
    glu_ref[...] = z[:, :CONV_W] * jax.nn.sigmoid(z[:, CONV_W:OFF_POOL])
    u_ref[...] = z[:, OFF_POOL:OFF_Q]
    q_ref[...] = z[:, OFF_Q:OFF_KV]
    kvc_ref[...] = z[:, OFF_KV:OFF_KV + 2 * KV_W]
    kvs_ref[...] = z[:, OFF_KV + 2 * KV_W:OFF_KV + 4 * KV_W]
    kvw_ref[...] = z[:, OFF_KV + 4 * KV_W:OFF_GATE]
    gate_ref[...] = jax.nn.sigmoid(z[:, OFF_GATE:])
    if vt_refs:
        for vt_ref, off in zip(vt_refs[:2], (OFF_KV + 3 * KV_W, OFF_KV + 5 * KV_W)):
            for c in range(vt_ref.shape[0]):
                vt_ref[c] = z[c * LANE:(c + 1) * LANE, off:off + KV_W].T.astype(BF16)
        for b, kvt_ref in enumerate(vt_refs[2:]):
            kvt_ref[0] = z[:, OFF_KV + 2 * b * KV_W:OFF_KV + 2 * (b + 1) * KV_W].T


def _inproj(x2d, g, w_pad, tm, seq_len, emit_vt):
    n = x2d.shape[0]
    nt = max(seq_len // tm, 1)
    row = lambda w: pl.BlockSpec((tm, w), lambda i: (i, 0))
    full = lambda a: pl.BlockSpec(a.shape, lambda i: (0, 0))
    widths = (CONV_W, POOL_W, NSA_W, 2 * KV_W, 2 * KV_W, 2 * KV_W, GATE_PAD)
    out_specs = [row(w) for w in widths]
    out_shape = [jax.ShapeDtypeStruct((n, w), F32) for w in widths]
    if emit_vt:
        out_specs += [pl.BlockSpec((tm // LANE, KV_W, LANE), lambda i: (i, 0, 0))] * 2
        out_shape += [jax.ShapeDtypeStruct((n // LANE, KV_W, LANE), BF16)] * 2
        out_specs += [pl.BlockSpec((1, 2 * KV_W, tm), lambda i: (i // nt, 0, i % nt))] * 3
        out_shape += [jax.ShapeDtypeStruct((n // seq_len, 2 * KV_W, seq_len), F32)] * 3
    return pl.pallas_call(
        _inproj_kernel,
        grid=(n // tm,),
        in_specs=[row(D_MODEL), full(g), full(w_pad)],
        out_specs=out_specs,
        out_shape=out_shape,
        compiler_params=pltpu.CompilerParams(dimension_semantics=("parallel",), vmem_limit_bytes=VMEM_LIMIT),
        name="inproj",
    )(x2d, g, w_pad)


def _ffn_kernel(x_ref, conv_ref, pool_ref, nsa_ref, wo_ref, g2_ref, wg_ref, wu_ref, wd_ref, gf_ref,
                y_ref, x1_ref, h2_ref, acc_ref, *, final_norm):
    j = pl.program_id(1)

    @pl.when(j == 0)
    def _():
        mix = jnp.concatenate([conv_ref[...], pool_ref[...], nsa_ref[...]], axis=-1).astype(BF16)
        x1 = x_ref[...] + jnp.dot(mix, wo_ref[...], preferred_element_type=F32)
        x1_ref[...] = x1
        h2_ref[...] = _rms(x1, g2_ref[...]).astype(BF16)
        acc_ref[...] = jnp.zeros_like(acc_ref)

    h2 = h2_ref[...]
    gate = jnp.dot(h2, wg_ref[...], preferred_element_type=F32)
    up = jnp.dot(h2, wu_ref[...], preferred_element_type=F32)
    act = (jax.nn.silu(gate) * up).astype(BF16)
    acc_ref[...] += jnp.dot(act, wd_ref[...], preferred_element_type=F32)

    @pl.when(j == pl.num_programs(1) - 1)
    def _():
        y = x1_ref[...] + acc_ref[...]
        if final_norm:
            y = _rms(y, gf_ref[...])
        y_ref[...] = y


def _ffn(x2d, conv_o, pool_o, nsa_o, w_out, g2, w_gu, w_down, gf, tm, tf, final_norm):
    n = x2d.shape[0]
    nf = D_FF // tf
    row = lambda w: pl.BlockSpec((tm, w), lambda i, j: (i, 0))
    full = lambda a: pl.BlockSpec(a.shape, lambda i, j: (0, 0))
    return pl.pallas_call(
        functools.partial(_ffn_kernel, final_norm=final_norm),
        grid=(n // tm, nf),
        in_specs=[row(D_MODEL), row(CONV_W), row(POOL_W), row(NSA_W), full(w_out), full(g2),
                  pl.BlockSpec((D_MODEL, tf), lambda i, j: (0, j)),
                  pl.BlockSpec((D_MODEL, tf), lambda i, j: (0, j + nf)),
                  pl.BlockSpec((tf, D_MODEL), lambda i, j: (j, 0)),
                  full(gf)],
        out_specs=row(D_MODEL),
        out_shape=jax.ShapeDtypeStruct((n, D_MODEL), F32),
        scratch_shapes=[pltpu.VMEM((tm, D_MODEL), F32), pltpu.VMEM((tm, D_MODEL), BF16),
                        pltpu.VMEM((tm, D_MODEL), F32)],
        compiler_params=pltpu.CompilerParams(dimension_semantics=("parallel", "arbitrary"),
                                             vmem_limit_bytes=VMEM_LIMIT),
        name="ffn",
    )(x2d, conv_o, pool_o, nsa_o, w_out, g2, w_gu, w_gu, w_down, gf)


def _bias_kernel(tbl_ref, dist_ref, out_ref, *, hi, heads_on_lanes, scale):
    g = pl.program_id(0)
    d = dist_ref[0]
    rows, cols = d.shape
    n = jnp.maximum(d, 0)
    max_exact = NUM_BUCKETS // 2
    nf = jnp.maximum(n, 1).astype(F32)
    large = max_exact + (jnp.log(nf / max_exact) / math.log(MAX_DISTANCE / max_exact)
                         * (NUM_BUCKETS - max_exact)).astype(jnp.int32)
    large = jnp.minimum(large, NUM_BUCKETS - 1)
    bucket = jnp.where(n < max_exact, n, large)
    valid = (d >= 0) & (d < hi)
    for r in range(GQA_GROUP):
        acc = jnp.zeros(d.shape, F32)
        for b in range(NUM_BUCKETS):
            acc = jnp.where(bucket == b, tbl_ref[(g * GQA_GROUP + r) * NUM_BUCKETS + b], acc)
        val = jnp.where(valid, acc * scale, NEG)
        if heads_on_lanes:
            out_ref[0, 0, :, r * cols:(r + 1) * cols] = val
        else:
            out_ref[0, 0, r * rows:(r + 1) * rows, :] = val


def _bias_tables(tbl_flat, dist, hi, heads_on_lanes=False, scale=1.0):
    nblk, R, C = dist.shape
    oshape = (R, GQA_GROUP * C) if heads_on_lanes else (GQA_GROUP * R, C)
    return pl.pallas_call(
        functools.partial(_bias_kernel, hi=hi, heads_on_lanes=heads_on_lanes, scale=scale),
        grid_spec=pltpu.PrefetchScalarGridSpec(
            num_scalar_prefetch=1,
            grid=(KV_HEADS, nblk),
            in_specs=[pl.BlockSpec((1, R, C), lambda g, i, tbl: (i, 0, 0))],
            out_specs=pl.BlockSpec((1, 1) + oshape, lambda g, i, tbl: (g, i, 0, 0)),
        ),
        out_shape=jax.ShapeDtypeStruct((KV_HEADS, nblk) + oshape, F32),
        compiler_params=pltpu.CompilerParams(dimension_semantics=("parallel", "parallel")),
        name="bias_tables",
    )(tbl_flat, dist)


def _compress_kernel(x_ref, pea_ref, peb_ref, wa_ref, wb_ref, k_ref, vt_ref):
    x = x_ref[0]
    a = jnp.dot((x + pea_ref[...]).astype(BF16), wa_ref[...], preferred_element_type=F32)
    b = jnp.dot((x + peb_ref[...]).astype(BF16), wb_ref[...], preferred_element_type=F32)
    kv = a + pltpu.roll(b, b.shape[0] - 1, axis=0)
    k_ref[0] = kv[:, :KV_W].astype(BF16)
    vt_ref[0] = kv[:, KV_W:].T.astype(BF16)


def _compress_weights(pe_k, wk, pe_v, wv):
    eye = jnp.eye(2 * KV_HEADS, dtype=F32)

    def half(lo):
        w = jnp.stack([wk[lo:lo + CMP_STRIDE], wk[lo:lo + CMP_STRIDE], wv[lo:lo + CMP_STRIDE], wv[lo:lo + CMP_STRIDE]], 1)
        big = jnp.einsum('lcde,cf->lcdfe', w, eye).reshape(CMP_STRIDE * 2 * KV_W, 2 * KV_W)
        pe = jnp.stack([pe_k[lo:lo + CMP_STRIDE], pe_k[lo:lo + CMP_STRIDE], pe_v[lo:lo + CMP_STRIDE], pe_v[lo:lo + CMP_STRIDE]], 1)
        return pe.reshape(1, CMP_STRIDE * 2 * KV_W), big.astype(BF16)

    pea, wa = half(0)
    peb, wb = half(CMP_STRIDE)
    return pea, peb, wa, wb


def _compress_call(x3, cw):
    B, nc, cw_in = x3.shape
    full = lambda a: pl.BlockSpec(a.shape, lambda b: (0, 0))
    return pl.pallas_call(
        _compress_kernel,
        grid=(B,),
        in_specs=[pl.BlockSpec((1, nc, cw_in), lambda b: (b, 0, 0))] + [full(a) for a in cw],
        out_specs=[pl.BlockSpec((1, nc, KV_W), lambda b: (b, 0, 0)), pl.BlockSpec((1, KV_W, nc), lambda b: (b, 0, 0))],
        out_shape=[jax.ShapeDtypeStruct((B, nc, KV_W), BF16), jax.ShapeDtypeStruct((B, KV_W, nc), BF16)],
        compiler_params=pltpu.CompilerParams(dimension_semantics=("parallel",), vmem_limit_bytes=VMEM_LIMIT),
        name="compress",
    )(x3, *cw)


SLC_TK = 1024
SLC_C = SLC_TK + LANE
SLC_UNROLL = 1
WIN_KEYS = WINDOW + Q_BLOCK


def _dot_nt(a, b):
    return lax.dot_general(a, b, (((1,), (1,)), ((), ())), preferred_element_type=F32)


def _half_select(slab, want_half, half):
    return jnp.where(half == want_half, slab, pltpu.roll(slab, HEAD_DIM, axis=1))


def _nsa_prompt_kernel(q_ref, gate_ref, kc_ref, vct_ref, kvs_ref, vst_ref, kvw_ref, vwt_ref, bcmp_ref, bslc_ref,
                       bwin_ref, et_ref, ovt_ref, o_ref):
    i = pl.program_id(1)
    Q = Q_BLOCK
    R = GQA_GROUP
    G = KV_HEADS
    q0 = i * Q
    half = lax.broadcasted_iota(jnp.int32, (Q, LANE), 1) // HEAD_DIM
    qb = q_ref[...] * (HEAD_DIM ** -0.5 * LOG2E)
    ovt = ovt_ref[...]
    ns = ovt.shape[0]
    jidx = lax.broadcasted_iota(jnp.int32, (ns, Q), 0)
    cur = (q0 + lax.broadcasted_iota(jnp.int32, (ns, Q), 1)) // SLC_BLOCK
    forced = (jidx == 0) | (jidx == cur) | (jidx == cur - 1)
    col_pos = q0 + lax.broadcasted_iota(jnp.int32, (1, R * Q), 1) % Q

    q_pads, lhss, o_cmps = [], [], []
    for g in range(G):
        slabs = []
        for r in range(R):
            h = g * R + r
            slab = qb[:, (h // 2) * LANE:(h // 2 + 1) * LANE]
            slabs.append(jnp.where(half == g, _half_select(slab, h % 2, half), 0.0))
        q_pad = jnp.concatenate(slabs, axis=0).astype(BF16)

        s = _dot_nt(kc_ref[0], q_pad) + bcmp_ref[g, 0]
        e = jnp.exp2(s - jnp.max(s, axis=0, keepdims=True))
        p = e / jnp.sum(e, axis=0, keepdims=True)
        p = jnp.where(col_pos >= CMP_BLOCK - 1, p, 0.0)
        pb = p.astype(BF16)
        o_cmps.append(jnp.dot(vct_ref[0], pb, preferred_element_type=F32))

        imp = jnp.dot(ovt, pb[:, :Q], preferred_element_type=F32)
        for r in range(1, R):
            imp = imp + jnp.dot(ovt, pb[:, r * Q:(r + 1) * Q], preferred_element_type=F32)
        score = jnp.where(forced, FORCE, jnp.where(jidx <= cur, imp, NEG))
        rank = jnp.zeros((ns, Q), F32)
        for jp in range(ns):
            row = score[jp:jp + 1, :]
            rank = rank + jnp.where(jidx > jp, jnp.where(row >= score, 1.0, 0.0), jnp.where(row > score, 1.0, 0.0))
        selneg_t = jnp.where(rank < SLC_TOPK, 0.0, NEG)
        selneg_t = jnp.concatenate([selneg_t, jnp.zeros((LANE - ns, Q), F32)], axis=0)
        selneg = selneg_t.T.astype(BF16)
        q_pads.append(q_pad)
        lhss.append(jnp.concatenate([q_pad, jnp.concatenate([selneg] * R, axis=0)], axis=1))

    def flash_step(s, vt_tile, carry):
        m, l, acc = carry
        m_new = jnp.maximum(m, jnp.max(s, axis=0, keepdims=True))
        alpha = jnp.exp2(m - m_new)
        pt = jnp.exp2(s - m_new)
        l = alpha * l + jnp.sum(pt, axis=0, keepdims=True)
        acc = alpha * acc + jnp.dot(vt_tile, pt.astype(BF16), preferred_element_type=F32)
        return m_new, l, acc

    init = (jnp.full((1, R * Q), NEG, F32), jnp.zeros((1, R * Q), F32), jnp.zeros((LANE, R * Q), F32))

    def slc_tile(t, carry):
        k0 = pl.multiple_of(t * SLC_TK, SLC_TK)
        kt = kvs_ref[pl.ds(k0, SLC_TK), :].astype(BF16)
        keys = jnp.concatenate([kt, et_ref[pl.ds(k0, SLC_TK), :]], axis=1)
        c0 = pl.multiple_of(jnp.maximum(SLC_C - (q0 - k0), 0), LANE)
        ch = t * (SLC_TK // LANE)
        vt = jnp.concatenate([vst_ref[ch + c] for c in range(SLC_TK // LANE)], axis=1)
        return tuple(flash_step(_dot_nt(keys, lhss[g]) + bslc_ref[g, pl.ds(c0, SLC_TK), :], vt, carry[g])
                     for g in range(G))

    def slc_body(tp, carry):
        for u in range(SLC_UNROLL):
            carry = slc_tile(tp * SLC_UNROLL + u, carry)
        return carry

    n_tiles = q0 // SLC_TK + 1
    slc = lax.fori_loop(0, (n_tiles + SLC_UNROLL - 1) // SLC_UNROLL, slc_body, (init,) * G)

    k0 = pl.multiple_of(jnp.maximum(q0 - WINDOW, 0), LANE)
    kt = kvw_ref[pl.ds(k0, WIN_KEYS), :].astype(BF16)
    c0 = pl.multiple_of(WINDOW - (q0 - k0), LANE)
    ch = k0 // LANE
    vt = jnp.concatenate([vwt_ref[ch + c] for c in range(WIN_KEYS // LANE)], axis=1)
    o_wins = []
    for g in range(G):
        s = _dot_nt(kt, q_pads[g]) + bwin_ref[g, pl.ds(c0, WIN_KEYS), :]
        e = jnp.exp2(s - jnp.max(s, axis=0, keepdims=True))
        o_wins.append(jnp.dot(vt, e.astype(BF16), preferred_element_type=F32) / jnp.sum(e, axis=0, keepdims=True))

    gates_t = gate_ref[...].T
    grow = lax.broadcasted_iota(jnp.int32, (GATE_PAD, Q), 0)
    for g in range(G):
        def gate_row(branch):
            rows = [jnp.sum(jnp.where(grow == branch * N_HEADS + g * R + r, gates_t, 0.0), axis=0, keepdims=True)
                    for r in range(R)]
            return jnp.concatenate(rows, axis=1)

        o = (gate_row(0) * o_cmps[g] + gate_row(1) * (slc[g][2] / slc[g][1])
             + gate_row(2) * o_wins[g])
        o = o[g * HEAD_DIM:(g + 1) * HEAD_DIM]
        for pair in range(R // 2):
            two = jnp.concatenate([o[:, (2 * pair) * Q:(2 * pair + 1) * Q],
                                   o[:, (2 * pair + 1) * Q:(2 * pair + 2) * Q]], axis=0)
            c = (g * R // 2 + pair) * LANE
            o_ref[:, c:c + LANE] = two.T


def _prompt_tables(tbl_flat, T):
    nqb = T // Q_BLOCK
    nc = T // CMP_STRIDE
    qi = jnp.arange(Q_BLOCK, dtype=jnp.int32)
    q_pos = jnp.arange(nqb, dtype=jnp.int32)[:, None, None] * Q_BLOCK + qi[None, :, None]
    c_end = jnp.arange(nc, dtype=jnp.int32) * CMP_STRIDE + (CMP_BLOCK - 1)
    big = 1 << 30
    bcmp = _bias_tables(tbl_flat, jnp.swapaxes(q_pos - c_end[None, None, :], 1, 2), big, True, LOG2E)

    def toeplitz(c_off, nrows, hi):
        dist = qi[None, None, :] + c_off - jnp.arange(nrows, dtype=jnp.int32)[None, :, None]
        return _bias_tables(tbl_flat, dist, hi, True, LOG2E)[:, 0]

    bslc = toeplitz(SLC_C, SLC_C + SLC_UNROLL * SLC_TK, big)
    bwin = toeplitz(WINDOW, WINDOW + WIN_KEYS, WINDOW)
    keys = jnp.arange(T, dtype=jnp.int32)
    et = (keys[:, None] // SLC_BLOCK == jnp.arange(LANE, dtype=jnp.int32)[None, :]).astype(BF16)
    c_start = jnp.arange(nc, dtype=jnp.int32) * CMP_STRIDE
    s_start = jnp.arange(T // SLC_BLOCK, dtype=jnp.int32) * SLC_BLOCK
    ovt = ((c_start[None, :] < s_start[:, None] + SLC_BLOCK)
           & (c_start[None, :] + CMP_BLOCK > s_start[:, None])).astype(BF16)
    return bcmp, bslc, bwin, et, ovt


def _nsa_prompt_call(q, gates, kc, vct, kvs, vst, kvw, vwt, tables, B, T):
    bcmp, bslc, bwin, et, ovt = tables
    nqb = T // Q_BLOCK
    nc = T // CMP_STRIDE
    R = GQA_GROUP
    seq = pl.BlockSpec((T, KV_W), lambda b, i: (b, 0))
    seq_t = pl.BlockSpec((T // LANE, KV_W, LANE), lambda b, i: (b, 0, 0))
    return pl.pallas_call(
        _nsa_prompt_kernel,
        grid=(B, nqb),
        in_specs=[
            pl.BlockSpec((Q_BLOCK, NSA_W), lambda b, i: (b * nqb + i, 0)),
            pl.BlockSpec((Q_BLOCK, GATE_PAD), lambda b, i: (b * nqb + i, 0)),
            pl.BlockSpec((1, nc, KV_W), lambda b, i: (b, 0, 0)),
            pl.BlockSpec((1, KV_W, nc), lambda b, i: (b, 0, 0)),
            seq, seq_t, seq, seq_t,
            pl.BlockSpec((KV_HEADS, 1, nc, R * Q_BLOCK), lambda b, i: (0, i, 0, 0)),
            pl.BlockSpec(bslc.shape, lambda b, i: (0, 0, 0)),
            pl.BlockSpec(bwin.shape, lambda b, i: (0, 0, 0)),
            pl.BlockSpec(et.shape, lambda b, i: (0, 0)),
            pl.BlockSpec(ovt.shape, lambda b, i: (0, 0)),
        ],
        out_specs=pl.BlockSpec((Q_BLOCK, NSA_W), lambda b, i: (b * nqb + i, 0)),
        out_shape=jax.ShapeDtypeStruct((B * T, NSA_W), F32),
        compiler_params=pltpu.CompilerParams(dimension_semantics=("parallel", "arbitrary"),
                                             vmem_limit_bytes=VMEM_LIMIT),
        name="nsa_prompt",
    )(q, gates, kc, vct, kvs, vst, kvw, vwt, bcmp, bslc, bwin, et, ovt)


SUB = 8
PAGES_PER_STEP = 64
UNTRANSPOSE_UNROLL = 8
SAMPLE_BATCHES = 4


def _compress_pages_kernel(xt_ref, pea_ref, peb_ref, wa_ref, wb_ref, o_ref, xk_ref, xv_ref):
    pb, w, page = xt_ref.shape
    hw = w // 2
    ch = page // CMP_STRIDE

    def untranspose(i, carry):
        for u in range(UNTRANSPOSE_UNROLL):
            p = i * UNTRANSPOSE_UNROLL + u
            xk_ref[p] = xt_ref[p, :hw, :].T
            xv_ref[p] = xt_ref[p, hw:, :].T
        return carry

    lax.fori_loop(0, pb // UNTRANSPOSE_UNROLL, untranspose, 0)
    a = jnp.zeros((pb * ch, w), F32)
    b = jnp.zeros((pb * ch, w), F32)
    for t in range(CMP_STRIDE):
        x = jnp.concatenate([ref[:, pl.ds(t, ch, stride=CMP_STRIDE), :].reshape(pb * ch, hw)
                             for ref in (xk_ref, xv_ref)], axis=1)
        cols = slice(t * w, (t + 1) * w)
        a = a + jnp.dot((x + pea_ref[:, cols]).astype(BF16), wa_ref[cols, :], preferred_element_type=F32)
        b = b + jnp.dot((x + peb_ref[:, cols]).astype(BF16), wb_ref[cols, :], preferred_element_type=F32)
    o_ref[...] = jnp.concatenate([a, b], axis=1).reshape(pb, ch, 2 * w)


def _compress_pages(pages, layer, n_phys, cw):
    _, w, page = pages.shape
    ch = page // CMP_STRIDE
    steps = n_phys // PAGES_PER_STEP
    full = lambda a: pl.BlockSpec(a.shape, lambda i: (0, 0))
    return pl.pallas_call(
        _compress_pages_kernel,
        grid=(steps,),
        in_specs=[pl.BlockSpec((PAGES_PER_STEP, w, page), lambda i: (layer * steps + i, 0, 0))] + [full(a) for a in cw],
        out_specs=pl.BlockSpec((PAGES_PER_STEP, ch, 4 * KV_W), lambda i: (i, 0, 0)),
        out_shape=jax.ShapeDtypeStruct((n_phys, ch, 4 * KV_W), F32),
        scratch_shapes=[pltpu.VMEM((PAGES_PER_STEP, page, w // 2), F32)] * 2,
        compiler_params=pltpu.CompilerParams(dimension_semantics=("parallel",), vmem_limit_bytes=VMEM_LIMIT),
        name="compress_pages",
    )(pages, *cw)


def _nsa_sample_kernel(pt_ref, q_ref, gate_ref, ks_new_ref, kw_new_ref, *rest, n_pages, past, dec_seq):
    n = SAMPLE_BATCHES * n_pages
    ab_refs, slc_refs = rest[:n], rest[n:2 * n]
    win_ref, bcmp_ref, bslc_ref, bwin_ref, et_ref, ov_ref, o_ref, win_out_ref = rest[2 * n:]
    for bb in range(SAMPLE_BATCHES):
        one = lambda ref: ref.at[pl.ds(bb, 1)]
        pages = slice(bb * n_pages, (bb + 1) * n_pages)
        _nsa_sample_one(one(q_ref), one(gate_ref), one(ks_new_ref), one(kw_new_ref), ab_refs[pages], slc_refs[pages],
                        one(win_ref), bcmp_ref, bslc_ref, bwin_ref, et_ref, ov_ref, one(o_ref), one(win_out_ref),
                        past, dec_seq)


def _nsa_sample_one(q_ref, gate_ref, ks_new_ref, kw_new_ref, ab_refs, slc_refs, win_ref, bcmp_ref, bslc_ref, bwin_ref,
                    et_ref, ov_ref, o_ref, win_out_ref, past, dec_seq):
    R, G = GQA_GROUP, KV_HEADS
    rows = R * G * SUB
    half = lax.broadcasted_iota(jnp.int32, (SUB, LANE), 1) // HEAD_DIM

    qb = q_ref[0] * (HEAD_DIM ** -0.5)
    slabs = []
    for r in range(R):
        for g in range(G):
            h = g * R + r
            slab = qb[:, (h // 2) * LANE:(h // 2 + 1) * LANE]
            slabs.append(jnp.where(half == g, _half_select(slab, h % 2, half), 0.0))
    q_pad = jnp.concatenate(slabs, axis=0).astype(BF16)

    ab = jnp.concatenate([r_[0] for r_ in ab_refs], axis=0)
    nc = ab.shape[0]
    kcvc = ab[:, :2 * KV_W] + pltpu.roll(ab[:, 2 * KV_W:], nc - 1, axis=0)
    kc = kcvc[:, :KV_W].astype(BF16)
    vc = kcvc[:, KV_W:].astype(BF16)
    s = _dot_nt(q_pad, kc) + bcmp_ref[...]
    e = jnp.exp(s - jnp.max(s, axis=-1, keepdims=True))
    p = e / jnp.sum(e, axis=-1, keepdims=True)
    row_pos = past + lax.broadcasted_iota(jnp.int32, (rows, 1), 0) % SUB
    p = jnp.where(row_pos >= CMP_BLOCK - 1, p, 0.0)
    pb = p.astype(BF16)
    o_cmp = jnp.dot(pb, vc, preferred_element_type=F32)

    ov = ov_ref[...]
    imp = jnp.dot(pb[:G * SUB], ov, preferred_element_type=F32)
    for r in range(1, R):
        imp = imp + jnp.dot(pb[r * G * SUB:(r + 1) * G * SUB], ov, preferred_element_type=F32)
    jidx = lax.broadcasted_iota(jnp.int32, (G * SUB, LANE), 1)
    cur = (past + lax.broadcasted_iota(jnp.int32, (G * SUB, LANE), 0) % SUB) // SLC_BLOCK
    forced = (jidx == 0) | (jidx == cur) | (jidx == cur - 1)
    score = jnp.where(forced, FORCE, jnp.where(jidx <= cur, imp, NEG))
    n_blocks = (past + dec_seq + SLC_BLOCK - 1) // SLC_BLOCK
    rank = jnp.zeros((G * SUB, LANE), F32)
    for jp in range(n_blocks):
        col = score[:, jp:jp + 1]
        rank = rank + jnp.where(jidx > jp, jnp.where(col >= score, 1.0, 0.0), jnp.where(col > score, 1.0, 0.0))
    selneg = jnp.where(rank < min(SLC_TOPK, n_blocks), 0.0, NEG).astype(BF16)
    lhs = jnp.concatenate([q_pad, jnp.concatenate([selneg] * R, axis=0)], axis=1)

    def new_cols(ref):
        x = ref[0]
        xt = jnp.concatenate([x, jnp.zeros((LANE - SUB, 2 * KV_W), F32)], axis=0).T
        return xt, xt[:KV_W].astype(BF16), xt[KV_W:].astype(BF16)

    def attend(lhs_, kt_tiles, vt_tiles, bias, ext):
        parts = []
        for t, kt in enumerate(kt_tiles):
            rhs = jnp.concatenate([kt, et_ref[:, t * LANE:(t + 1) * LANE]], axis=0) if ext else kt
            parts.append(jnp.dot(lhs_, rhs, preferred_element_type=F32))
        s_ = jnp.concatenate(parts, axis=1) + bias
        e_ = jnp.exp(s_ - jnp.max(s_, axis=-1, keepdims=True))
        p_ = (e_ / jnp.sum(e_, axis=-1, keepdims=True)).astype(BF16)
        o_ = _dot_nt(p_[:, :LANE], vt_tiles[0])
        for t in range(1, len(vt_tiles)):
            o_ = o_ + _dot_nt(p_[:, t * LANE:(t + 1) * LANE], vt_tiles[t])
        return o_

    _, kt_new, vt_new = new_cols(ks_new_ref)
    kt_tiles = [r_[0, :KV_W, :].astype(BF16) for r_ in slc_refs] + [kt_new]
    vt_tiles = [r_[0, KV_W:, :].astype(BF16) for r_ in slc_refs] + [vt_new]
    o_slc = attend(lhs, kt_tiles, vt_tiles, bslc_ref[...], True)

    win_new_t, kt_new, vt_new = new_cols(kw_new_ref)
    wb = win_ref.shape[2]
    kt_tiles = [win_ref[0, :KV_W, t * LANE:(t + 1) * LANE].astype(BF16) for t in range(wb // LANE)] + [kt_new]
    vt_tiles = [win_ref[0, KV_W:, t * LANE:(t + 1) * LANE].astype(BF16) for t in range(wb // LANE)] + [vt_new]
    o_win = attend(q_pad, kt_tiles, vt_tiles, bwin_ref[...], False)

    gates = gate_ref[0]
    glane = lax.broadcasted_iota(jnp.int32, (SUB, LANE), 1)

    def gate_col(branch):
        cols = [jnp.sum(jnp.where(glane == branch * N_HEADS + g * R + r, gates, 0.0), axis=-1, keepdims=True)
                for r in range(R) for g in range(G)]
        return jnp.concatenate(cols, axis=0)

    o = gate_col(0) * o_cmp + gate_col(1) * o_slc + gate_col(2) * o_win
    for g in range(G):
        for pair in range(R // 2):
            r0 = ((2 * pair) * G + g) * SUB
            r1 = ((2 * pair + 1) * G + g) * SUB
            lo = _half_select(o[r0:r0 + SUB], g, half)
            hi = _half_select(o[r1:r1 + SUB], g, half)
            c = (g * R // 2 + pair) * LANE
            o_ref[0, :, c:c + LANE] = jnp.where(half == 0, lo, hi)

    shifted = pltpu.roll(win_ref[0], wb - dec_seq, axis=1)
    win_out_ref[0, :, :wb - LANE] = shifted[:, :wb - LANE]
    klane = lax.broadcasted_iota(jnp.int32, (2 * KV_W, LANE), 1)
    win_out_ref[0, :, wb - LANE:] = jnp.where(klane >= LANE - dec_seq, pltpu.roll(win_new_t, LANE - dec_seq, axis=1),
                                              shifted[:, wb - LANE:])


def _sample_tables(tbl_flat, past, dec_seq, wb):
    q_pos = past + jnp.arange(SUB, dtype=jnp.int32)
    nc = past // CMP_STRIDE
    c_end = jnp.arange(nc, dtype=jnp.int32) * CMP_STRIDE + (CMP_BLOCK - 1)
    slc_keys = jnp.arange(past + LANE, dtype=jnp.int32)
    win_pos = jnp.concatenate([past - wb + jnp.arange(wb, dtype=jnp.int32), past + jnp.arange(LANE, dtype=jnp.int32)])
    big = 1 << 30

    def table(key_pos, hi):
        t = _bias_tables(tbl_flat, (q_pos[:, None] - key_pos[None, :])[None], hi)
        c = t.shape[-1]
        return t.reshape(KV_HEADS, GQA_GROUP, SUB, c).transpose(1, 0, 2, 3).reshape(GQA_GROUP * KV_HEADS * SUB, c)

    bcmp = table(c_end, big)
    bslc = table(slc_keys, big)
    bwin = table(win_pos, WINDOW)
    def kill_pad(b):
        lane = jnp.arange(b.shape[1]) - (b.shape[1] - LANE)
        return jnp.where((lane >= dec_seq)[None, :], NEG, b)
    bslc, bwin = kill_pad(bslc), kill_pad(bwin)
    et = (slc_keys[None, :] // SLC_BLOCK == jnp.arange(LANE, dtype=jnp.int32)[:, None]).astype(BF16)
    c_start = jnp.arange(nc, dtype=jnp.int32) * CMP_STRIDE
    s_start = jnp.arange(LANE, dtype=jnp.int32) * SLC_BLOCK
    ov = ((c_start[:, None] < s_start[None, :] + SLC_BLOCK)
          & (c_start[:, None] + CMP_BLOCK > s_start[None, :])).astype(BF16)
    return bcmp, bslc, bwin, et, ov


def _nsa_sample_call(page_table_flat, q, gates, ks_new, kw_new, ab_pages, slc_pages, win_buf, tables, layer, n_phys,
                     dec_seq):
    bcmp, bslc, bwin, et, ov = tables
    B = q.shape[0]
    n_pages = page_table_flat.shape[0] // B
    page = slc_pages.shape[2]
    past = n_pages * page
    wb = win_buf.shape[2]
    nb = SAMPLE_BATCHES
    row3 = lambda a: pl.BlockSpec((nb,) + a.shape[1:], lambda b, pt: (b, 0, 0))
    full = lambda a: pl.BlockSpec(a.shape, lambda b, pt: (0, 0))
    slots = [(bb, p) for bb in range(nb) for p in range(n_pages)]
    ab_specs = [pl.BlockSpec((1,) + ab_pages.shape[1:],
                             lambda b, pt, bb=bb, p=p: (pt[(b * nb + bb) * n_pages + p], 0, 0)) for bb, p in slots]
    slc_specs = [pl.BlockSpec((1,) + slc_pages.shape[1:],
                              lambda b, pt, bb=bb, p=p: (layer * n_phys + pt[(b * nb + bb) * n_pages + p], 0, 0))
                 for bb, p in slots]
    return pl.pallas_call(
        functools.partial(_nsa_sample_kernel, n_pages=n_pages, past=past, dec_seq=dec_seq),
        grid_spec=pltpu.PrefetchScalarGridSpec(
            num_scalar_prefetch=1,
            grid=(B // nb,),
            in_specs=[row3(q), row3(gates), row3(ks_new), row3(kw_new)] + ab_specs + slc_specs
                     + [pl.BlockSpec((nb, 2 * KV_W, wb), lambda b, pt: (layer * (B // nb) + b, 0, 0)),
                        full(bcmp), full(bslc), full(bwin), full(et), full(ov)],
            out_specs=[pl.BlockSpec((nb, SUB, NSA_W), lambda b, pt: (b, 0, 0)),
                       pl.BlockSpec((nb, 2 * KV_W, wb), lambda b, pt: (b, 0, 0))],
        ),
        out_shape=[jax.ShapeDtypeStruct((B, SUB, NSA_W), F32), jax.ShapeDtypeStruct((B, 2 * KV_W, wb), F32)],
        compiler_params=pltpu.CompilerParams(dimension_semantics=("parallel",), vmem_limit_bytes=VMEM_LIMIT),
        name="nsa_sample",
    )(page_table_flat, q, gates, ks_new, kw_new, *([ab_pages] * len(slots)), *([slc_pages] * len(slots)), win_buf,
      bcmp, bslc, bwin, et, ov)


CONV_HALO = 32
POOL_HALO = 16
MIX_ROWS = 512


def _mix_kernel(glu_ref, glu_halo_ref, glu_hist_ref, u_ref, u_halo_ref, u_hist_ref, dw_ref, cb_ref, lng_ref, lnb_ref,
                wp_ref, ps_ref, conv_ref, pool_ref, xs_ref, us_ref, y_ref, d_ref, *, pos0, ns):
    j = pl.program_id(1)
    tq = glu_ref.shape[0] // ns
    pos = pos0 + j * tq + lax.broadcasted_iota(jnp.int32, (tq, 1), 0)
    grp = lax.broadcasted_iota(jnp.int32, (tq, POOL_W), 1) // POOL_GROUP_W

    def one_sequence(s, rows):
        @pl.when(j == 0)
        def _():
            xs_ref[:CONV_HALO, :] = glu_hist_ref[s]
            us_ref[:POOL_HALO, :] = u_hist_ref[s]

        @pl.when(j > 0)
        def _():
            xs_ref[:CONV_HALO, :] = glu_halo_ref[...]
            us_ref[:POOL_HALO, :] = u_halo_ref[...]

        u = u_ref[rows, :]
        xs_ref[CONV_HALO:, :] = glu_ref[rows, :]
        us_ref[POOL_HALO:, :] = u

        base = CONV_HALO - CONV_BUF
        y = xs_ref[base:base + tq, :] * dw_ref[0:1, :]
        for k in range(1, CONV_WIDTH):
            y = y + xs_ref[base + k:base + k + tq, :] * dw_ref[k:k + 1, :]
        y_ref[rows, :] = y

        run = u
        d = jnp.zeros_like(u)
        done = 1
        for gi, w in enumerate(POOL_WINDOWS):
            for back in range(done, w):
                run = run + us_ref[POOL_HALO - back:POOL_HALO - back + tq, :]
            done = w
            cnt = jnp.minimum(w, pos + 1).astype(F32)
            d = jnp.where(grp == gi, run / cnt - u, d)
        d_ref[rows, :] = d

    if ns == 1:
        one_sequence(0, slice(None))
    else:
        def body(s, carry):
            one_sequence(s, pl.ds(pl.multiple_of(s * tq, SUB), tq))
            return carry

        lax.fori_loop(0, ns, body, 0)

    y = y_ref[...] + cb_ref[...]
    mu = jnp.mean(y, axis=-1, keepdims=True)
    yc = y - mu
    var = jnp.mean(yc * yc, axis=-1, keepdims=True)
    conv_ref[...] = jax.nn.silu(yc * lax.rsqrt(var + EPS) * lng_ref[...] + lnb_ref[...])
    yp = jnp.dot(d_ref[...].astype(BF16), wp_ref[...], preferred_element_type=F32)
    pool_ref[...] = yp * ps_ref[...]


def _mix(glu, u, glu_hist, u_hist, dw, cb, lng, lnb, wp_bd, ps, B, T, tq, pos0):
    nt = T // tq
    ns = min(B, MIX_ROWS // tq) if nt == 1 else 1
    row = pl.BlockSpec((ns * tq, CONV_W), lambda b, j: (b * nt + j, 0))
    full = lambda a: pl.BlockSpec(a.shape, lambda b, j: (0, 0))

    def halo(h):
        return pl.BlockSpec((h, CONV_W), lambda b, j: (jnp.maximum(b * (T // h) + j * (tq // h) - 1, 0), 0))

    hist = lambda h: pl.BlockSpec((ns, h, CONV_W), lambda b, j: (b, 0, 0))
    if nt == 1:
        glu_halo_arr, u_halo_arr = glu_hist.reshape(-1, CONV_W), u_hist.reshape(-1, POOL_W)
        halo_c = pl.BlockSpec((CONV_HALO, CONV_W), lambda b, j: (0, 0))
        halo_p = pl.BlockSpec((POOL_HALO, POOL_W), lambda b, j: (0, 0))
    else:
        glu_halo_arr, u_halo_arr = glu, u
        halo_c, halo_p = halo(CONV_HALO), halo(POOL_HALO)
    return pl.pallas_call(
        functools.partial(_mix_kernel, pos0=pos0, ns=ns),
        grid=(B // ns, nt),
        in_specs=[row, halo_c, hist(CONV_HALO), row, halo_p, hist(POOL_HALO),
                  full(dw), full(cb), full(lng), full(lnb), full(wp_bd), full(ps)],
        out_specs=[row, row],
        out_shape=[jax.ShapeDtypeStruct((B * T, CONV_W), F32), jax.ShapeDtypeStruct((B * T, POOL_W), F32)],
        scratch_shapes=[pltpu.VMEM((CONV_HALO + tq, CONV_W), F32), pltpu.VMEM((POOL_HALO + tq, POOL_W), F32),
                        pltpu.VMEM((ns * tq, CONV_W), F32), pltpu.VMEM((ns * tq, POOL_W), F32)],
        compiler_params=pltpu.CompilerParams(dimension_semantics=("parallel", "parallel"),
                                             vmem_limit_bytes=VMEM_LIMIT),
        name="conv_pool",
    )(glu, glu_halo_arr, glu_hist, u, u_halo_arr, u_hist, dw, cb, lng, lnb, wp_bd, ps)


def _layer(x2d, B, T, l, P, glu_hist, u_hist, pos0, nsa_fn, final, emit_vt):
    n = B * T
    glu, u, q, kvc, kvs, kvw, gates, *vts = _inproj(x2d, P['norm1'][l], P['w_in'][l], min(n, 512), T, emit_vt)
    conv_o, pool_o = _mix(glu, u, glu_hist, u_hist, P['conv_dw'][l], P['conv_b'][l], P['conv_ln_g'][l],
                          P['conv_ln_b'][l], P['pool_w'][l], P['pool_scale'][l], B, T, min(T, MIX_ROWS), pos0)
    nsa_o, extra = nsa_fn(q, gates, kvc, kvs, kvw, *vts[:2])
    y = _ffn(x2d, conv_o, pool_o, nsa_o, P['w_out'][l], P['norm2'][l], P['w_gu'][l], P['w_down'][l],
             P['final_norm'], min(n, 512), D_FF, final)
    return y, glu, u, kvc, kvs, kvw, extra, vts[2:]


def kernel(x_prompt, x_sample, cache_cmp_kv, cache_slc_kv, cache_win_kv, state_conv, state_pool, page_table, rel_bias, norm1, w_in, conv_dw, conv_b, conv_ln_g, conv_ln_b, pool_w, pool_scale, cmp_pe_k, cmp_wk, cmp_pe_v, cmp_wv, w_out, norm2, w_gu, w_down, final_norm):
    depth = w_in.shape[0]
    Bp, T, _ = x_prompt.shape
    Bs, S, _ = x_sample.shape
    n_phys, page = cache_cmp_kv.shape[1:3]
    n_pages = page_table.shape[1]
    past = n_pages * page
    wb = cache_win_kv.shape[2]
    tbl_flat = rel_bias.reshape(NUM_BUCKETS, KV_HEADS, GQA_GROUP).transpose(1, 2, 0).reshape(-1)
    eye = jnp.eye(len(POOL_WINDOWS), dtype=F32)
    row = lambda a: a[:, None, :]
    P = dict(
        norm1=row(norm1), norm2=row(norm2), final_norm=final_norm[None, :],
        w_in=jnp.pad(w_in, ((0, 0), (0, 0), (0, IN_W_PAD - IN_W))).astype(BF16),
        w_out=w_out.astype(BF16), w_gu=w_gu.astype(BF16), w_down=w_down.astype(BF16),
        conv_dw=conv_dw, conv_b=row(conv_b), conv_ln_g=row(conv_ln_g), conv_ln_b=row(conv_ln_b),
        pool_w=jnp.einsum('lgce,gh->lgche', pool_w, eye).reshape(depth, POOL_W, POOL_W).astype(BF16),
        pool_scale=row(pool_scale))
    ptables = _prompt_tables(tbl_flat, T)
    stables = _sample_tables(tbl_flat, past, S, wb)
    cmp_pages = jnp.swapaxes(cache_cmp_kv.reshape(depth * n_phys, page, 2 * KV_W), 1, 2)
    slc_pages = jnp.swapaxes(cache_slc_kv.reshape(depth * n_phys, page, 2 * KV_W), 1, 2)
    win_bufs = jnp.swapaxes(cache_win_kv.reshape(depth * Bs, wb, 2 * KV_W), 1, 2)
    pt_flat = page_table.reshape(-1)

    xp = x_prompt.reshape(Bp * T, D_MODEL)
    xs = jnp.pad(x_sample, ((0, 0), (0, SUB - S), (0, 0))).reshape(Bs * SUB, D_MODEL)
    zc = jnp.zeros((Bp, CONV_HALO, CONV_W), F32)
    zp = jnp.zeros((Bp, POOL_HALO, POOL_W), F32)
    kv5 = lambda a, b, t: a.reshape(b, t, 2, KV_HEADS, HEAD_DIM)
    outs_p, outs_s = [], []
    for l in range(depth):
        final = l == depth - 1
        cw = _compress_weights(cmp_pe_k[l], cmp_wk[l], cmp_pe_v[l], cmp_wv[l])

        def nsa_p(q, gates, kvc, kvs, kvw, vst, vwt):
            kc, vct = _compress_call(kvc.reshape(Bp, T // CMP_STRIDE, CMP_STRIDE * 2 * KV_W), cw)
            return _nsa_prompt_call(q, gates, kc, vct, kvs, vst, kvw, vwt, ptables, Bp, T), None

        xp, glu, u, _, _, _, _, kvts = _layer(xp, Bp, T, l, P, zc, zp, 0, nsa_p, final, True)
        from_t = lambda a: kv5(jnp.swapaxes(a, 1, 2), Bp, a.shape[2])
        outs_p.append((from_t(kvts[0]), from_t(kvts[1]), from_t(kvts[2][:, :, T - min(WINDOW, T):]),
                       glu.reshape(Bp, T, CONV_W)[:, T - CONV_BUF:], u.reshape(Bp, T, POOL_W)[:, T - POOL_BUF:]))

        def nsa_s(q, gates, kvc, kvs, kvw):
            ab = _compress_pages(cmp_pages, l, n_phys, cw)
            r3 = lambda a: a.reshape(Bs, SUB, a.shape[-1])
            o, new_win = _nsa_sample_call(pt_flat, r3(q), r3(gates), r3(kvs), r3(kvw), ab, slc_pages, win_bufs,
                                          stables, l, n_phys, S)
            return o.reshape(Bs * SUB, NSA_W), new_win

        glu_hist = jnp.pad(state_conv[l], ((0, 0), (CONV_HALO - CONV_BUF, 0), (0, 0)))
        u_hist = jnp.pad(state_pool[l], ((0, 0), (POOL_HALO - POOL_BUF, 0), (0, 0)))
        xs, glu, u, kvc, kvs, kvw, new_win, _ = _layer(xs, Bs, SUB, l, P, glu_hist, u_hist, past, nsa_s, final, False)
        new_conv = jnp.concatenate([state_conv[l], glu.reshape(Bs, SUB, CONV_W)[:, :S]], axis=1)[:, -CONV_BUF:]
        new_pool = jnp.concatenate([state_pool[l], u.reshape(Bs, SUB, POOL_W)[:, :S]], axis=1)[:, -POOL_BUF:]
        outs_s.append((kv5(kvc, Bs, SUB)[:, :S], kv5(kvs, Bs, SUB)[:, :S], kv5(jnp.swapaxes(new_win, 1, 2), Bs, wb), new_conv, new_pool))
    st = lambda outs, i: jnp.stack([o[i] for o in outs])
    return (xp.reshape(Bp, T, D_MODEL), xs.reshape(Bs, SUB, D_MODEL)[:, :S],
            st(outs_p, 0), st(outs_p, 1), st(outs_p, 2), st(outs_p, 3), st(outs_p, 4),
            st(outs_s, 0), st(outs_s, 1), st(outs_s, 2), st(outs_s, 3), st(outs_s, 4))
```

```python
import functools
import math

import jax
import jax.numpy as jnp
from jax import lax
from jax.experimental import pallas as pl
from jax.experimental.pallas import tpu as pltpu

D_MODEL = 1024
CONV_W = 256
POOL_W = 256
N_HEADS = 8
HEAD_DIM = 64
NSA_W = N_HEADS * HEAD_DIM
KV_HEADS = 2
GQA_GROUP = N_HEADS // KV_HEADS
KV_W = KV_HEADS * HEAD_DIM
CONV_WIDTH = 31
CONV_BUF = CONV_WIDTH - 1
POOL_WINDOWS = (2, 4, 8, 16)
POOL_GROUP_W = POOL_W // len(POOL_WINDOWS)
POOL_BUF = max(POOL_WINDOWS) - 1
CMP_STRIDE = 16
CMP_BLOCK = 2 * CMP_STRIDE
SLC_BLOCK = 64
SLC_TOPK = 16
WINDOW = 512
Q_BLOCK = 128
NUM_BUCKETS = 32
MAX_DISTANCE = 128
D_FF = 2816
EPS = 1e-6
NEG = -1e30
LOG2E = math.log2(math.e)
FORCE = 1e4

OFF_POOL = 2 * CONV_W
OFF_Q = OFF_POOL + POOL_W
OFF_KV = OFF_Q + NSA_W
OFF_GATE = OFF_KV + 6 * KV_W
IN_W = OFF_GATE + 3 * N_HEADS
LANE = 128
IN_W_PAD = -(-IN_W // LANE) * LANE
GATE_PAD = IN_W_PAD - OFF_GATE

VMEM_LIMIT = 56 * 1024 * 1024

F32 = jnp.float32
BF16 = jnp.bfloat16


def _rms(x, g):
    return x * lax.rsqrt(jnp.mean(x * x, axis=-1, keepdims=True) + EPS) * g


def _inproj_kernel(x_ref, g_ref, w_ref, glu_ref, u_ref, q_ref, kvc_ref, kvs_ref, kvw_ref, gate_ref, *vt_refs):
    h = _rms(x_ref[...], g_ref[...]).astype(BF16)
    z = jnp.dot(h, w_ref[...], preferred_element_type=F32)
    glu_ref[...] = z[:, :CONV_W] * jax.nn.sigmoid(z[:, CONV_W:OFF_POOL])
    u_ref[...] = z[:, OFF_POOL:OFF_Q]
    q_ref[...] = z[:, OFF_Q:OFF_KV]
    kvc_ref[...] = z[:, OFF_KV:OFF_KV + 2 * KV_W]
    kvs_ref[...] = z[:, OFF_KV + 2 * KV_W:OFF_KV + 4 * KV_W]
    kvw_ref[...] = z[:, OFF_KV + 4 * KV_W:OFF_GATE]
    gate_ref[...] = jax.nn.sigmoid(z[:, OFF_GATE:])
    if vt_refs:
        for vt_ref, off in zip(vt_refs[:2], (OFF_KV + 3 * KV_W, OFF_KV + 5 * KV_W)):
            for c in range(vt_ref.shape[0]):
                vt_ref[c] = z[c * LANE:(c + 1) * LANE, off:off + KV_W].T.astype(BF16)
        for b, kvt_ref in enumerate(vt_refs[2:5]):
            kvt_ref[0] = z[:, OFF_KV + 2 * b * KV_W:OFF_KV + 2 * (b + 1) * KV_W].T
        for kb_ref, off in zip(vt_refs[5:], (OFF_KV + 2 * KV_W, OFF_KV + 4 * KV_W)):
            kb_ref[...] = z[:, off:off + KV_W].astype(BF16)


def _inproj(x2d, g, w_pad, tm, seq_len, emit_vt):
    n = x2d.shape[0]
    nt = max(seq_len // tm, 1)
    row = lambda w: pl.BlockSpec((tm, w), lambda i: (i, 0))
    full = lambda a: pl.BlockSpec(a.shape, lambda i: (0, 0))
    widths = (CONV_W, POOL_W, NSA_W, 2 * KV_W, 2 * KV_W, 2 * KV_W, GATE_PAD)
    out_specs = [row(w) for w in widths]
    out_shape = [jax.ShapeDtypeStruct((n, w), F32) for w in widths]
    if emit_vt:
        out_specs += [pl.BlockSpec((tm // LANE, KV_W, LANE), lambda i: (i, 0, 0))] * 2
        out_shape += [jax.ShapeDtypeStruct((n // LANE, KV_W, LANE), BF16)] * 2
        out_specs += [pl.BlockSpec((1, 2 * KV_W, tm), lambda i: (i // nt, 0, i % nt))] * 3
        out_shape += [jax.ShapeDtypeStruct((n // seq_len, 2 * KV_W, seq_len), F32)] * 3
        out_specs += [row(KV_W)] * 2
        out_shape += [jax.ShapeDtypeStruct((n, KV_W), BF16)] * 2
    return pl.pallas_call(
        _inproj_kernel,
        grid=(n // tm,),
        in_specs=[row(D_MODEL), full(g), full(w_pad)],
        out_specs=out_specs,
        out_shape=out_shape,
        compiler_params=pltpu.CompilerParams(dimension_semantics=("parallel",), vmem_limit_bytes=VMEM_LIMIT),
        name="inproj",
    )(x2d, g, w_pad)


def _ffn_kernel(x_ref, conv_ref, pool_ref, nsa_ref, wo_ref, g2_ref, wg_ref, wu_ref, wd_ref, gf_ref,
                y_ref, x1_ref, h2_ref, acc_ref, *, final_norm):
    j = pl.program_id(1)

    @pl.when(j == 0)
    def _():
        mix = jnp.concatenate([conv_ref[...], pool_ref[...], nsa_ref[...]], axis=-1).astype(BF16)
        x1 = x_ref[...] + jnp.dot(mix, wo_ref[...], preferred_element_type=F32)
        x1_ref[...] = x1
        h2_ref[...] = _rms(x1, g2_ref[...]).astype(BF16)
        acc_ref[...] = jnp.zeros_like(acc_ref)

    h2 = h2_ref[...]
    gate = jnp.dot(h2, wg_ref[...], preferred_element_type=F32)
    up = jnp.dot(h2, wu_ref[...], preferred_element_type=F32)
    act = (jax.nn.silu(gate) * up).astype(BF16)
    acc_ref[...] += jnp.dot(act, wd_ref[...], preferred_element_type=F32)

    @pl.when(j == pl.num_programs(1) - 1)
    def _():
        y = x1_ref[...] + acc_ref[...]
        if final_norm:
            y = _rms(y, gf_ref[...])
        y_ref[...] = y


def _ffn(x2d, conv_o, pool_o, nsa_o, w_out, g2, w_gu, w_down, gf, tm, tf, final_norm):
    n = x2d.shape[0]
    nf = D_FF // tf
    row = lambda w: pl.BlockSpec((tm, w), lambda i, j: (i, 0))
    full = lambda a: pl.BlockSpec(a.shape, lambda i, j: (0, 0))
    return pl.pallas_call(
        functools.partial(_ffn_kernel, final_norm=final_norm),
        grid=(n // tm, nf),
        in_specs=[row(D_MODEL), row(CONV_W), row(POOL_W), row(NSA_W), full(w_out), full(g2),
                  pl.BlockSpec((D_MODEL, tf), lambda i, j: (0, j)),
                  pl.BlockSpec((D_MODEL, tf), lambda i, j: (0, j + nf)),
                  pl.BlockSpec((tf, D_MODEL), lambda i, j: (j, 0)),
                  full(gf)],
        out_specs=row(D_MODEL),
        out_shape=jax.ShapeDtypeStruct((n, D_MODEL), F32),
        scratch_shapes=[pltpu.VMEM((tm, D_MODEL), F32), pltpu.VMEM((tm, D_MODEL), BF16),
                        pltpu.VMEM((tm, D_MODEL), F32)],
        compiler_params=pltpu.CompilerParams(dimension_semantics=("parallel", "arbitrary"),
                                             vmem_limit_bytes=VMEM_LIMIT),
        name="ffn",
    )(x2d, conv_o, pool_o, nsa_o, w_out, g2, w_gu, w_gu, w_down, gf)


def _bias_kernel(tbl_ref, dist_ref, out_ref, *, hi, heads_on_lanes, scale):
    g = pl.program_id(0)
    d = dist_ref[0]
    rows, cols = d.shape
    n = jnp.maximum(d, 0)
    max_exact = NUM_BUCKETS // 2
    nf = jnp.maximum(n, 1).astype(F32)
    large = max_exact + (jnp.log(nf / max_exact) / math.log(MAX_DISTANCE / max_exact)
                         * (NUM_BUCKETS - max_exact)).astype(jnp.int32)
    large = jnp.minimum(large, NUM_BUCKETS - 1)
    bucket = jnp.where(n < max_exact, n, large)
    valid = (d >= 0) & (d < hi)
    for r in range(GQA_GROUP):
        acc = jnp.zeros(d.shape, F32)
        for b in range(NUM_BUCKETS):
            acc = jnp.where(bucket == b, tbl_ref[(g * GQA_GROUP + r) * NUM_BUCKETS + b], acc)
        val = jnp.where(valid, acc * scale, NEG)
        if heads_on_lanes:
            out_ref[0, 0, :, r * cols:(r + 1) * cols] = val
        else:
            out_ref[0, 0, r * rows:(r + 1) * rows, :] = val


def _bias_tables(tbl_flat, dist, hi, heads_on_lanes=False, scale=1.0):
    nblk, R, C = dist.shape
    oshape = (R, GQA_GROUP * C) if heads_on_lanes else (GQA_GROUP * R, C)
    return pl.pallas_call(
        functools.partial(_bias_kernel, hi=hi, heads_on_lanes=heads_on_lanes, scale=scale),
        grid_spec=pltpu.PrefetchScalarGridSpec(
            num_scalar_prefetch=1,
            grid=(KV_HEADS, nblk),
            in_specs=[pl.BlockSpec((1, R, C), lambda g, i, tbl: (i, 0, 0))],
            out_specs=pl.BlockSpec((1, 1) + oshape, lambda g, i, tbl: (g, i, 0, 0)),
        ),
        out_shape=jax.ShapeDtypeStruct((KV_HEADS, nblk) + oshape, F32),
        compiler_params=pltpu.CompilerParams(dimension_semantics=("parallel", "parallel")),
        name="bias_tables",
    )(tbl_flat, dist)


def _compress_kernel(x_ref, pea_ref, peb_ref, wa_ref, wb_ref, k_ref, vt_ref):
    x = x_ref[0]
    a = jnp.dot((x + pea_ref[...]).astype(BF16), wa_ref[...], preferred_element_type=F32)
    b = jnp.dot((x + peb_ref[...]).astype(BF16), wb_ref[...], preferred_element_type=F32)
    kv = a + pltpu.roll(b, b.shape[0] - 1, axis=0)
    k_ref[0] = kv[:, :KV_W].astype(BF16)
    vt_ref[0] = kv[:, KV_W:].T.astype(BF16)


def _compress_weights(pe_k, wk, pe_v, wv):
    eye = jnp.eye(2 * KV_HEADS, dtype=F32)

    def half(lo):
        w = jnp.stack([wk[lo:lo + CMP_STRIDE], wk[lo:lo + CMP_STRIDE], wv[lo:lo + CMP_STRIDE], wv[lo:lo + CMP_STRIDE]], 1)
        big = jnp.einsum('lcde,cf->lcdfe', w, eye).reshape(CMP_STRIDE * 2 * KV_W, 2 * KV_W)
        pe = jnp.stack([pe_k[lo:lo + CMP_STRIDE], pe_k[lo:lo + CMP_STRIDE], pe_v[lo:lo + CMP_STRIDE], pe_v[lo:lo + CMP_STRIDE]], 1)
        return pe.reshape(1, CMP_STRIDE * 2 * KV_W), big.astype(BF16)

    pea, wa = half(0)
    peb, wb = half(CMP_STRIDE)
    return pea, peb, wa, wb


def _compress_call(x3, cw):
    B, nc, cw_in = x3.shape
    full = lambda a: pl.BlockSpec(a.shape, lambda b: (0, 0))
    return pl.pallas_call(
        _compress_kernel,
        grid=(B,),
        in_specs=[pl.BlockSpec((1, nc, cw_in), lambda b: (b, 0, 0))] + [full(a) for a in cw],
        out_specs=[pl.BlockSpec((1, nc, KV_W), lambda b: (b, 0, 0)), pl.BlockSpec((1, KV_W, nc), lambda b: (b, 0, 0))],
        out_shape=[jax.ShapeDtypeStruct((B, nc, KV_W), BF16), jax.ShapeDtypeStruct((B, KV_W, nc), BF16)],
        compiler_params=pltpu.CompilerParams(dimension_semantics=("parallel",), vmem_limit_bytes=VMEM_LIMIT),
        name="compress",
    )(x3, *cw)


SLC_TK = 1024
SLC_C = SLC_TK + LANE
SLC_UNROLL = 1
WIN_KEYS = WINDOW + Q_BLOCK


def _dot_nt(a, b):
    return lax.dot_general(a, b, (((1,), (1,)), ((), ())), preferred_element_type=F32)


def _half_select(slab, want_half, half):
    return jnp.where(half == want_half, slab, pltpu.roll(slab, HEAD_DIM, axis=1))


def _nsa_prompt_kernel(q_ref, gate_ref, kc_ref, vct_ref, kvs_ref, vst_ref, kvw_ref, vwt_ref, bcmp_ref, bslc_ref,
                       bwin_ref, et_ref, ovt_ref, o_ref):
    i = pl.program_id(1)
    Q = Q_BLOCK
    R = GQA_GROUP
    G = KV_HEADS
    q0 = i * Q
    half = lax.broadcasted_iota(jnp.int32, (Q, LANE), 1) // HEAD_DIM
    qb = q_ref[...] * (HEAD_DIM ** -0.5 * LOG2E)
    ovt = ovt_ref[...]
    ns = ovt.shape[0]
    jidx = lax.broadcasted_iota(jnp.int32, (ns, Q), 0)
    cur = (q0 + lax.broadcasted_iota(jnp.int32, (ns, Q), 1)) // SLC_BLOCK
    forced = (jidx == 0) | (jidx == cur) | (jidx == cur - 1)
    col_pos = q0 + lax.broadcasted_iota(jnp.int32, (1, R * Q), 1) % Q

    q_pads, lhss, o_cmps = [], [], []
    for g in range(G):
        slabs = []
        for r in range(R):
            h = g * R + r
            slab = qb[:, (h // 2) * LANE:(h // 2 + 1) * LANE]
            slabs.append(jnp.where(half == g, _half_select(slab, h % 2, half), 0.0))
        q_pad = jnp.concatenate(slabs, axis=0).astype(BF16)

        s = _dot_nt(kc_ref[0], q_pad) + bcmp_ref[g, 0]
        e = jnp.exp2(s - jnp.max(s, axis=0, keepdims=True))
        p = e / jnp.sum(e, axis=0, keepdims=True)
        p = jnp.where(col_pos >= CMP_BLOCK - 1, p, 0.0)
        pb = p.astype(BF16)
        o_cmps.append(jnp.dot(vct_ref[0], pb, preferred_element_type=F32))

        imp = jnp.dot(ovt, pb[:, :Q], preferred_element_type=F32)
        for r in range(1, R):
            imp = imp + jnp.dot(ovt, pb[:, r * Q:(r + 1) * Q], preferred_element_type=F32)
        score = jnp.where(forced, FORCE, jnp.where(jidx <= cur, imp, NEG))
        rank = jnp.zeros((ns, Q), F32)
        for jp in range(ns):
            row = score[jp:jp + 1, :]
            rank = rank + jnp.where(jidx > jp, jnp.where(row >= score, 1.0, 0.0), jnp.where(row > score, 1.0, 0.0))
        selneg_t = jnp.where(rank < SLC_TOPK, 0.0, NEG)
        selneg_t = jnp.concatenate([selneg_t, jnp.zeros((LANE - ns, Q), F32)], axis=0)
        selneg = selneg_t.T.astype(BF16)
        q_pads.append(q_pad)
        lhss.append(jnp.concatenate([q_pad, jnp.concatenate([selneg] * R, axis=0)], axis=1))

    def flash_step(s, vt_tile, carry):
        m, l, acc = carry
        m_new = jnp.maximum(m, jnp.max(s, axis=0, keepdims=True))
        alpha = jnp.exp2(m - m_new)
        pt = jnp.exp2(s - m_new)
        l = alpha * l + jnp.sum(pt, axis=0, keepdims=True)
        acc = alpha * acc + jnp.dot(vt_tile, pt.astype(BF16), preferred_element_type=F32)
        return m_new, l, acc

    init = (jnp.full((1, R * Q), NEG, F32), jnp.zeros((1, R * Q), F32), jnp.zeros((LANE, R * Q), F32))

    def slc_tile(t, carry):
        k0 = pl.multiple_of(t * SLC_TK, SLC_TK)
        kt = kvs_ref[pl.ds(k0, SLC_TK), :].astype(BF16)
        keys = jnp.concatenate([kt, et_ref[pl.ds(k0, SLC_TK), :]], axis=1)
        c0 = pl.multiple_of(jnp.maximum(SLC_C - (q0 - k0), 0), LANE)
        ch = t * (SLC_TK // LANE)
        vt = jnp.concatenate([vst_ref[ch + c] for c in range(SLC_TK // LANE)], axis=1)
        return tuple(flash_step(_dot_nt(keys, lhss[g]) + bslc_ref[g, pl.ds(c0, SLC_TK), :], vt, carry[g])
                     for g in range(G))

    def slc_body(tp, carry):
        for u in range(SLC_UNROLL):
            carry = slc_tile(tp * SLC_UNROLL + u, carry)
        return carry

    n_tiles = q0 // SLC_TK + 1
    slc = lax.fori_loop(0, (n_tiles + SLC_UNROLL - 1) // SLC_UNROLL, slc_body, (init,) * G)

    k0 = pl.multiple_of(jnp.maximum(q0 - WINDOW, 0), LANE)
    kt = kvw_ref[pl.ds(k0, WIN_KEYS), :].astype(BF16)
    c0 = pl.multiple_of(WINDOW - (q0 - k0), LANE)
    ch = k0 // LANE
    vt = jnp.concatenate([vwt_ref[ch + c] for c in range(WIN_KEYS // LANE)], axis=1)
    o_wins = []
    for g in range(G):
        s = _dot_nt(kt, q_pads[g]) + bwin_ref[g, pl.ds(c0, WIN_KEYS), :]
        e = jnp.exp2(s - jnp.max(s, axis=0, keepdims=True))
        o_wins.append(jnp.dot(vt, e.astype(BF16), preferred_element_type=F32) / jnp.sum(e, axis=0, keepdims=True))

    gates_t = gate_ref[...].T
    grow = lax.broadcasted_iota(jnp.int32, (GATE_PAD, Q), 0)
    for g in range(G):
        def gate_row(branch):
            rows = [jnp.sum(jnp.where(grow == branch * N_HEADS + g * R + r, gates_t, 0.0), axis=0, keepdims=True)
                    for r in range(R)]
            return jnp.concatenate(rows, axis=1)

        o = (gate_row(0) * o_cmps[g] + gate_row(1) * (slc[g][2] / slc[g][1])
             + gate_row(2) * o_wins[g])
        o = o[g * HEAD_DIM:(g + 1) * HEAD_DIM]
        for pair in range(R // 2):
            two = jnp.concatenate([o[:, (2 * pair) * Q:(2 * pair + 1) * Q],
                                   o[:, (2 * pair + 1) * Q:(2 * pair + 2) * Q]], axis=0)
            c = (g * R // 2 + pair) * LANE
            o_ref[:, c:c + LANE] = two.T


def _prompt_tables(tbl_flat, T):
    nqb = T // Q_BLOCK
    nc = T // CMP_STRIDE
    qi = jnp.arange(Q_BLOCK, dtype=jnp.int32)
    q_pos = jnp.arange(nqb, dtype=jnp.int32)[:, None, None] * Q_BLOCK + qi[None, :, None]
    c_end = jnp.arange(nc, dtype=jnp.int32) * CMP_STRIDE + (CMP_BLOCK - 1)
    big = 1 << 30
    bcmp = _bias_tables(tbl_flat, jnp.swapaxes(q_pos - c_end[None, None, :], 1, 2), big, True, LOG2E)

    def toeplitz(c_off, nrows, hi):
        dist = qi[None, None, :] + c_off - jnp.arange(nrows, dtype=jnp.int32)[None, :, None]
        return _bias_tables(tbl_flat, dist, hi, True, LOG2E)[:, 0]

    bslc = toeplitz(SLC_C, SLC_C + SLC_UNROLL * SLC_TK, big)
    bwin = toeplitz(WINDOW, WINDOW + WIN_KEYS, WINDOW)
    keys = jnp.arange(T, dtype=jnp.int32)
    et = (keys[:, None] // SLC_BLOCK == jnp.arange(LANE, dtype=jnp.int32)[None, :]).astype(BF16)
    c_start = jnp.arange(nc, dtype=jnp.int32) * CMP_STRIDE
    s_start = jnp.arange(T // SLC_BLOCK, dtype=jnp.int32) * SLC_BLOCK
    ovt = ((c_start[None, :] < s_start[:, None] + SLC_BLOCK)
           & (c_start[None, :] + CMP_BLOCK > s_start[:, None])).astype(BF16)
    return bcmp, bslc, bwin, et, ovt


def _nsa_prompt_call(q, gates, kc, vct, kvs, vst, kvw, vwt, tables, B, T):
    bcmp, bslc, bwin, et, ovt = tables
    nqb = T // Q_BLOCK
    nc = T // CMP_STRIDE
    R = GQA_GROUP
    seq = pl.BlockSpec((T, KV_W), lambda b, i: (b, 0))
    seq_t = pl.BlockSpec((T // LANE, KV_W, LANE), lambda b, i: (b, 0, 0))
    return pl.pallas_call(
        _nsa_prompt_kernel,
        grid=(B, nqb),
        in_specs=[
            pl.BlockSpec((Q_BLOCK, NSA_W), lambda b, i: (b * nqb + i, 0)),
            pl.BlockSpec((Q_BLOCK, GATE_PAD), lambda b, i: (b * nqb + i, 0)),
            pl.BlockSpec((1, nc, KV_W), lambda b, i: (b, 0, 0)),
            pl.BlockSpec((1, KV_W, nc), lambda b, i: (b, 0, 0)),
            seq, seq_t, seq, seq_t,
            pl.BlockSpec((KV_HEADS, 1, nc, R * Q_BLOCK), lambda b, i: (0, i, 0, 0)),
            pl.BlockSpec(bslc.shape, lambda b, i: (0, 0, 0)),
            pl.BlockSpec(bwin.shape, lambda b, i: (0, 0, 0)),
            pl.BlockSpec(et.shape, lambda b, i: (0, 0)),
            pl.BlockSpec(ovt.shape, lambda b, i: (0, 0)),
        ],
        out_specs=pl.BlockSpec((Q_BLOCK, NSA_W), lambda b, i: (b * nqb + i, 0)),
        out_shape=jax.ShapeDtypeStruct((B * T, NSA_W), F32),
        compiler_params=pltpu.CompilerParams(dimension_semantics=("parallel", "arbitrary"),
                                             vmem_limit_bytes=VMEM_LIMIT),
        name="nsa_prompt",
    )(q, gates, kc, vct, kvs, vst, kvw, vwt, bcmp, bslc, bwin, et, ovt)


SUB = 8
PAGES_PER_STEP = 64
UNTRANSPOSE_UNROLL = 8
SAMPLE_BATCHES = 4


def _compress_pages_kernel(xt_ref, pea_ref, peb_ref, wa_ref, wb_ref, o_ref, xk_ref, xv_ref):
    pb, w, page = xt_ref.shape
    hw = w // 2
    ch = page // CMP_STRIDE

    def untranspose(i, carry):
        for u in range(UNTRANSPOSE_UNROLL):
            p = i * UNTRANSPOSE_UNROLL + u
            xk_ref[p] = xt_ref[p, :hw, :].T
            xv_ref[p] = xt_ref[p, hw:, :].T
        return carry

    lax.fori_loop(0, pb // UNTRANSPOSE_UNROLL, untranspose, 0)
    a = jnp.zeros((pb * ch, w), F32)
    b = jnp.zeros((pb * ch, w), F32)
    for t in range(CMP_STRIDE):
        x = jnp.concatenate([ref[:, pl.ds(t, ch, stride=CMP_STRIDE), :].reshape(pb * ch, hw)
                             for ref in (xk_ref, xv_ref)], axis=1)
        cols = slice(t * w, (t + 1) * w)
        a = a + jnp.dot((x + pea_ref[:, cols]).astype(BF16), wa_ref[cols, :], preferred_element_type=F32)
        b = b + jnp.dot((x + peb_ref[:, cols]).astype(BF16), wb_ref[cols, :], preferred_element_type=F32)
    o_ref[...] = jnp.concatenate([a, b], axis=1).reshape(pb, ch, 2 * w)


def _compress_pages(pages, layer, n_phys, cw):
    _, w, page = pages.shape
    ch = page // CMP_STRIDE
    steps = n_phys // PAGES_PER_STEP
    full = lambda a: pl.BlockSpec(a.shape, lambda i: (0, 0))
    return pl.pallas_call(
        _compress_pages_kernel,
        grid=(steps,),
        in_specs=[pl.BlockSpec((PAGES_PER_STEP, w, page), lambda i: (layer * steps + i, 0, 0))] + [full(a) for a in cw],
        out_specs=pl.BlockSpec((PAGES_PER_STEP, ch, 4 * KV_W), lambda i: (i, 0, 0)),
        out_shape=jax.ShapeDtypeStruct((n_phys, ch, 4 * KV_W), F32),
        scratch_shapes=[pltpu.VMEM((PAGES_PER_STEP, page, w // 2), F32)] * 2,
        compiler_params=pltpu.CompilerParams(dimension_semantics=("parallel",), vmem_limit_bytes=VMEM_LIMIT),
        name="compress_pages",
    )(pages, *cw)


def _nsa_sample_kernel(pt_ref, q_ref, gate_ref, ks_new_ref, kw_new_ref, *rest, n_pages, past, dec_seq):
    n = SAMPLE_BATCHES * n_pages
    ab_refs, slc_refs = rest[:n], rest[n:2 * n]
    win_ref, bcmp_ref, bslc_ref, bwin_ref, et_ref, ov_ref, o_ref, win_out_ref = rest[2 * n:]
    for bb in range(SAMPLE_BATCHES):
        one = lambda ref: ref.at[pl.ds(bb, 1)]
        pages = slice(bb * n_pages, (bb + 1) * n_pages)
        _nsa_sample_one(one(q_ref), one(gate_ref), one(ks_new_ref), one(kw_new_ref), ab_refs[pages], slc_refs[pages],
                        one(win_ref), bcmp_ref, bslc_ref, bwin_ref, et_ref, ov_ref, one(o_ref), one(win_out_ref),
                        past, dec_seq)


def _nsa_sample_one(q_ref, gate_ref, ks_new_ref, kw_new_ref, ab_refs, slc_refs, win_ref, bcmp_ref, bslc_ref, bwin_ref,
                    et_ref, ov_ref, o_ref, win_out_ref, past, dec_seq):
    R, G = GQA_GROUP, KV_HEADS
    rows = R * G * SUB
    half = lax.broadcasted_iota(jnp.int32, (SUB, LANE), 1) // HEAD_DIM

    qb = q_ref[0] * (HEAD_DIM ** -0.5)
    slabs = []
    for r in range(R):
        for g in range(G):
            h = g * R + r
            slab = qb[:, (h // 2) * LANE:(h // 2 + 1) * LANE]
            slabs.append(jnp.where(half == g, _half_select(slab, h % 2, half), 0.0))
    q_pad = jnp.concatenate(slabs, axis=0).astype(BF16)

    ab = jnp.concatenate([r_[0] for r_ in ab_refs], axis=0)
    nc = ab.shape[0]
    kcvc = ab[:, :2 * KV_W] + pltpu.roll(ab[:, 2 * KV_W:], nc - 1, axis=0)
    kc = kcvc[:, :KV_W].astype(BF16)
    vc = kcvc[:, KV_W:].astype(BF16)
    s = _dot_nt(q_pad, kc) + bcmp_ref[...]
    e = jnp.exp(s - jnp.max(s, axis=-1, keepdims=True))
    p = e / jnp.sum(e, axis=-1, keepdims=True)
    row_pos = past + lax.broadcasted_iota(jnp.int32, (rows, 1), 0) % SUB
    p = jnp.where(row_pos >= CMP_BLOCK - 1, p, 0.0)
    pb = p.astype(BF16)
    o_cmp = jnp.dot(pb, vc, preferred_element_type=F32)

    ov = ov_ref[...]
    imp = jnp.dot(pb[:G * SUB], ov, preferred_element_type=F32)
    for r in range(1, R):
        imp = imp + jnp.dot(pb[r * G * SUB:(r + 1) * G * SUB], ov, preferred_element_type=F32)
    jidx = lax.broadcasted_iota(jnp.int32, (G * SUB, LANE), 1)
    cur = (past + lax.broadcasted_iota(jnp.int32, (G * SUB, LANE), 0) % SUB) // SLC_BLOCK
    forced = (jidx == 0) | (jidx == cur) | (jidx == cur - 1)
    score = jnp.where(forced, FORCE, jnp.where(jidx <= cur, imp, NEG))
    n_blocks = (past + dec_seq + SLC_BLOCK - 1) // SLC_BLOCK
    rank = jnp.zeros((G * SUB, LANE), F32)
    for jp in range(n_blocks):
        col = score[:, jp:jp + 1]
        rank = rank + jnp.where(jidx > jp, jnp.where(col >= score, 1.0, 0.0), jnp.where(col > score, 1.0, 0.0))
    selneg = jnp.where(rank < min(SLC_TOPK, n_blocks), 0.0, NEG).astype(BF16)
    lhs = jnp.concatenate([q_pad, jnp.concatenate([selneg] * R, axis=0)], axis=1)

    def new_cols(ref):
        x = ref[0]
        xt = jnp.concatenate([x, jnp.zeros((LANE - SUB, 2 * KV_W), F32)], axis=0).T
        return xt, xt[:KV_W].astype(BF16), xt[KV_W:].astype(BF16)

    def attend(lhs_, kt_tiles, vt_tiles, bias, ext):
        parts = []
        for t, kt in enumerate(kt_tiles):
            rhs = jnp.concatenate([kt, et_ref[:, t * LANE:(t + 1) * LANE]], axis=0) if ext else kt
            parts.append(jnp.dot(lhs_, rhs, preferred_element_type=F32))
        s_ = jnp.concatenate(parts, axis=1) + bias
        e_ = jnp.exp(s_ - jnp.max(s_, axis=-1, keepdims=True))
        p_ = (e_ / jnp.sum(e_, axis=-1, keepdims=True)).astype(BF16)
        o_ = _dot_nt(p_[:, :LANE], vt_tiles[0])
        for t in range(1, len(vt_tiles)):
            o_ = o_ + _dot_nt(p_[:, t * LANE:(t + 1) * LANE], vt_tiles[t])
        return o_

    _, kt_new, vt_new = new_cols(ks_new_ref)
    kt_tiles = [r_[0, :KV_W, :].astype(BF16) for r_ in slc_refs] + [kt_new]
    vt_tiles = [r_[0, KV_W:, :].astype(BF16) for r_ in slc_refs] + [vt_new]
    o_slc = attend(lhs, kt_tiles, vt_tiles, bslc_ref[...], True)

    win_new_t, kt_new, vt_new = new_cols(kw_new_ref)
    wb = win_ref.shape[2]
    kt_tiles = [win_ref[0, :KV_W, t * LANE:(t + 1) * LANE].astype(BF16) for t in range(wb // LANE)] + [kt_new]
    vt_tiles = [win_ref[0, KV_W:, t * LANE:(t + 1) * LANE].astype(BF16) for t in range(wb // LANE)] + [vt_new]
    o_win = attend(q_pad, kt_tiles, vt_tiles, bwin_ref[...], False)

    gates = gate_ref[0]
    glane = lax.broadcasted_iota(jnp.int32, (SUB, LANE), 1)

    def gate_col(branch):
        cols = [jnp.sum(jnp.where(glane == branch * N_HEADS + g * R + r, gates, 0.0), axis=-1, keepdims=True)
                for r in range(R) for g in range(G)]
        return jnp.concatenate(cols, axis=0)

    o = gate_col(0) * o_cmp + gate_col(1) * o_slc + gate_col(2) * o_win
    for g in range(G):
        for pair in range(R // 2):
            r0 = ((2 * pair) * G + g) * SUB
            r1 = ((2 * pair + 1) * G + g) * SUB
            lo = _half_select(o[r0:r0 + SUB], g, half)
            hi = _half_select(o[r1:r1 + SUB], g, half)
            c = (g * R // 2 + pair) * LANE
            o_ref[0, :, c:c + LANE] = jnp.where(half == 0, lo, hi)

    shifted = pltpu.roll(win_ref[0], wb - dec_seq, axis=1)
    win_out_ref[0, :, :wb - LANE] = shifted[:, :wb - LANE]
    klane = lax.broadcasted_iota(jnp.int32, (2 * KV_W, LANE), 1)
    win_out_ref[0, :, wb - LANE:] = jnp.where(klane >= LANE - dec_seq, pltpu.roll(win_new_t, LANE - dec_seq, axis=1),
                                              shifted[:, wb - LANE:])


def _sample_tables(tbl_flat, past, dec_seq, wb):
    q_pos = past + jnp.arange(SUB, dtype=jnp.int32)
    nc = past // CMP_STRIDE
    c_end = jnp.arange(nc, dtype=jnp.int32) * CMP_STRIDE + (CMP_BLOCK - 1)
    slc_keys = jnp.arange(past + LANE, dtype=jnp.int32)
    win_pos = jnp.concatenate([past - wb + jnp.arange(wb, dtype=jnp.int32), past + jnp.arange(LANE, dtype=jnp.int32)])
    big = 1 << 30

    def table(key_pos, hi):
        t = _bias_tables(tbl_flat, (q_pos[:, None] - key_pos[None, :])[None], hi)
        c = t.shape[-1]
        return t.reshape(KV_HEADS, GQA_GROUP, SUB, c).transpose(1, 0, 2, 3).reshape(GQA_GROUP * KV_HEADS * SUB, c)

    bcmp = table(c_end, big)
    bslc = table(slc_keys, big)
    bwin = table(win_pos, WINDOW)
    def kill_pad(b):
        lane = jnp.arange(b.shape[1]) - (b.shape[1] - LANE)
        return jnp.where((lane >= dec_seq)[None, :], NEG, b)
    bslc, bwin = kill_pad(bslc), kill_pad(bwin)
    et = (slc_keys[None, :] // SLC_BLOCK == jnp.arange(LANE, dtype=jnp.int32)[:, None]).astype(BF16)
    c_start = jnp.arange(nc, dtype=jnp.int32) * CMP_STRIDE
    s_start = jnp.arange(LANE, dtype=jnp.int32) * SLC_BLOCK
    ov = ((c_start[:, None] < s_start[None, :] + SLC_BLOCK)
          & (c_start[:, None] + CMP_BLOCK > s_start[None, :])).astype(BF16)
    return bcmp, bslc, bwin, et, ov


def _nsa_sample_call(page_table_flat, q, gates, ks_new, kw_new, ab_pages, slc_pages, win_buf, tables, layer, n_phys,
                     dec_seq):
    bcmp, bslc, bwin, et, ov = tables
    B = q.shape[0]
    n_pages = page_table_flat.shape[0] // B
    page = slc_pages.shape[2]
    past = n_pages * page
    wb = win_buf.shape[2]
    nb = SAMPLE_BATCHES
    row3 = lambda a: pl.BlockSpec((nb,) + a.shape[1:], lambda b, pt: (b, 0, 0))
    full = lambda a: pl.BlockSpec(a.shape, lambda b, pt: (0, 0))
    slots = [(bb, p) for bb in range(nb) for p in range(n_pages)]
    ab_specs = [pl.BlockSpec((1,) + ab_pages.shape[1:],
                             lambda b, pt, bb=bb, p=p: (pt[(b * nb + bb) * n_pages + p], 0, 0)) for bb, p in slots]
    slc_specs = [pl.BlockSpec((1,) + slc_pages.shape[1:],
                              lambda b, pt, bb=bb, p=p: (layer * n_phys + pt[(b * nb + bb) * n_pages + p], 0, 0))
                 for bb, p in slots]
    return pl.pallas_call(
        functools.partial(_nsa_sample_kernel, n_pages=n_pages, past=past, dec_seq=dec_seq),
        grid_spec=pltpu.PrefetchScalarGridSpec(
            num_scalar_prefetch=1,
            grid=(B // nb,),
            in_specs=[row3(q), row3(gates), row3(ks_new), row3(kw_new)] + ab_specs + slc_specs
                     + [pl.BlockSpec((nb, 2 * KV_W, wb), lambda b, pt: (layer * (B // nb) + b, 0, 0)),
                        full(bcmp), full(bslc), full(bwin), full(et), full(ov)],
            out_specs=[pl.BlockSpec((nb, SUB, NSA_W), lambda b, pt: (b, 0, 0)),
                       pl.BlockSpec((nb, 2 * KV_W, wb), lambda b, pt: (b, 0, 0))],
        ),
        out_shape=[jax.ShapeDtypeStruct((B, SUB, NSA_W), F32), jax.ShapeDtypeStruct((B, 2 * KV_W, wb), F32)],
        compiler_params=pltpu.CompilerParams(dimension_semantics=("parallel",), vmem_limit_bytes=VMEM_LIMIT),
        name="nsa_sample",
    )(page_table_flat, q, gates, ks_new, kw_new, *([ab_pages] * len(slots)), *([slc_pages] * len(slots)), win_buf,
      bcmp, bslc, bwin, et, ov)


CONV_HALO = 32
POOL_HALO = 16
MIX_ROWS = 512


def _mix_kernel(glu_ref, glu_halo_ref, glu_hist_ref, u_ref, u_halo_ref, u_hist_ref, dw_ref, cb_ref, lng_ref, lnb_ref,
                wp_ref, ps_ref, conv_ref, pool_ref, xs_ref, us_ref, y_ref, d_ref, *, pos0, ns):
    j = pl.program_id(1)
    tq = glu_ref.shape[0] // ns
    pos = pos0 + j * tq + lax.broadcasted_iota(jnp.int32, (tq, 1), 0)
    grp = lax.broadcasted_iota(jnp.int32, (tq, POOL_W), 1) // POOL_GROUP_W

    def one_sequence(s, rows):
        @pl.when(j == 0)
        def _():
            xs_ref[:CONV_HALO, :] = glu_hist_ref[s]
            us_ref[:POOL_HALO, :] = u_hist_ref[s]

        @pl.when(j > 0)
        def _():
            xs_ref[:CONV_HALO, :] = glu_halo_ref[...]
            us_ref[:POOL_HALO, :] = u_halo_ref[...]

        u = u_ref[rows, :]
        xs_ref[CONV_HALO:, :] = glu_ref[rows, :]
        us_ref[POOL_HALO:, :] = u

        base = CONV_HALO - CONV_BUF
        y = xs_ref[base:base + tq, :] * dw_ref[0:1, :]
        for k in range(1, CONV_WIDTH):
            y = y + xs_ref[base + k:base + k + tq, :] * dw_ref[k:k + 1, :]
        y_ref[rows, :] = y

        run = u
        d = jnp.zeros_like(u)
        done = 1
        for gi, w in enumerate(POOL_WINDOWS):
            for back in range(done, w):
                run = run + us_ref[POOL_HALO - back:POOL_HALO - back + tq, :]
            done = w
            cnt = jnp.minimum(w, pos + 1).astype(F32)
            d = jnp.where(grp == gi, run / cnt - u, d)
        d_ref[rows, :] = d

    if ns == 1:
        one_sequence(0, slice(None))
    else:
        def body(s, carry):
            one_sequence(s, pl.ds(pl.multiple_of(s * tq, SUB), tq))
            return carry

        lax.fori_loop(0, ns, body, 0)

    y = y_ref[...] + cb_ref[...]
    mu = jnp.mean(y, axis=-1, keepdims=True)
    yc = y - mu
    var = jnp.mean(yc * yc, axis=-1, keepdims=True)
    conv_ref[...] = jax.nn.silu(yc * lax.rsqrt(var + EPS) * lng_ref[...] + lnb_ref[...])
    yp = jnp.dot(d_ref[...].astype(BF16), wp_ref[...], preferred_element_type=F32)
    pool_ref[...] = yp * ps_ref[...]


def _mix(glu, u, glu_hist, u_hist, dw, cb, lng, lnb, wp_bd, ps, B, T, tq, pos0):
    nt = T // tq
    ns = min(B, MIX_ROWS // tq) if nt == 1 else 1
    row = pl.BlockSpec((ns * tq, CONV_W), lambda b, j: (b * nt + j, 0))
    full = lambda a: pl.BlockSpec(a.shape, lambda b, j: (0, 0))

    def halo(h):
        return pl.BlockSpec((h, CONV_W), lambda b, j: (jnp.maximum(b * (T // h) + j * (tq // h) - 1, 0), 0))

    hist = lambda h: pl.BlockSpec((ns, h, CONV_W), lambda b, j: (b, 0, 0))
    if nt == 1:
        glu_halo_arr, u_halo_arr = glu_hist.reshape(-1, CONV_W), u_hist.reshape(-1, POOL_W)
        halo_c = pl.BlockSpec((CONV_HALO, CONV_W), lambda b, j: (0, 0))
        halo_p = pl.BlockSpec((POOL_HALO, POOL_W), lambda b, j: (0, 0))
    else:
        glu_halo_arr, u_halo_arr = glu, u
        halo_c, halo_p = halo(CONV_HALO), halo(POOL_HALO)
    return pl.pallas_call(
        functools.partial(_mix_kernel, pos0=pos0, ns=ns),
        grid=(B // ns, nt),
        in_specs=[row, halo_c, hist(CONV_HALO), row, halo_p, hist(POOL_HALO),
                  full(dw), full(cb), full(lng), full(lnb), full(wp_bd), full(ps)],
        out_specs=[row, row],
        out_shape=[jax.ShapeDtypeStruct((B * T, CONV_W), F32), jax.ShapeDtypeStruct((B * T, POOL_W), F32)],
        scratch_shapes=[pltpu.VMEM((CONV_HALO + tq, CONV_W), F32), pltpu.VMEM((POOL_HALO + tq, POOL_W), F32),
                        pltpu.VMEM((ns * tq, CONV_W), F32), pltpu.VMEM((ns * tq, POOL_W), F32)],
        compiler_params=pltpu.CompilerParams(dimension_semantics=("parallel", "parallel"),
                                             vmem_limit_bytes=VMEM_LIMIT),
        name="conv_pool",
    )(glu, glu_halo_arr, glu_hist, u, u_halo_arr, u_hist, dw, cb, lng, lnb, wp_bd, ps)


def _layer(x2d, B, T, l, P, glu_hist, u_hist, pos0, nsa_fn, final, emit_vt):
    n = B * T
    glu, u, q, kvc, kvs, kvw, gates, *vts = _inproj(x2d, P['norm1'][l], P['w_in'][l], min(n, 512), T, emit_vt)
    conv_o, pool_o = _mix(glu, u, glu_hist, u_hist, P['conv_dw'][l], P['conv_b'][l], P['conv_ln_g'][l],
                          P['conv_ln_b'][l], P['pool_w'][l], P['pool_scale'][l], B, T, min(T, MIX_ROWS), pos0)
    nsa_o, extra = nsa_fn(q, gates, kvc, kvs, kvw, *vts[:2], *vts[5:])
    y = _ffn(x2d, conv_o, pool_o, nsa_o, P['w_out'][l], P['norm2'][l], P['w_gu'][l], P['w_down'][l],
             P['final_norm'], min(n, 512), D_FF, final)
    return y, glu, u, kvc, kvs, kvw, extra, vts[2:5]


def kernel(x_prompt, x_sample, cache_cmp_kv, cache_slc_kv, cache_win_kv, state_conv, state_pool, page_table, rel_bias, norm1, w_in, conv_dw, conv_b, conv_ln_g, conv_ln_b, pool_w, pool_scale, cmp_pe_k, cmp_wk, cmp_pe_v, cmp_wv, w_out, norm2, w_gu, w_down, final_norm):
    depth = w_in.shape[0]
    Bp, T, _ = x_prompt.shape
    Bs, S, _ = x_sample.shape
    n_phys, page = cache_cmp_kv.shape[1:3]
    n_pages = page_table.shape[1]
    past = n_pages * page
    wb = cache_win_kv.shape[2]
    tbl_flat = rel_bias.reshape(NUM_BUCKETS, KV_HEADS, GQA_GROUP).transpose(1, 2, 0).reshape(-1)
    eye = jnp.eye(len(POOL_WINDOWS), dtype=F32)
    row = lambda a: a[:, None, :]
    P = dict(
        norm1=row(norm1), norm2=row(norm2), final_norm=final_norm[None, :],
        w_in=jnp.pad(w_in, ((0, 0), (0, 0), (0, IN_W_PAD - IN_W))).astype(BF16),
        w_out=w_out.astype(BF16), w_gu=w_gu.astype(BF16), w_down=w_down.astype(BF16),
        conv_dw=conv_dw, conv_b=row(conv_b), conv_ln_g=row(conv_ln_g), conv_ln_b=row(conv_ln_b),
        pool_w=jnp.einsum('lgce,gh->lgche', pool_w, eye).reshape(depth, POOL_W, POOL_W).astype(BF16),
        pool_scale=row(pool_scale))
    ptables = _prompt_tables(tbl_flat, T)
    stables = _sample_tables(tbl_flat, past, S, wb)
    cmp_pages = jnp.swapaxes(cache_cmp_kv.reshape(depth * n_phys, page, 2 * KV_W), 1, 2)
    slc_pages = jnp.swapaxes(cache_slc_kv.reshape(depth * n_phys, page, 2 * KV_W), 1, 2)
    win_bufs = jnp.swapaxes(cache_win_kv.reshape(depth * Bs, wb, 2 * KV_W), 1, 2)
    pt_flat = page_table.reshape(-1)

    xp = x_prompt.reshape(Bp * T, D_MODEL)
    xs = jnp.pad(x_sample, ((0, 0), (0, SUB - S), (0, 0))).reshape(Bs * SUB, D_MODEL)
    zc = jnp.zeros((Bp, CONV_HALO, CONV_W), F32)
    zp = jnp.zeros((Bp, POOL_HALO, POOL_W), F32)
    kv5 = lambda a, b, t: a.reshape(b, t, 2, KV_HEADS, HEAD_DIM)
    outs_p, outs_s = [], []
    for l in range(depth):
        final = l == depth - 1
        cw = _compress_weights(cmp_pe_k[l], cmp_wk[l], cmp_pe_v[l], cmp_wv[l])

        def nsa_p(q, gates, kvc, kvs, kvw, vst, vwt, ksb, kwb):
            kc, vct = _compress_call(kvc.reshape(Bp, T // CMP_STRIDE, CMP_STRIDE * 2 * KV_W), cw)
            return _nsa_prompt_call(q, gates, kc, vct, ksb, vst, kwb, vwt, ptables, Bp, T), None

        xp, glu, u, _, _, _, _, kvts = _layer(xp, Bp, T, l, P, zc, zp, 0, nsa_p, final, True)
        from_t = lambda a: kv5(jnp.swapaxes(a, 1, 2), Bp, a.shape[2])
        outs_p.append((from_t(kvts[0]), from_t(kvts[1]), from_t(kvts[2][:, :, T - min(WINDOW, T):]),
                       glu.reshape(Bp, T, CONV_W)[:, T - CONV_BUF:], u.reshape(Bp, T, POOL_W)[:, T - POOL_BUF:]))

        def nsa_s(q, gates, kvc, kvs, kvw):
            ab = _compress_pages(cmp_pages, l, n_phys, cw)
            r3 = lambda a: a.reshape(Bs, SUB, a.shape[-1])
            o, new_win = _nsa_sample_call(pt_flat, r3(q), r3(gates), r3(kvs), r3(kvw), ab, slc_pages, win_bufs,
                                          stables, l, n_phys, S)
            return o.reshape(Bs * SUB, NSA_W), new_win

        glu_hist = jnp.pad(state_conv[l], ((0, 0), (CONV_HALO - CONV_BUF, 0), (0, 0)))
        u_hist = jnp.pad(state_pool[l], ((0, 0), (POOL_HALO - POOL_BUF, 0), (0, 0)))
        xs, glu, u, kvc, kvs, kvw, new_win, _ = _layer(xs, Bs, SUB, l, P, glu_hist, u_hist, past, nsa_s, final, False)
        new_conv = jnp.concatenate([state_conv[l], glu.reshape(Bs, SUB, CONV_W)[:, :S]], axis=1)[:, -CONV_BUF:]
        new_pool = jnp.concatenate([state_pool[l], u.reshape(Bs, SUB, POOL_W)[:, :S]], axis=1)[:, -POOL_BUF:]
        outs_s.append((kv5(kvc, Bs, SUB)[:, :S], kv5(kvs, Bs, SUB)[:, :S], kv5(jnp.swapaxes(new_win, 1, 2), Bs, wb), new_conv, new_pool))
    st = lambda outs, i: jnp.stack([o[i] for o in outs])
    return (xp.reshape(Bp, T, D_MODEL), xs.reshape(Bs, SUB, D_MODEL)[:, :S],
            st(outs_p, 0), st(outs_p, 1), st(outs_p, 2), st(outs_p, 3), st(outs_p, 4),
            st(outs_s, 0), st(outs_s, 1), st(outs_s, 2), st(outs_s, 3), st(outs_s, 4))
```
